```python
import jax, jax.numpy as jnp
from jax import lax
import numpy as np

D_MODEL = 1024
BATCH = 16
SEQ = 2048
DEPTH = 2

HEAD_DIM = 64
N_FOX_HEADS = 8
N_SB_HEADS = 8
FOX_WIDTH = N_FOX_HEADS * HEAD_DIM
SB_WIDTH = N_SB_HEADS * HEAD_DIM
LRU_WIDTH = D_MODEL
LRU_BLOCKS = 16
LRU_BLOCK_DIM = LRU_WIDTH // LRU_BLOCKS
CONV_WIDTH = 4
LRU_C = 8.0
N_BRANCHES = 3
N_EXPERTS = 32
TOP_K = 4
D_EXPERT = D_MODEL
SWIGLU_LIMIT = 7.0
SWIGLU_ALPHA = 1.702
Q_BLOCK = 128
EXPERT_BLOCK = 256
LN_EPS = 1e-5
DEEPNORM_ALPHA = (2.0 * DEPTH) ** 0.25
DEEPNORM_BETA = (8.0 * DEPTH) ** -0.25
IN_SIZES = (FOX_WIDTH, FOX_WIDTH, FOX_WIDTH, N_FOX_HEADS, SB_WIDTH, SB_WIDTH, SB_WIDTH, LRU_WIDTH, LRU_WIDTH)
IN_SPLITS = tuple(sum(IN_SIZES[:i + 1]) for i in range(len(IN_SIZES) - 1))
D_IN = sum(IN_SIZES)

kernel_name = 'hybrid_fox_stickbreak_rglru_moe_deepnorm'


def layer_norm(x, g, b):
    xf = x.astype(jnp.float32)
    mu = jnp.mean(xf, axis=-1, keepdims=True)
    var = jnp.mean(jnp.square(xf - mu), axis=-1, keepdims=True)
    y = (xf - mu) * lax.rsqrt(var + LN_EPS)
    return (y * g + b).astype(x.dtype)


def _to_blocks(t):
    b, s = t.shape[0], t.shape[1]
    return jnp.moveaxis(t.reshape((b, s // Q_BLOCK, Q_BLOCK) + t.shape[2:]), 1, 0)


def _from_blocks(o):
    o = jnp.moveaxis(o, 0, 1)
    b, nb, qb, h, d = o.shape
    return o.reshape(b, nb * qb, h * d)


def fox_attention(q, k, v, logf):
    s_len, dh = q.shape[1], q.shape[3]
    nb = s_len // Q_BLOCK
    F = jnp.cumsum(logf.astype(jnp.float32), axis=1)
    Fk = jnp.transpose(F, (0, 2, 1))
    key_pos = jnp.arange(s_len)
    scale = dh ** -0.5

    def block(args):
        qb, Fq, i = args
        q_pos = i * Q_BLOCK + jnp.arange(Q_BLOCK)
        s = jnp.einsum('bqhd,bkhd->bhqk', qb, k, preferred_element_type=jnp.float32) * scale
        s = s + jnp.transpose(Fq, (0, 2, 1))[..., None] - Fk[:, :, None, :]
        s = jnp.where(key_pos[None, :] <= q_pos[:, None], s, -jnp.inf)
        p = jax.nn.softmax(s, axis=-1)
        return jnp.einsum('bhqk,bkhd->bqhd', p.astype(v.dtype), v)

    out = lax.map(block, (_to_blocks(q), _to_blocks(F), jnp.arange(nb)))
    return _from_blocks(out)


def stick_breaking_attention(q, k, v):
    s_len, dh = q.shape[1], q.shape[3]
    nb = s_len // Q_BLOCK
    key_pos = jnp.arange(s_len)
    scale = dh ** -0.5

    def block(args):
        qb, i = args
        q_pos = i * Q_BLOCK + jnp.arange(Q_BLOCK)
        z = jnp.einsum('bqhd,bkhd->bhqk', qb, k, preferred_element_type=jnp.float32) * scale
        strict = key_pos[None, :] < q_pos[:, None]
        log_keep = jnp.where(strict, jax.nn.log_sigmoid(-z), 0.0)
        cs = jnp.cumsum(log_keep, axis=-1)
        after = jnp.minimum(cs[..., -1:] - cs, 0.0)
        w = jnp.where(strict, jnp.exp(jax.nn.log_sigmoid(z) + after), 0.0)
        return jnp.einsum('bhqk,bkhd->bqhd', w.astype(v.dtype), v)

    out = lax.map(block, (_to_blocks(q), jnp.arange(nb)))
    return _from_blocks(out)


def rg_lru_branch(xc, gc, conv_w, conv_b, wa, ba, wx, bx, lam):
    b, s, w = xc.shape
    xconv = lax.conv_general_dilated(
        xc, conv_w[:, None, :], window_strides=(1,), padding=[(CONV_WIDTH - 1, 0)],
        dimension_numbers=('NWC', 'WIO', 'NWC'), feature_group_count=w) + conv_b
    xf = xconv.astype(jnp.float32)
    xb = xf.reshape(b, s, LRU_BLOCKS, LRU_BLOCK_DIM)
    r = jax.nn.sigmoid(jnp.einsum('bsnc,ncd->bsnd', xb, wa.astype(jnp.float32)).reshape(b, s, w) + ba.astype(jnp.float32))
    i_gate = jax.nn.sigmoid(jnp.einsum('bsnc,ncd->bsnd', xb, wx.astype(jnp.float32)).reshape(b, s, w) + bx.astype(jnp.float32))
    log_a = -LRU_C * jax.nn.softplus(-lam.astype(jnp.float32)) * r
    a = jnp.exp(log_a)
    u = jnp.sqrt(-jnp.expm1(2.0 * log_a)) * (i_gate * xf)

    def combine(left, right):
        a_l, b_l = left
        a_r, b_r = right
        return a_l * a_r, a_r * b_l + b_r

    _, h = lax.associative_scan(combine, (a, u), axis=1)
    return (h * jax.nn.gelu(gc.astype(jnp.float32))).astype(xc.dtype)


def token_mixer(h, w_in, b_f, conv_w, conv_b, lru_wa, lru_ba, lru_wx, lru_bx, lru_lambda,
                w_gate, b_gate, w_pa, w_pb, w_pc, w_o):
    b, s, d = h.shape
    z = h @ w_in
    qa, ka, va, fa, qb, kb, vb, xc, gc = jnp.split(z, IN_SPLITS, axis=-1)
    fox_h = lambda t: t.reshape(b, s, N_FOX_HEADS, HEAD_DIM)
    sb_h = lambda t: t.reshape(b, s, N_SB_HEADS, HEAD_DIM)
    logf = jax.nn.log_sigmoid(fa.astype(jnp.float32) + b_f.astype(jnp.float32))
    o_a = fox_attention(fox_h(qa), fox_h(ka), fox_h(va), logf)
    o_b = stick_breaking_attention(sb_h(qb), sb_h(kb), sb_h(vb))
    o_c = rg_lru_branch(xc, gc, conv_w, conv_b, lru_wa, lru_ba, lru_wx, lru_bx, lru_lambda)
    g = jax.nn.sigmoid((h @ w_gate + b_gate).astype(jnp.float32)).astype(h.dtype)
    g = g.reshape(b, s, N_BRANCHES, d)
    merged = g[:, :, 0] * (o_a @ w_pa) + g[:, :, 1] * (o_b @ w_pb) + g[:, :, 2] * (o_c @ w_pc)
    return merged @ w_o


def moe_ffn(h, w_router, b_router, w_gu, b_gu, w_down, b_down):
    b, s, d = h.shape
    t = b * s
    hf = h.reshape(t, d)
    logits = (hf @ w_router).astype(jnp.float32) + b_router.astype(jnp.float32)
    top_v, top_i = lax.top_k(logits, TOP_K)
    gates = jax.nn.softmax(top_v, axis=-1)
    n = t * TOP_K
    flat_e = top_i.reshape(n)
    order = jnp.argsort(flat_e)
    e_sorted = flat_e[order]
    tok_sorted = order // TOP_K
    gate_sorted = gates.reshape(n)[order]
    counts = jnp.bincount(flat_e, length=N_EXPERTS)
    padded = (counts + EXPERT_BLOCK - 1) // EXPERT_BLOCK * EXPERT_BLOCK
    pad_end = jnp.cumsum(padded)
    pad_start = pad_end - padded
    start = jnp.cumsum(counts) - counts
    dest = pad_start[e_sorted] + jnp.arange(n) - start[e_sorted]
    n_blocks = -(-n // EXPERT_BLOCK) + N_EXPERTS
    x_buf = jnp.zeros((n_blocks * EXPERT_BLOCK, d), hf.dtype).at[dest].set(hf[tok_sorted])
    block_e = jnp.minimum(jnp.searchsorted(pad_end, jnp.arange(n_blocks) * EXPERT_BLOCK, side='right'),
                          N_EXPERTS - 1)

    def expert_block(args):
        xb, e = args
        gu = xb @ w_gu[e] + b_gu[e]
        gate, up = jnp.split(gu, 2, axis=-1)
        gate = jnp.minimum(gate, SWIGLU_LIMIT)
        up = jnp.clip(up, -SWIGLU_LIMIT, SWIGLU_LIMIT)
        act = (up + 1.0) * (gate * jax.nn.sigmoid(SWIGLU_ALPHA * gate))
        return act @ w_down[e] + b_down[e]

    y_buf = lax.map(expert_block, (x_buf.reshape(n_blocks, EXPERT_BLOCK, d), block_e)).reshape(-1, d)
    y_tok = y_buf[dest] * gate_sorted[:, None].astype(y_buf.dtype)
    y = jnp.zeros((t, d), h.dtype).at[tok_sorted].add(y_tok.astype(h.dtype))
    return y.reshape(b, s, d)


def setup_inputs(seed: int = 0) -> dict:
    key = jax.random.key(seed)
    ks = jax.random.split(key, 32)
    f32 = jnp.float32
    nrm = lambda k, shape, sc: jax.random.normal(k, shape, f32) * sc
    L, D = DEPTH, D_MODEL
    a0 = jax.random.uniform(ks[14], (L, LRU_WIDTH), f32, minval=0.9, maxval=0.999)
    p0 = a0 ** (1.0 / LRU_C)
    lru_lambda = jnp.log(p0) - jnp.log1p(-p0)
    return {
        'x': nrm(ks[0], (BATCH, SEQ, D), 1.0),
        'c': nrm(ks[1], (BATCH, D), 1.0),
        'w_ada': nrm(ks[2], (L, D, 6 * D), 0.2 * D ** -0.5),
        'b_ada': nrm(ks[3], (L, 6 * D), 0.01),
        'ln1_g': 1.0 + nrm(ks[4], (L, D), 0.02),
        'ln1_b': nrm(ks[5], (L, D), 0.02),
        'w_in': nrm(ks[6], (L, D, D_IN), D ** -0.5),
        'b_f': jax.random.uniform(ks[7], (L, N_FOX_HEADS), f32, minval=1.0, maxval=4.0),
        'conv_w': nrm(ks[8], (L, CONV_WIDTH, LRU_WIDTH), CONV_WIDTH ** -0.5),
        'conv_b': nrm(ks[9], (L, LRU_WIDTH), 0.01),
        'lru_wa': nrm(ks[10], (L, LRU_BLOCKS, LRU_BLOCK_DIM, LRU_BLOCK_DIM), LRU_BLOCK_DIM ** -0.5),
        'lru_ba': nrm(ks[11], (L, LRU_WIDTH), 0.01),
        'lru_wx': nrm(ks[12], (L, LRU_BLOCKS, LRU_BLOCK_DIM, LRU_BLOCK_DIM), LRU_BLOCK_DIM ** -0.5),
        'lru_bx': nrm(ks[13], (L, LRU_WIDTH), 0.01),
        'lru_lambda': lru_lambda,
        'w_gate': nrm(ks[15], (L, D, N_BRANCHES * D), D ** -0.5),
        'b_gate': nrm(ks[16], (L, N_BRANCHES * D), 0.01),
        'w_pa': nrm(ks[17], (L, FOX_WIDTH, D), FOX_WIDTH ** -0.5),
        'w_pb': nrm(ks[18], (L, SB_WIDTH, D), SB_WIDTH ** -0.5),
        'w_pc': nrm(ks[19], (L, LRU_WIDTH, D), LRU_WIDTH ** -0.5),
        'w_o': nrm(ks[20], (L, D, D), DEEPNORM_BETA * D ** -0.5),
        'ln2_g': 1.0 + nrm(ks[21], (L, D), 0.02),
        'ln2_b': nrm(ks[22], (L, D), 0.02),
        'w_router': nrm(ks[23], (L, D, N_EXPERTS), D ** -0.5),
        'b_router': nrm(ks[24], (L, N_EXPERTS), 0.01),
        'w_gu': nrm(ks[25], (L, N_EXPERTS, D, 2 * D_EXPERT), D ** -0.5),
        'b_gu': nrm(ks[26], (L, N_EXPERTS, 2 * D_EXPERT), 0.01),
        'w_down': nrm(ks[27], (L, N_EXPERTS, D_EXPERT, D), DEEPNORM_BETA * D_EXPERT ** -0.5),
        'b_down': nrm(ks[28], (L, N_EXPERTS, D), 0.01),
    }


def reference(x, c, w_ada, b_ada, ln1_g, ln1_b, w_in, b_f, conv_w, conv_b, lru_wa, lru_ba, lru_wx,
              lru_bx, lru_lambda, w_gate, b_gate, w_pa, w_pb, w_pc, w_o, ln2_g, ln2_b, w_router,
              b_router, w_gu, b_gu, w_down, b_down):
    cond = jax.nn.silu(c)
    for l in range(DEPTH):
        ada = (cond @ w_ada[l] + b_ada[l])[:, None, :]
        shift1, scale1, gate1, shift2, scale2, gate2 = jnp.split(ada, 6, axis=-1)
        h = x * (1.0 + scale1) + shift1
        y = token_mixer(h, w_in[l], b_f[l], conv_w[l], conv_b[l], lru_wa[l], lru_ba[l], lru_wx[l],
                        lru_bx[l], lru_lambda[l], w_gate[l], b_gate[l], w_pa[l], w_pb[l], w_pc[l], w_o[l])
        x = layer_norm(DEEPNORM_ALPHA * x + (1.0 + gate1) * y, ln1_g[l], ln1_b[l])
        h = x * (1.0 + scale2) + shift2
        y = moe_ffn(h, w_router[l], b_router[l], w_gu[l], b_gu[l], w_down[l], b_down[l])
        x = layer_norm(DEEPNORM_ALPHA * x + (1.0 + gate2) * y, ln2_g[l], ln2_b[l])
    return x
```

```python
import functools

import jax
import jax.numpy as jnp
from jax import lax
from jax.experimental import pallas as pl
from jax.experimental.pallas import tpu as pltpu

D_MODEL = 1024
DEPTH = 2
HEAD_DIM = 64
N_HEADS = 8
ATT_WIDTH = N_HEADS * HEAD_DIM
LRU_WIDTH = D_MODEL
LRU_BLOCKS = 16
CONV_WIDTH = 4
LRU_C = 8.0
N_EXPERTS = 32
TOP_K = 4
SWIGLU_LIMIT = 7.0
SWIGLU_ALPHA = 1.702
LN_EPS = 1e-5
DEEPNORM_ALPHA = (2.0 * DEPTH) ** 0.25
ATT_SCALE = HEAD_DIM ** -0.5

LANES = 128
MXU_DTYPE = jnp.bfloat16
F32 = jnp.float32

ROW_TILE = 512
ATT_TILE = 256
LRU_TILE = 256
ROUTE_TILE = 512
EXPERT_BLOCK = 256
MOVE_TILE = 256
VMEM_LIMIT = 56 * 1024 * 1024


def _params(n_axes, vmem=VMEM_LIMIT):
    return pltpu.CompilerParams(
        dimension_semantics=("arbitrary",) * n_axes, vmem_limit_bytes=vmem)


def _log_sigmoid(x):
    return jnp.minimum(x, 0.0) - jnp.log1p(jnp.exp(-jnp.abs(x)))


def _softplus(x):
    return jnp.maximum(x, 0.0) + jnp.log1p(jnp.exp(-jnp.abs(x)))


def _layer_norm(v, g, b):
    mu = jnp.mean(v, axis=-1, keepdims=True)
    d = v - mu
    var = jnp.mean(d * d, axis=-1, keepdims=True)
    return d * lax.rsqrt(var + LN_EPS) * g + b


def _split3(x):
    hi = x.astype(MXU_DTYPE)
    r1 = x - hi.astype(F32)
    mid = r1.astype(MXU_DTYPE)
    lo = (r1 - mid.astype(F32)).astype(MXU_DTYPE)
    return hi, mid, lo


def _ada_kernel(c_ref, w_ref, b_ref, o_ref):
    c = c_ref[...]
    cond = c * jax.nn.sigmoid(c)
    o_ref[...] = jnp.dot(cond.astype(MXU_DTYPE), w_ref[...].astype(MXU_DTYPE),
                         preferred_element_type=F32) + b_ref[...]


def _ada(c, w_ada, b_ada):
    n_layers, d, n6 = w_ada.shape
    b = c.shape[0]
    tn = n6 // 4
    return pl.pallas_call(
        _ada_kernel,
        grid=(n_layers, n6 // tn),
        in_specs=[
            pl.BlockSpec((b, d), lambda l, j: (0, 0)),
            pl.BlockSpec((None, d, tn), lambda l, j: (l, 0, j)),
            pl.BlockSpec((None, 1, tn), lambda l, j: (l, 0, j)),
        ],
        out_specs=pl.BlockSpec((None, b, tn), lambda l, j: (l, 0, j)),
        out_shape=jax.ShapeDtypeStruct((n_layers, b, n6), F32),
        compiler_params=_params(2),
        name="ada",
    )(c, w_ada, b_ada.reshape(n_layers, 1, n6))


def _inproj_kernel(x_ref, ada_ref, w_ref, bf_ref, tri_ref,
                   qa_ref, ka_ref, va_ref, qb_ref, kb_ref, vb_ref, xc_ref, gc_ref,
                   fcol_ref, frow_ref, carry_ref, *, tiles_per_seq):
    i = pl.program_id(0)
    tm = x_ref.shape[0]
    shift = ada_ref[0:1, :]
    scale = ada_ref[1:2, :]
    h = (x_ref[...] * (1.0 + scale) + shift).astype(MXU_DTYPE)
    off = 0
    for ref in (qa_ref, ka_ref, va_ref, qb_ref, kb_ref, vb_ref, xc_ref, gc_ref):
        width = ref.shape[1]
        ref[...] = jnp.dot(h, w_ref[:, off:off + width],
                           preferred_element_type=F32).astype(ref.dtype)
        off += width
    zf = jnp.dot(h, w_ref[:, off:off + LANES], preferred_element_type=F32) + bf_ref[...]
    logf = _log_sigmoid(zf)

    @pl.when(i % tiles_per_seq == 0)
    def _():
        carry_ref[...] = jnp.zeros_like(carry_ref)

    tri = tri_ref[...]
    f_cum = carry_ref[0:1, :]
    for piece in _split3(logf):
        f_cum = f_cum + jnp.dot(tri, piece, preferred_element_type=F32)
    carry_ref[0:1, :] = f_cum[tm - 1:tm, :]
    fcol_ref[...] = f_cum
    frow_ref[...] = f_cum.T[0:N_HEADS, :]


def _inproj(x2d, ada_l, w_in_p, bf_pad, seq):
    t, d = x2d.shape
    tm = min(ROW_TILE, seq)
    tiles_per_seq = seq // tm
    n_batch = t // seq
    tri = (lax.broadcasted_iota(jnp.int32, (tm, tm), 1)
           <= lax.broadcasted_iota(jnp.int32, (tm, tm), 0)).astype(MXU_DTYPE)
    row = lambda i: (i, 0)
    att = jax.ShapeDtypeStruct((t, ATT_WIDTH), MXU_DTYPE)
    wide = jax.ShapeDtypeStruct((t, LRU_WIDTH), F32)
    att_spec = pl.BlockSpec((tm, ATT_WIDTH), row)
    wide_spec = pl.BlockSpec((tm, LRU_WIDTH), row)
    return pl.pallas_call(
        functools.partial(_inproj_kernel, tiles_per_seq=tiles_per_seq),
        grid=(t // tm,),
        in_specs=[
            pl.BlockSpec((tm, d), row),
            pl.BlockSpec((None, 6, d), lambda i: (i // tiles_per_seq, 0, 0)),
            pl.BlockSpec(w_in_p.shape, lambda i: (0, 0)),
            pl.BlockSpec((1, LANES), lambda i: (0, 0)),
            pl.BlockSpec((tm, tm), lambda i: (0, 0)),
        ],
        out_specs=[att_spec] * 6 + [wide_spec] * 2 + [
            pl.BlockSpec((tm, LANES), row),
            pl.BlockSpec((None, N_HEADS, tm),
                         lambda i: (i // tiles_per_seq, 0, i % tiles_per_seq)),
        ],
        out_shape=[att] * 6 + [wide] * 2 + [
            jax.ShapeDtypeStruct((t, LANES), F32),
            jax.ShapeDtypeStruct((n_batch, N_HEADS, seq), F32),
        ],
        scratch_shapes=[pltpu.VMEM((8, LANES), F32)],
        compiler_params=_params(1),
        name="inproj",
    )(x2d, ada_l, w_in_p, bf_pad, tri)


def _head_query(q2, c):
    lane = lax.broadcasted_iota(jnp.int32, (1, LANES), 1)
    in_head = (lane >= c * HEAD_DIM) & (lane < (c + 1) * HEAD_DIM)
    return jnp.where(in_head, q2, jnp.zeros_like(q2)) * ATT_SCALE


def _qk(qm, k2):
    return lax.dot_general(qm, k2, (((1,), (1,)), ((), ())), preferred_element_type=F32)


def _fox_kernel(q_ref, k_ref, v_ref, fcol_ref, frow_ref, o_ref):
    i = pl.program_id(1)
    tq = q_ref.shape[0]
    row = lax.broadcasted_iota(jnp.int32, (tq, tq), 0)
    col = lax.broadcasted_iota(jnp.int32, (tq, tq), 1)
    causal = col <= row
    lane = lax.broadcasted_iota(jnp.int32, (1, LANES), 1)
    for p in range(N_HEADS // 2):
        cs = slice(p * LANES, (p + 1) * LANES)
        q2 = q_ref[:, cs]
        outs = []
        for c in range(2):
            head = 2 * p + c
            qm = _head_query(q2, c)
            fq = fcol_ref[:, head:head + 1]

            def block(j, carry, masked, qm=qm, fq=fq, head=head, cs=cs):
                m, l, acc = carry
                start = pl.multiple_of(j * tq, tq)
                k2 = k_ref[pl.ds(start, tq), cs]
                v2 = v_ref[pl.ds(start, tq), cs]
                fk = frow_ref[head, pl.ds(j, 1), :]
                s = _qk(qm, k2) + (fq - fk)
                if masked:
                    s = jnp.where(causal, s, -jnp.inf)
                m_new = jnp.maximum(m, jnp.max(s, axis=1, keepdims=True))
                alpha = jnp.exp(m - m_new)
                pm = jnp.exp(s - m_new)
                l = alpha * l + jnp.sum(pm, axis=1, keepdims=True)
                acc = alpha * acc + jnp.dot(pm.astype(MXU_DTYPE), v2,
                                            preferred_element_type=F32)
                return m_new, l, acc

            init = (jnp.full((tq, 1), -jnp.inf, F32), jnp.zeros((tq, 1), F32),
                    jnp.zeros((tq, LANES), F32))
            carry = lax.fori_loop(0, i, lambda j, cr, blk=block: blk(j, cr, False), init)
            _, l, acc = block(i, carry, True)
            outs.append(acc / l)
        o_ref[:, cs] = jnp.where(lane < HEAD_DIM, outs[0], outs[1]).astype(o_ref.dtype)


def _fox(q, k, v, fcol, frow4, seq):
    t = q.shape[0]
    n_batch = t // seq
    tq = min(ATT_TILE, seq)
    nq = seq // tq
    q_spec = pl.BlockSpec((tq, ATT_WIDTH), lambda b, i: (b * nq + i, 0))
    kv_spec = pl.BlockSpec((seq, ATT_WIDTH), lambda b, i: (b, 0))
    return pl.pallas_call(
        _fox_kernel,
        grid=(n_batch, nq),
        in_specs=[
            q_spec, kv_spec, kv_spec,
            pl.BlockSpec((tq, LANES), lambda b, i: (b * nq + i, 0)),
            pl.BlockSpec((None, N_HEADS, nq, tq), lambda b, i: (b, 0, 0, 0)),
        ],
        out_specs=q_spec,
        out_shape=jax.ShapeDtypeStruct((t, ATT_WIDTH), MXU_DTYPE),
        compiler_params=_params(2),
        name="fox_attention",
    )(q, k, v, fcol, frow4)


def _sb_kernel(q_ref, k_ref, v_ref, suf_ref, o_ref):
    i = pl.program_id(1)
    tq = q_ref.shape[0]
    row = lax.broadcasted_iota(jnp.int32, (tq, tq), 0)
    col = lax.broadcasted_iota(jnp.int32, (tq, tq), 1)
    strict = col < row
    lane = lax.broadcasted_iota(jnp.int32, (1, LANES), 1)
    suf = suf_ref[...]
    for p in range(N_HEADS // 2):
        cs = slice(p * LANES, (p + 1) * LANES)
        q2 = q_ref[:, cs]
        outs = []
        for c in range(2):
            qm = _head_query(q2, c)

            def block(j, carry, masked, qm=qm, cs=cs):
                later, acc = carry
                start = pl.multiple_of(j * tq, tq)
                k2 = k_ref[pl.ds(start, tq), cs]
                v2 = v_ref[pl.ds(start, tq), cs]
                z = _qk(qm, k2)
                soft = jnp.log1p(jnp.exp(-jnp.abs(z)))
                log_beta = jnp.minimum(z, 0.0) - soft
                log_keep = -jnp.maximum(z, 0.0) - soft
                if masked:
                    log_keep = jnp.where(strict, log_keep, 0.0)
                hi = log_keep.astype(MXU_DTYPE)
                lo = (log_keep - hi.astype(F32)).astype(MXU_DTYPE)
                after = (jnp.dot(hi, suf, preferred_element_type=F32)
                         + jnp.dot(lo, suf, preferred_element_type=F32) + later)
                w = jnp.exp(log_beta + jnp.minimum(after, 0.0))
                if masked:
                    w = jnp.where(strict, w, 0.0)
                acc = acc + jnp.dot(w.astype(MXU_DTYPE), v2, preferred_element_type=F32)
                later = later + jnp.sum(log_keep, axis=1, keepdims=True)
                return later, acc

            carry = block(i, (jnp.zeros((tq, 1), F32), jnp.zeros((tq, LANES), F32)), True)
            carry = lax.fori_loop(
                0, i, lambda s, cr, blk=block: blk(i - 1 - s, cr, False), carry)
            outs.append(carry[1])
        o_ref[:, cs] = jnp.where(lane < HEAD_DIM, outs[0], outs[1]).astype(o_ref.dtype)


def _sb(q, k, v, seq):
    t = q.shape[0]
    n_batch = t // seq
    tq = min(ATT_TILE, seq)
    nq = seq // tq
    suf = (lax.broadcasted_iota(jnp.int32, (tq, tq), 0)
           > lax.broadcasted_iota(jnp.int32, (tq, tq), 1)).astype(MXU_DTYPE)
    q_spec = pl.BlockSpec((tq, ATT_WIDTH), lambda b, i: (b * nq + i, 0))
    kv_spec = pl.BlockSpec((seq, ATT_WIDTH), lambda b, i: (b, 0))
    return pl.pallas_call(
        _sb_kernel,
        grid=(n_batch, nq),
        in_specs=[q_spec, kv_spec, kv_spec, pl.BlockSpec((tq, tq), lambda b, i: (0, 0))],
        out_specs=q_spec,
        out_shape=jax.ShapeDtypeStruct((t, ATT_WIDTH), MXU_DTYPE),
        compiler_params=_params(2),
        name="sb_attention",
    )(q, k, v, suf)


def _lru_kernel(xc_ref, gc_ref, cw_ref, cb_ref, wa_ref, ba_ref, wx_ref, bx_ref, lam_ref,
                o_ref, ext_ref, a_ref, u_ref, h_ref, state_ref):
    s_idx = pl.program_id(1)
    ts = xc_ref.shape[0]
    halo = 8

    @pl.when(s_idx == 0)
    def _():
        ext_ref[0:halo, :] = jnp.zeros((halo, LRU_WIDTH), F32)
        state_ref[...] = jnp.zeros_like(state_ref)

    ext_ref[halo:halo + ts, :] = xc_ref[...]
    xconv = cb_ref[...]
    for j in range(CONV_WIDTH):
        lo = halo - (CONV_WIDTH - 1) + j
        xconv = xconv + cw_ref[j:j + 1, :] * ext_ref[lo:lo + ts, :]
    ext_ref[0:halo, :] = ext_ref[ts:ts + halo, :]

    xb = xconv.astype(MXU_DTYPE)
    r = jax.nn.sigmoid(jnp.dot(xb, wa_ref[...], preferred_element_type=F32) + ba_ref[...])
    ig = jax.nn.sigmoid(jnp.dot(xb, wx_ref[...], preferred_element_type=F32) + bx_ref[...])
    log_a = (-LRU_C * _softplus(-lam_ref[...])) * r
    a = jnp.exp(log_a)
    a_ref[...] = a
    u_ref[...] = jnp.sqrt(1.0 - a * a) * (ig * xconv)

    def step(t, h):
        h = a_ref[pl.ds(t, 1), :] * h + u_ref[pl.ds(t, 1), :]
        h_ref[pl.ds(t, 1), :] = h
        return h

    state_ref[0:1, :] = lax.fori_loop(0, ts, step, state_ref[0:1, :], unroll=8)
    o_ref[...] = (h_ref[...] * jax.nn.gelu(gc_ref[...])).astype(o_ref.dtype)


def _lru(xc, gc, conv_w, conv_b, wa_bd, ba, wx_bd, bx, lam, seq):
    t, w = xc.shape
    n_batch = t // seq
    ts = min(LRU_TILE, seq)
    ns = seq // ts
    row_spec = pl.BlockSpec((ts, w), lambda b, s: (b * ns + s, 0))
    vec_spec = pl.BlockSpec((1, w), lambda b, s: (0, 0))
    mat_spec = pl.BlockSpec((w, w), lambda b, s: (0, 0))
    return pl.pallas_call(
        _lru_kernel,
        grid=(n_batch, ns),
        in_specs=[row_spec, row_spec,
                  pl.BlockSpec((CONV_WIDTH, w), lambda b, s: (0, 0)), vec_spec,
                  mat_spec, vec_spec, mat_spec, vec_spec, vec_spec],
        out_specs=row_spec,
        out_shape=jax.ShapeDtypeStruct((t, w), MXU_DTYPE),
        scratch_shapes=[pltpu.VMEM((ts + 8, w), F32), pltpu.VMEM((ts, w), F32),
                        pltpu.VMEM((ts, w), F32), pltpu.VMEM((ts, w), F32),
                        pltpu.VMEM((8, w), F32)],
        compiler_params=_params(2),
        name="rg_lru",
    )(xc, gc, conv_w, conv_b, wa_bd, ba, wx_bd, bx, lam)


def _merge_kernel(x_ref, ada_ref, oa_ref, ob_ref, oc_ref, wg_ref, bg_ref, wpa_ref, wpb_ref,
                  wpc_ref, wo_ref, lng_ref, lnb_ref, wr_ref, br_ref,
                  x1_ref, h2_ref, logit_ref):
    d = x_ref.shape[1]
    x = x_ref[...]
    shift1, scale1, gate1 = ada_ref[0:1, :], ada_ref[1:2, :], ada_ref[2:3, :]
    shift2, scale2 = ada_ref[3:4, :], ada_ref[4:5, :]
    h = (x * (1.0 + scale1) + shift1).astype(MXU_DTYPE)
    merged = None
    for n, (o_ref, w_ref) in enumerate(((oa_ref, wpa_ref), (ob_ref, wpb_ref), (oc_ref, wpc_ref))):
        cs = slice(n * d, (n + 1) * d)
        g = jax.nn.sigmoid(jnp.dot(h, wg_ref[:, cs], preferred_element_type=F32) + bg_ref[:, cs])
        term = g * jnp.dot(o_ref[...], w_ref[...], preferred_element_type=F32)
        merged = term if merged is None else merged + term
    y = jnp.dot(merged.astype(MXU_DTYPE), wo_ref[...], preferred_element_type=F32)
    x1 = _layer_norm(DEEPNORM_ALPHA * x + (1.0 + gate1) * y, lng_ref[...], lnb_ref[...])
    x1_ref[...] = x1
    h2 = x1 * (1.0 + scale2) + shift2
    h2_ref[...] = h2
    logit_ref[...] = _qk(wr_ref[...], h2.astype(MXU_DTYPE)) + br_ref[...]


def _merge(x2d, ada_l, o_a, o_b, o_c, w_gate, b_gate, w_pa, w_pb, w_pc, w_o, ln_g, ln_b,
           w_router_t, b_router, seq):
    t, d = x2d.shape
    tm = min(ROW_TILE, seq)
    tiles_per_seq = seq // tm
    row = lambda i: (i, 0)
    whole = lambda a: pl.BlockSpec(a.shape, lambda i: (0,) * a.ndim)
    return pl.pallas_call(
        _merge_kernel,
        grid=(t // tm,),
        in_specs=[
            pl.BlockSpec((tm, d), row),
            pl.BlockSpec((None, 6, d), lambda i: (i // tiles_per_seq, 0, 0)),
            pl.BlockSpec((tm, ATT_WIDTH), row), pl.BlockSpec((tm, ATT_WIDTH), row),
            pl.BlockSpec((tm, LRU_WIDTH), row),
            whole(w_gate), whole(b_gate), whole(w_pa), whole(w_pb), whole(w_pc), whole(w_o),
            whole(ln_g), whole(ln_b), whole(w_router_t), whole(b_router),
        ],
        out_specs=[pl.BlockSpec((tm, d), row), pl.BlockSpec((tm, d), row),
                   pl.BlockSpec((N_EXPERTS, tm), lambda i: (0, i))],
        out_shape=[jax.ShapeDtypeStruct((t, d), F32), jax.ShapeDtypeStruct((t, d), F32),
                   jax.ShapeDtypeStruct((N_EXPERTS, t), F32)],
        compiler_params=_params(1),
        name="merge_outproj_ln",
    )(x2d, ada_l, o_a, o_b, o_c, w_gate, b_gate, w_pa, w_pb, w_pc, w_o, ln_g, ln_b,
      w_router_t, b_router)


def _route_kernel(logit_ref, tri_ref, eid_ref, gate_ref, pos_ref, cnt_ref, carry_ref):
    i = pl.program_id(0)
    tr = logit_ref.shape[1]

    @pl.when(i == 0)
    def _():
        carry_ref[...] = jnp.zeros_like(carry_ref)

    erow = lax.broadcasted_iota(jnp.int32, (N_EXPERTS, tr), 0)
    cur = logit_ref[...]
    vals, ids = [], []
    for _ in range(TOP_K):
        m = jnp.max(cur, axis=0, keepdims=True)
        idx = jnp.min(jnp.where(cur == m, erow, N_EXPERTS), axis=0, keepdims=True)
        vals.append(m)
        ids.append(idx)
        cur = jnp.where(erow == idx, -jnp.inf, cur)
    exps = [jnp.exp(v - vals[0]) for v in vals]
    denom = exps[0] + exps[1] + exps[2] + exps[3]
    chosen = jnp.zeros((N_EXPERTS, tr), F32)
    for idx in ids:
        chosen = chosen + jnp.where(erow == idx, 1.0, 0.0)
    prefix = (jnp.dot(chosen.astype(MXU_DTYPE), tri_ref[...], preferred_element_type=F32)
              + carry_ref[:, 0:1])
    for k in range(TOP_K):
        eid_ref[k:k + 1, :] = ids[k]
        gate_ref[k:k + 1, :] = exps[k] / denom
        pos_ref[k:k + 1, :] = jnp.sum(jnp.where(erow == ids[k], prefix, 0.0), axis=0,
                                      keepdims=True).astype(jnp.int32)
    carry_ref[...] = carry_ref[...] + jnp.sum(chosen, axis=1, keepdims=True)
    cnt_ref[...] = carry_ref[...].astype(jnp.int32)


def _route(logits_t):
    n_e, t = logits_t.shape
    tr = min(ROUTE_TILE, t)
    tri = (lax.broadcasted_iota(jnp.int32, (tr, tr), 0)
           < lax.broadcasted_iota(jnp.int32, (tr, tr), 1)).astype(MXU_DTYPE)
    tok = lambda i: (0, i)
    return pl.pallas_call(
        _route_kernel,
        grid=(t // tr,),
        in_specs=[pl.BlockSpec((n_e, tr), tok), pl.BlockSpec((tr, tr), lambda i: (0, 0))],
        out_specs=[pl.BlockSpec((TOP_K, tr), tok)] * 3
        + [pl.BlockSpec((n_e, LANES), lambda i: (0, 0))],
        out_shape=[jax.ShapeDtypeStruct((TOP_K, t), jnp.int32),
                   jax.ShapeDtypeStruct((TOP_K, t), F32),
                   jax.ShapeDtypeStruct((TOP_K, t), jnp.int32),
                   jax.ShapeDtypeStruct((n_e, LANES), jnp.int32)],
        scratch_shapes=[pltpu.VMEM((n_e, LANES), F32)],
        compiler_params=_params(1),
        name="route_topk",
    )(logits_t, tri)


def _dest_kernel(start_ref, eid_ref, pos_ref, dest_ref):
    eid = eid_ref[...]
    dest = pos_ref[...]
    for e in range(N_EXPERTS):
        dest = jnp.where(eid == e, dest + start_ref[e], dest)
    dest_ref[...] = dest


def _dest(pad_start, eid, pos):
    k, t = eid.shape
    tr = min(ROUTE_TILE, t)
    spec = pl.BlockSpec((k, tr), lambda i, s: (0, i))
    return pl.pallas_call(
        _dest_kernel,
        grid_spec=pltpu.PrefetchScalarGridSpec(
            num_scalar_prefetch=1, grid=(t // tr,), in_specs=[spec, spec], out_specs=spec),
        out_shape=jax.ShapeDtypeStruct((k, t), jnp.int32),
        compiler_params=_params(1),
        name="route_dest",
    )(pad_start, eid, pos)


def _dispatch_kernel(dest_ref, h_ref, init_ref, buf_ref, sem):
    del init_ref
    tm = h_ref.shape[0]

    def row_copy(r, k):
        return pltpu.make_async_copy(h_ref.at[pl.ds(r, 1), :],
                                     buf_ref.at[pl.ds(dest_ref[k, r], 1), :], sem)

    def issue(r, _):
        for k in range(TOP_K):
            row_copy(r, k).start()
        return 0

    def drain(r, _):
        for k in range(TOP_K):
            row_copy(r, k).wait()
        return 0

    lax.fori_loop(0, tm, issue, 0)
    lax.fori_loop(0, tm, drain, 0)


def _dispatch(dest, h2, n_slots):
    t, d = h2.shape
    tm = min(MOVE_TILE, t)
    init = jnp.zeros((n_slots, d), h2.dtype)
    return pl.pallas_call(
        _dispatch_kernel,
        grid=(t // tm,),
        in_specs=[
            pl.BlockSpec((TOP_K, tm), lambda i: (0, i), memory_space=pltpu.SMEM),
            pl.BlockSpec((tm, d), lambda i: (i, 0)),
            pl.BlockSpec(memory_space=pl.ANY),
        ],
        out_specs=pl.BlockSpec(memory_space=pl.ANY),
        out_shape=jax.ShapeDtypeStruct((n_slots, d), h2.dtype),
        scratch_shapes=[pltpu.SemaphoreType.DMA(())],
        input_output_aliases={2: 0},
        compiler_params=_params(1),
        name="moe_dispatch",
    )(dest, h2, init)


def _expert_kernel(be_ref, nused_ref, x_ref, wgu_ref, bgu_ref, wd_ref, bd_ref, y_ref):
    i = pl.program_id(0)
    d_e = wd_ref.shape[0]

    @pl.when(i < nused_ref[0])
    def _():
        x = x_ref[...].astype(MXU_DTYPE)
        gu = jnp.dot(x, wgu_ref[...], preferred_element_type=F32) + bgu_ref[...]
        gate = jnp.minimum(gu[:, :d_e], SWIGLU_LIMIT)
        up = jnp.clip(gu[:, d_e:], -SWIGLU_LIMIT, SWIGLU_LIMIT)
        act = (up + 1.0) * (gate * jax.nn.sigmoid(SWIGLU_ALPHA * gate))
        y_ref[...] = jnp.dot(act.astype(MXU_DTYPE), wd_ref[...],
                             preferred_element_type=F32) + bd_ref[...]

    @pl.when(i >= nused_ref[0])
    def _():
        y_ref[...] = jnp.zeros_like(y_ref)


def _experts(block_e, n_used, x_buf, w_gu, b_gu, w_down, b_down):
    n_slots, d = x_buf.shape
    n_e, _, d2 = w_gu.shape
    d_e = w_down.shape[1]
    n_blocks = n_slots // EXPERT_BLOCK
    return pl.pallas_call(
        _expert_kernel,
        grid_spec=pltpu.PrefetchScalarGridSpec(
            num_scalar_prefetch=2,
            grid=(n_blocks,),
            in_specs=[
                pl.BlockSpec((EXPERT_BLOCK, d), lambda i, be, nu: (i, 0)),
                pl.BlockSpec((None, d, d2), lambda i, be, nu: (be[i], 0, 0)),
                pl.BlockSpec((None, 1, d2), lambda i, be, nu: (be[i], 0, 0)),
                pl.BlockSpec((None, d_e, d), lambda i, be, nu: (be[i], 0, 0)),
                pl.BlockSpec((None, 1, d), lambda i, be, nu: (be[i], 0, 0)),
            ],
            out_specs=pl.BlockSpec((EXPERT_BLOCK, d), lambda i, be, nu: (i, 0)),
        ),
        out_shape=jax.ShapeDtypeStruct((n_slots, d), F32),
        compiler_params=_params(1),
        name="moe_experts",
    )(block_e, n_used, x_buf, w_gu, b_gu.reshape(n_e, 1, d2), w_down,
      b_down.reshape(n_e, 1, d))


def _combine_kernel(dest_ref, x1_ref, ada_ref, gate_ref, lng_ref, lnb_ref, ybuf_ref,
                    o_ref, rows_ref, sem):
    tm = x1_ref.shape[0]

    def row_copy(r, k):
        return pltpu.make_async_copy(ybuf_ref.at[pl.ds(dest_ref[k, r], 1), :],
                                     rows_ref.at[k, pl.ds(r, 1), :], sem)

    def issue(r, _):
        for k in range(TOP_K):
            row_copy(r, k).start()
        return 0

    def drain(r, _):
        for k in range(TOP_K):
            row_copy(r, k).wait()
        return 0

    lax.fori_loop(0, tm, issue, 0)
    lax.fori_loop(0, tm, drain, 0)
    gates = gate_ref[...]
    y = rows_ref[0] * gates[:, 0:1]
    for k in range(1, TOP_K):
        y = y + rows_ref[k] * gates[:, k:k + 1]
    gate2 = ada_ref[5:6, :]
    o_ref[...] = _layer_norm(DEEPNORM_ALPHA * x1_ref[...] + (1.0 + gate2) * y,
                             lng_ref[...], lnb_ref[...])


def _combine(dest, x1, ada_l, gates_tk, ln_g, ln_b, y_buf, seq):
    t, d = x1.shape
    tm = min(MOVE_TILE, seq)
    tiles_per_seq = seq // tm
    return pl.pallas_call(
        _combine_kernel,
        grid=(t // tm,),
        in_specs=[
            pl.BlockSpec((TOP_K, tm), lambda i: (0, i), memory_space=pltpu.SMEM),
            pl.BlockSpec((tm, d), lambda i: (i, 0)),
            pl.BlockSpec((None, 6, d), lambda i: (i // tiles_per_seq, 0, 0)),
            pl.BlockSpec((tm, TOP_K), lambda i: (i, 0)),
            pl.BlockSpec((1, d), lambda i: (0, 0)),
            pl.BlockSpec((1, d), lambda i: (0, 0)),
            pl.BlockSpec(memory_space=pl.ANY),
        ],
        out_specs=pl.BlockSpec((tm, d), lambda i: (i, 0)),
        out_shape=jax.ShapeDtypeStruct((t, d), F32),
        scratch_shapes=[pltpu.VMEM((TOP_K, tm, d), F32), pltpu.SemaphoreType.DMA(())],
        compiler_params=_params(1),
        name="moe_combine_ln",
    )(dest, x1, ada_l, gates_tk, ln_g, ln_b, y_buf)


def _permute_w_in(w_in):
    d = w_in.shape[0]
    a_end = 3 * ATT_WIDTH
    forget = w_in[:, a_end:a_end + N_HEADS]
    forget = jnp.concatenate([forget, jnp.zeros((d, LANES - N_HEADS), w_in.dtype)], axis=1)
    return jnp.concatenate([w_in[:, :a_end], w_in[:, a_end + N_HEADS:], forget],
                           axis=1).astype(MXU_DTYPE)


def _block_diag(w):
    n, c, dd = w.shape
    eye = jnp.eye(n, dtype=w.dtype)
    return (eye[:, None, :, None] * w[:, :, None, :]).reshape(n * c, n * dd).astype(MXU_DTYPE)


def kernel(x, c, w_ada, b_ada, ln1_g, ln1_b, w_in, b_f, conv_w, conv_b, lru_wa, lru_ba, lru_wx,
           lru_bx, lru_lambda, w_gate, b_gate, w_pa, w_pb, w_pc, w_o, ln2_g, ln2_b, w_router,
           b_router, w_gu, b_gu, w_down, b_down):
    n_batch, seq, d = x.shape
    t = n_batch * seq
    n_layers = w_ada.shape[0]
    n_blocks = (t * TOP_K) // EXPERT_BLOCK + N_EXPERTS
    n_slots = n_blocks * EXPERT_BLOCK
    att_tile = min(ATT_TILE, seq)
    vec = lambda a: a.reshape(1, -1)

    ada = _ada(c, w_ada, b_ada).reshape(n_layers, n_batch, 6, d)
    x2d = x.reshape(t, d)
    for l in range(n_layers):
        ada_l = ada[l]
        bf_pad = jnp.concatenate([b_f[l], jnp.zeros((LANES - N_HEADS,), F32)]).reshape(1, LANES)
        qa, ka, va, qb, kb, vb, xc, gc, fcol, frow = _inproj(
            x2d, ada_l, _permute_w_in(w_in[l]), bf_pad, seq)
        frow4 = frow.reshape(n_batch, N_HEADS, seq // att_tile, att_tile)
        o_a = _fox(qa, ka, va, fcol, frow4, seq)
        o_b = _sb(qb, kb, vb, seq)
        o_c = _lru(xc, gc, conv_w[l], vec(conv_b[l]), _block_diag(lru_wa[l]), vec(lru_ba[l]),
                   _block_diag(lru_wx[l]), vec(lru_bx[l]), vec(lru_lambda[l]), seq)
        x1, h2, logits_t = _merge(
            x2d, ada_l, o_a, o_b, o_c, w_gate[l].astype(MXU_DTYPE), vec(b_gate[l]),
            w_pa[l].astype(MXU_DTYPE), w_pb[l].astype(MXU_DTYPE), w_pc[l].astype(MXU_DTYPE),
            w_o[l].astype(MXU_DTYPE), vec(ln1_g[l]), vec(ln1_b[l]),
            w_router[l].T.astype(MXU_DTYPE), b_router[l].reshape(N_EXPERTS, 1), seq)

        eid, gates, pos, cnt = _route(logits_t)
        counts = cnt[:, 0]
        padded = (counts + EXPERT_BLOCK - 1) // EXPERT_BLOCK * EXPERT_BLOCK
        pad_end = jnp.cumsum(padded)
        pad_start = (pad_end - padded).astype(jnp.int32)
        block_e = jnp.minimum(
            jnp.searchsorted(pad_end, jnp.arange(n_blocks) * EXPERT_BLOCK, side="right"),
            N_EXPERTS - 1).astype(jnp.int32)
        n_used = (pad_end[-1:] // EXPERT_BLOCK).astype(jnp.int32)
        dest = _dest(pad_start, eid, pos)

        x_buf = _dispatch(dest, h2, n_slots)
        y_buf = _experts(block_e, n_used, x_buf, w_gu[l].astype(MXU_DTYPE), b_gu[l],
                         w_down[l].astype(MXU_DTYPE), b_down[l])
        x2d = _combine(dest, x1, ada_l, gates.T, vec(ln2_g[l]), vec(ln2_b[l]), y_buf, seq)
    return x2d.reshape(n_batch, seq, d)
```

```python
import functools

import jax
import jax.numpy as jnp
from jax import lax
from jax.experimental import pallas as pl
from jax.experimental.pallas import tpu as pltpu

D_MODEL = 1024
DEPTH = 2
HEAD_DIM = 64
N_HEADS = 8
ATT_WIDTH = N_HEADS * HEAD_DIM
LRU_WIDTH = D_MODEL
LRU_BLOCKS = 16
CONV_WIDTH = 4
LRU_C = 8.0
N_EXPERTS = 32
TOP_K = 4
SWIGLU_LIMIT = 7.0
SWIGLU_ALPHA = 1.702
LN_EPS = 1e-5
DEEPNORM_ALPHA = (2.0 * DEPTH) ** 0.25
ATT_SCALE = HEAD_DIM ** -0.5

LANES = 128
MXU_DTYPE = jnp.bfloat16
F32 = jnp.float32

ROW_TILE = 512
ATT_TILE = 256
ATT_GROUP = 4
LRU_TILE = 256
ROUTE_TILE = 512
EXPERT_BLOCK = 256
MOVE_TILE = 256
VMEM_LIMIT = 56 * 1024 * 1024


def _params(n_axes, vmem=VMEM_LIMIT):
    return pltpu.CompilerParams(
        dimension_semantics=("arbitrary",) * n_axes, vmem_limit_bytes=vmem)


def _log_sigmoid(x):
    return jnp.minimum(x, 0.0) - jnp.log1p(jnp.exp(-jnp.abs(x)))


def _softplus(x):
    return jnp.maximum(x, 0.0) + jnp.log1p(jnp.exp(-jnp.abs(x)))


def _layer_norm(v, g, b):
    mu = jnp.mean(v, axis=-1, keepdims=True)
    d = v - mu
    var = jnp.mean(d * d, axis=-1, keepdims=True)
    return d * lax.rsqrt(var + LN_EPS) * g + b


def _split3(x):
    hi = x.astype(MXU_DTYPE)
    r1 = x - hi.astype(F32)
    mid = r1.astype(MXU_DTYPE)
    lo = (r1 - mid.astype(F32)).astype(MXU_DTYPE)
    return hi, mid, lo


def _ada_kernel(c_ref, w_ref, b_ref, o_ref):
    c = c_ref[...]
    cond = c * jax.nn.sigmoid(c)
    o_ref[...] = jnp.dot(cond.astype(MXU_DTYPE), w_ref[...].astype(MXU_DTYPE),
                         preferred_element_type=F32) + b_ref[...]


def _ada(c, w_ada, b_ada):
    n_layers, d, n6 = w_ada.shape
    b = c.shape[0]
    tn = n6 // 4
    return pl.pallas_call(
        _ada_kernel,
        grid=(n_layers, n6 // tn),
        in_specs=[
            pl.BlockSpec((b, d), lambda l, j: (0, 0)),
            pl.BlockSpec((None, d, tn), lambda l, j: (l, 0, j)),
            pl.BlockSpec((None, 1, tn), lambda l, j: (l, 0, j)),
        ],
        out_specs=pl.BlockSpec((None, b, tn), lambda l, j: (l, 0, j)),
        out_shape=jax.ShapeDtypeStruct((n_layers, b, n6), F32),
        compiler_params=_params(2),
        name="ada",
    )(c, w_ada, b_ada.reshape(n_layers, 1, n6))


def _nt_dot(a, b):
    return lax.dot_general(a, b, (((1,), (1,)), ((), ())), preferred_element_type=F32)


def _inproj_kernel(x_ref, ada_ref, w_ref, wvt_ref, bf_ref, tri_ref,
                   qa_ref, ka_ref, qb_ref, kb_ref, xc_ref, gc_ref, vat_ref, vbt_ref,
                   fcol_ref, frow_ref, carry_ref, *, tiles_per_seq):
    i = pl.program_id(0)
    tm = x_ref.shape[0]
    shift = ada_ref[0:1, :]
    scale = ada_ref[1:2, :]
    h = (x_ref[...] * (1.0 + scale) + shift).astype(MXU_DTYPE)
    off = 0
    for ref in (qa_ref, ka_ref, qb_ref, kb_ref, xc_ref, gc_ref):
        width = ref.shape[1]
        ref[...] = jnp.dot(h, w_ref[:, off:off + width],
                           preferred_element_type=F32).astype(ref.dtype)
        off += width
    vt = _nt_dot(wvt_ref[...], h).astype(MXU_DTYPE)
    chunk = vat_ref.shape[2]
    for n in range(tm // chunk):
        cols = slice(n * chunk, (n + 1) * chunk)
        vat_ref[n] = vt[0:ATT_WIDTH, cols]
        vbt_ref[n] = vt[ATT_WIDTH:2 * ATT_WIDTH, cols]
    zf = jnp.dot(h, w_ref[:, off:off + LANES], preferred_element_type=F32) + bf_ref[...]
    logf = _log_sigmoid(zf)

    @pl.when(i % tiles_per_seq == 0)
    def _():
        carry_ref[...] = jnp.zeros_like(carry_ref)

    tri = tri_ref[...]
    f_cum = carry_ref[0:1, :]
    for piece in _split3(logf):
        f_cum = f_cum + jnp.dot(tri, piece, preferred_element_type=F32)
    carry_ref[0:1, :] = f_cum[tm - 1:tm, :]
    fcol_ref[...] = f_cum
    frow_ref[...] = f_cum.T[0:N_HEADS, :]


def _inproj(x2d, ada_l, w_main, w_vt, bf_pad, seq):
    t, d = x2d.shape
    tm = min(ROW_TILE, seq)
    chunk = min(ATT_TILE, seq)
    tiles_per_seq = seq // tm
    n_batch = t // seq
    tri = (lax.broadcasted_iota(jnp.int32, (tm, tm), 1)
           <= lax.broadcasted_iota(jnp.int32, (tm, tm), 0)).astype(MXU_DTYPE)
    row = lambda i: (i, 0)
    att = jax.ShapeDtypeStruct((t, ATT_WIDTH), MXU_DTYPE)
    att_t = jax.ShapeDtypeStruct((t // chunk, ATT_WIDTH, chunk), MXU_DTYPE)
    wide = jax.ShapeDtypeStruct((t, LRU_WIDTH), F32)
    att_spec = pl.BlockSpec((tm, ATT_WIDTH), row)
    att_t_spec = pl.BlockSpec((tm // chunk, ATT_WIDTH, chunk), lambda i: (i, 0, 0))
    wide_spec = pl.BlockSpec((tm, LRU_WIDTH), row)
    return pl.pallas_call(
        functools.partial(_inproj_kernel, tiles_per_seq=tiles_per_seq),
        grid=(t // tm,),
        in_specs=[
            pl.BlockSpec((tm, d), row),
            pl.BlockSpec((None, 6, d), lambda i: (i // tiles_per_seq, 0, 0)),
            pl.BlockSpec(w_main.shape, lambda i: (0, 0)),
            pl.BlockSpec(w_vt.shape, lambda i: (0, 0)),
            pl.BlockSpec((1, LANES), lambda i: (0, 0)),
            pl.BlockSpec((tm, tm), lambda i: (0, 0)),
        ],
        out_specs=[att_spec] * 4 + [wide_spec] * 2 + [att_t_spec] * 2 + [
            pl.BlockSpec((tm, LANES), row),
            pl.BlockSpec((None, N_HEADS, tm),
                         lambda i: (i // tiles_per_seq, 0, i % tiles_per_seq)),
        ],
        out_shape=[att] * 4 + [wide] * 2 + [att_t] * 2 + [
            jax.ShapeDtypeStruct((t, LANES), F32),
            jax.ShapeDtypeStruct((n_batch, N_HEADS, seq), F32),
        ],
        scratch_shapes=[pltpu.VMEM((8, LANES), F32)],
        compiler_params=_params(1),
        name="inproj",
    )(x2d, ada_l, w_main, w_vt, bf_pad, tri)


def _head_query(q2, c):
    lane = lax.broadcasted_iota(jnp.int32, (1, LANES), 1)
    in_head = (lane >= c * HEAD_DIM) & (lane < (c + 1) * HEAD_DIM)
    return jnp.where(in_head, q2, jnp.zeros_like(q2)) * ATT_SCALE


def _pair_cols(head):
    return slice((head // 2) * LANES, (head // 2 + 1) * LANES)


def _store_heads(o_ref, heads, outs_t):
    sub = lax.broadcasted_iota(jnp.int32, (LANES, 1), 0)
    for n in range(0, len(heads), 2):
        pair_t = jnp.where(sub < HEAD_DIM, outs_t[n], outs_t[n + 1])
        o_ref[:, _pair_cols(heads[n])] = pair_t.T.astype(o_ref.dtype)


def _fox_kernel(q_ref, k_ref, vt_ref, fcol_ref, frow_ref, o_ref):
    i = pl.program_id(1)
    tq = q_ref.shape[0]
    key = lax.broadcasted_iota(jnp.int32, (tq, tq), 0)
    qry = lax.broadcasted_iota(jnp.int32, (tq, tq), 1)
    causal = key <= qry
    for g in range(N_HEADS // ATT_GROUP):
        heads = list(range(g * ATT_GROUP, (g + 1) * ATT_GROUP))
        qms = [_head_query(q_ref[:, _pair_cols(hd)], hd % 2) for hd in heads]
        fqs = [frow_ref[hd, pl.ds(i, 1), :] for hd in heads]

        def block(j, carry, masked, heads=heads, qms=qms, fqs=fqs):
            start = pl.multiple_of(j * tq, tq)
            scores = [_nt_dot(k_ref[pl.ds(start, tq), _pair_cols(hd)], qms[n])
                      for n, hd in enumerate(heads)]
            stats, probs = [], []
            for n, hd in enumerate(heads):
                m, l, _ = carry[n]
                fk = fcol_ref[pl.ds(start, tq), hd:hd + 1]
                s = scores[n] + (fqs[n] - fk)
                if masked:
                    s = jnp.where(causal, s, -jnp.inf)
                m_new = jnp.maximum(m, jnp.max(s, axis=0, keepdims=True))
                alpha = jnp.exp(m - m_new)
                pm = jnp.exp(s - m_new)
                l = alpha * l + jnp.sum(pm, axis=0, keepdims=True)
                stats.append((m_new, l, alpha))
                probs.append(pm.astype(MXU_DTYPE))
            pvs = [jnp.dot(vt_ref[j, _pair_cols(hd), :], probs[n],
                           preferred_element_type=F32) for n, hd in enumerate(heads)]
            return tuple((stats[n][0], stats[n][1], stats[n][2] * carry[n][2] + pvs[n])
                         for n in range(len(heads)))

        init = tuple((jnp.full((1, tq), -jnp.inf, F32), jnp.zeros((1, tq), F32),
                      jnp.zeros((LANES, tq), F32)) for _ in heads)
        carry = lax.fori_loop(0, i, lambda j, cr, blk=block: blk(j, cr, False), init)
        carry = block(i, carry, True)
        _store_heads(o_ref, heads, [acc / l for _, l, acc in carry])


def _att_specs(t, seq):
    n_batch = t // seq
    tq = min(ATT_TILE, seq)
    nq = seq // tq
    q_spec = pl.BlockSpec((tq, ATT_WIDTH), lambda b, i: (b * nq + i, 0))
    k_spec = pl.BlockSpec((seq, ATT_WIDTH), lambda b, i: (b, 0))
    vt_spec = pl.BlockSpec((nq, ATT_WIDTH, tq), lambda b, i: (b, 0, 0))
    return n_batch, tq, nq, q_spec, k_spec, vt_spec


def _fox(q, k, vt, fcol, frow4, seq):
    t = q.shape[0]
    n_batch, tq, nq, q_spec, k_spec, vt_spec = _att_specs(t, seq)
    return pl.pallas_call(
        _fox_kernel,
        grid=(n_batch, nq),
        in_specs=[
            q_spec, k_spec, vt_spec,
            pl.BlockSpec((seq, LANES), lambda b, i: (b, 0)),
            pl.BlockSpec((None, N_HEADS, nq, tq), lambda b, i: (b, 0, 0, 0)),
        ],
        out_specs=q_spec,
        out_shape=jax.ShapeDtypeStruct((t, ATT_WIDTH), MXU_DTYPE),
        compiler_params=_params(2),
        name="fox_attention",
    )(q, k, vt, fcol, frow4)


def _sb_kernel(q_ref, k_ref, vt_ref, suf_ref, o_ref):
    i = pl.program_id(1)
    tq = q_ref.shape[0]
    key = lax.broadcasted_iota(jnp.int32, (tq, tq), 0)
    qry = lax.broadcasted_iota(jnp.int32, (tq, tq), 1)
    strict = key < qry
    suf = suf_ref[...]
    for g in range(N_HEADS // ATT_GROUP):
        heads = list(range(g * ATT_GROUP, (g + 1) * ATT_GROUP))
        qms = [_head_query(q_ref[:, _pair_cols(hd)], hd % 2) for hd in heads]

        def block(j, carry, masked, heads=heads, qms=qms):
            start = pl.multiple_of(j * tq, tq)
            zs = [_nt_dot(k_ref[pl.ds(start, tq), _pair_cols(hd)], qms[n])
                  for n, hd in enumerate(heads)]
            log_betas, his, los, laters = [], [], [], []
            for n in range(len(heads)):
                z = zs[n]
                sp = jnp.maximum(z, 0.0) + jnp.log1p(jnp.exp(-jnp.abs(z)))
                log_betas.append(z - sp)
                if masked:
                    sp = jnp.where(strict, sp, 0.0)
                hi = sp.astype(MXU_DTYPE)
                his.append(hi)
                los.append((sp - hi.astype(F32)).astype(MXU_DTYPE))
                laters.append(carry[n][0] + jnp.sum(sp, axis=0, keepdims=True))
            afters = [jnp.dot(suf, his[n], preferred_element_type=F32)
                      + jnp.dot(suf, los[n], preferred_element_type=F32)
                      for n in range(len(heads))]
            ws = []
            for n in range(len(heads)):
                w = jnp.exp(log_betas[n] - jnp.maximum(afters[n] + carry[n][0], 0.0))
                if masked:
                    w = jnp.where(strict, w, 0.0)
                ws.append(w.astype(MXU_DTYPE))
            return tuple(
                (laters[n], carry[n][1] + jnp.dot(vt_ref[j, _pair_cols(hd), :], ws[n],
                                                  preferred_element_type=F32))
                for n, hd in enumerate(heads))

        init = tuple((jnp.zeros((1, tq), F32), jnp.zeros((LANES, tq), F32)) for _ in heads)
        carry = block(i, init, True)
        carry = lax.fori_loop(
            0, i, lambda s, cr, blk=block: blk(i - 1 - s, cr, False), carry)
        _store_heads(o_ref, heads, [acc for _, acc in carry])


def _sb(q, k, vt, seq):
    t = q.shape[0]
    n_batch, tq, nq, q_spec, k_spec, vt_spec = _att_specs(t, seq)
    suf = (lax.broadcasted_iota(jnp.int32, (tq, tq), 1)
           > lax.broadcasted_iota(jnp.int32, (tq, tq), 0)).astype(MXU_DTYPE)
    return pl.pallas_call(
        _sb_kernel,
        grid=(n_batch, nq),
        in_specs=[q_spec, k_spec, vt_spec, pl.BlockSpec((tq, tq), lambda b, i: (0, 0))],
        out_specs=q_spec,
        out_shape=jax.ShapeDtypeStruct((t, ATT_WIDTH), MXU_DTYPE),
        compiler_params=_params(2),
        name="sb_attention",
    )(q, k, vt, suf)


def _lru_kernel(xc_ref, gc_ref, cw_ref, cb_ref, wa_ref, ba_ref, wx_ref, bx_ref, lam_ref,
                o_ref, ext_ref, a_ref, u_ref, h_ref, state_ref):
    s_idx = pl.program_id(1)
    ts = xc_ref.shape[0]
    halo = 8

    @pl.when(s_idx == 0)
    def _():
        ext_ref[0:halo, :] = jnp.zeros((halo, LRU_WIDTH), F32)
        state_ref[...] = jnp.zeros_like(state_ref)

    ext_ref[halo:halo + ts, :] = xc_ref[...]
    xconv = cb_ref[...]
    for j in range(CONV_WIDTH):
        lo = halo - (CONV_WIDTH - 1) + j
        xconv = xconv + cw_ref[j:j + 1, :] * ext_ref[lo:lo + ts, :]
    ext_ref[0:halo, :] = ext_ref[ts:ts + halo, :]

    xb = xconv.astype(MXU_DTYPE)
    r = jax.nn.sigmoid(jnp.dot(xb, wa_ref[...], preferred_element_type=F32) + ba_ref[...])
    ig = jax.nn.sigmoid(jnp.dot(xb, wx_ref[...], preferred_element_type=F32) + bx_ref[...])
    log_a = (-LRU_C * _softplus(-lam_ref[...])) * r
    a = jnp.exp(log_a)
    a_ref[...] = a
    u_ref[...] = jnp.sqrt(1.0 - a * a) * (ig * xconv)

    def step(t, h):
        h = a_ref[pl.ds(t, 1), :] * h + u_ref[pl.ds(t, 1), :]
        h_ref[pl.ds(t, 1), :] = h
        return h

    state_ref[0:1, :] = lax.fori_loop(0, ts, step, state_ref[0:1, :], unroll=8)
    o_ref[...] = (h_ref[...] * jax.nn.gelu(gc_ref[...])).astype(o_ref.dtype)


def _lru(xc, gc, conv_w, conv_b, wa_bd, ba, wx_bd, bx, lam, seq):
    t, w = xc.shape
    n_batch = t // seq
    ts = min(LRU_TILE, seq)
    ns = seq // ts
    row_spec = pl.BlockSpec((ts, w), lambda b, s: (b * ns + s, 0))
    vec_spec = pl.BlockSpec((1, w), lambda b, s: (0, 0))
    mat_spec = pl.BlockSpec((w, w), lambda b, s: (0, 0))
    return pl.pallas_call(
        _lru_kernel,
        grid=(n_batch, ns),
        in_specs=[row_spec, row_spec,
                  pl.BlockSpec((CONV_WIDTH, w), lambda b, s: (0, 0)), vec_spec,
                  mat_spec, vec_spec, mat_spec, vec_spec, vec_spec],
        out_specs=row_spec,
        out_shape=jax.ShapeDtypeStruct((t, w), MXU_DTYPE),
        scratch_shapes=[pltpu.VMEM((ts + 8, w), F32), pltpu.VMEM((ts, w), F32),
                        pltpu.VMEM((ts, w), F32), pltpu.VMEM((ts, w), F32),
                        pltpu.VMEM((8, w), F32)],
        compiler_params=_params(2),
        name="rg_lru",
    )(xc, gc, conv_w, conv_b, wa_bd, ba, wx_bd, bx, lam)


def _merge_kernel(x_ref, ada_ref, oa_ref, ob_ref, oc_ref, wg_ref, bg_ref, wpa_ref, wpb_ref,
                  wpc_ref, wo_ref, lng_ref, lnb_ref, wr_ref, br_ref,
                  x1_ref, h2_ref, logit_ref):
    d = x_ref.shape[1]
    x = x_ref[...]
    shift1, scale1, gate1 = ada_ref[0:1, :], ada_ref[1:2, :], ada_ref[2:3, :]
    shift2, scale2 = ada_ref[3:4, :], ada_ref[4:5, :]
    h = (x * (1.0 + scale1) + shift1).astype(MXU_DTYPE)
    merged = None
    for n, (o_ref, w_ref) in enumerate(((oa_ref, wpa_ref), (ob_ref, wpb_ref), (oc_ref, wpc_ref))):
        cs = slice(n * d, (n + 1) * d)
        g = jax.nn.sigmoid(jnp.dot(h, wg_ref[:, cs], preferred_element_type=F32) + bg_ref[:, cs])
        term = g * jnp.dot(o_ref[...], w_ref[...], preferred_element_type=F32)
        merged = term if merged is None else merged + term
    y = jnp.dot(merged.astype(MXU_DTYPE), wo_ref[...], preferred_element_type=F32)
    x1 = _layer_norm(DEEPNORM_ALPHA * x + (1.0 + gate1) * y, lng_ref[...], lnb_ref[...])
    x1_ref[...] = x1
    h2 = x1 * (1.0 + scale2) + shift2
    h2_ref[...] = h2
    logit_ref[...] = _nt_dot(wr_ref[...], h2.astype(MXU_DTYPE)) + br_ref[...]


def _merge(x2d, ada_l, o_a, o_b, o_c, w_gate, b_gate, w_pa, w_pb, w_pc, w_o, ln_g, ln_b,
           w_router_t, b_router, seq):
    t, d = x2d.shape
    tm = min(ROW_TILE, seq)
    tiles_per_seq = seq // tm
    row = lambda i: (i, 0)
    whole = lambda a: pl.BlockSpec(a.shape, lambda i: (0,) * a.ndim)
    return pl.pallas_call(
        _merge_kernel,
        grid=(t // tm,),
        in_specs=[
            pl.BlockSpec((tm, d), row),
            pl.BlockSpec((None, 6, d), lambda i: (i // tiles_per_seq, 0, 0)),
            pl.BlockSpec((tm, ATT_WIDTH), row), pl.BlockSpec((tm, ATT_WIDTH), row),
            pl.BlockSpec((tm, LRU_WIDTH), row),
            whole(w_gate), whole(b_gate), whole(w_pa), whole(w_pb), whole(w_pc), whole(w_o),
            whole(ln_g), whole(ln_b), whole(w_router_t), whole(b_router),
        ],
        out_specs=[pl.BlockSpec((tm, d), row), pl.BlockSpec((tm, d), row),
                   pl.BlockSpec((N_EXPERTS, tm), lambda i: (0, i))],
        out_shape=[jax.ShapeDtypeStruct((t, d), F32), jax.ShapeDtypeStruct((t, d), F32),
                   jax.ShapeDtypeStruct((N_EXPERTS, t), F32)],
        compiler_params=_params(1),
        name="merge_outproj_ln",
    )(x2d, ada_l, o_a, o_b, o_c, w_gate, b_gate, w_pa, w_pb, w_pc, w_o, ln_g, ln_b,
      w_router_t, b_router)


def _route_kernel(logit_ref, tri_ref, eid_ref, gate_ref, pos_ref, cnt_ref, carry_ref):
    i = pl.program_id(0)
    tr = logit_ref.shape[1]

    @pl.when(i == 0)
    def _():
        carry_ref[...] = jnp.zeros_like(carry_ref)

    erow = lax.broadcasted_iota(jnp.int32, (N_EXPERTS, tr), 0)
    cur = logit_ref[...]
    vals, ids = [], []
    for _ in range(TOP_K):
        m = jnp.max(cur, axis=0, keepdims=True)
        idx = jnp.min(jnp.where(cur == m, erow, N_EXPERTS), axis=0, keepdims=True)
        vals.append(m)
        ids.append(idx)
        cur = jnp.where(erow == idx, -jnp.inf, cur)
    exps = [jnp.exp(v - vals[0]) for v in vals]
    denom = exps[0] + exps[1] + exps[2] + exps[3]
    chosen = jnp.zeros((N_EXPERTS, tr), F32)
    for idx in ids:
        chosen = chosen + jnp.where(erow == idx, 1.0, 0.0)
    prefix = (jnp.dot(chosen.astype(MXU_DTYPE), tri_ref[...], preferred_element_type=F32)
              + carry_ref[:, 0:1])
    for k in range(TOP_K):
        eid_ref[k:k + 1, :] = ids[k]
        gate_ref[k:k + 1, :] = exps[k] / denom
        pos_ref[k:k + 1, :] = jnp.sum(jnp.where(erow == ids[k], prefix, 0.0), axis=0,
                                      keepdims=True).astype(jnp.int32)
    carry_ref[...] = carry_ref[...] + jnp.sum(chosen, axis=1, keepdims=True)
    cnt_ref[...] = carry_ref[...].astype(jnp.int32)


def _route(logits_t):
    n_e, t = logits_t.shape
    tr = min(ROUTE_TILE, t)
    tri = (lax.broadcasted_iota(jnp.int32, (tr, tr), 0)
           < lax.broadcasted_iota(jnp.int32, (tr, tr), 1)).astype(MXU_DTYPE)
    tok = lambda i: (0, i)
    return pl.pallas_call(
        _route_kernel,
        grid=(t // tr,),
        in_specs=[pl.BlockSpec((n_e, tr), tok), pl.BlockSpec((tr, tr), lambda i: (0, 0))],
        out_specs=[pl.BlockSpec((TOP_K, tr), tok)] * 3
        + [pl.BlockSpec((n_e, LANES), lambda i: (0, 0))],
        out_shape=[jax.ShapeDtypeStruct((TOP_K, t), jnp.int32),
                   jax.ShapeDtypeStruct((TOP_K, t), F32),
                   jax.ShapeDtypeStruct((TOP_K, t), jnp.int32),
                   jax.ShapeDtypeStruct((n_e, LANES), jnp.int32)],
        scratch_shapes=[pltpu.VMEM((n_e, LANES), F32)],
        compiler_params=_params(1),
        name="route_topk",
    )(logits_t, tri)


def _dest_kernel(start_ref, eid_ref, pos_ref, dest_ref):
    eid = eid_ref[...]
    dest = pos_ref[...]
    for e in range(N_EXPERTS):
        dest = jnp.where(eid == e, dest + start_ref[e], dest)
    dest_ref[...] = dest


def _dest(pad_start, eid, pos):
    k, t = eid.shape
    tr = min(ROUTE_TILE, t)
    spec = pl.BlockSpec((k, tr), lambda i, s: (0, i))
    return pl.pallas_call(
        _dest_kernel,
        grid_spec=pltpu.PrefetchScalarGridSpec(
            num_scalar_prefetch=1, grid=(t // tr,), in_specs=[spec, spec], out_specs=spec),
        out_shape=jax.ShapeDtypeStruct((k, t), jnp.int32),
        compiler_params=_params(1),
        name="route_dest",
    )(pad_start, eid, pos)


def _dispatch_kernel(dest_ref, h_ref, init_ref, buf_ref, sem):
    del init_ref
    tm = h_ref.shape[0]

    def row_copy(r, k):
        return pltpu.make_async_copy(h_ref.at[pl.ds(r, 1), :],
                                     buf_ref.at[pl.ds(dest_ref[k, r], 1), :], sem)

    def issue(r, _):
        for k in range(TOP_K):
            row_copy(r, k).start()
        return 0

    def drain(r, _):
        for k in range(TOP_K):
            row_copy(r, k).wait()
        return 0

    lax.fori_loop(0, tm, issue, 0)
    lax.fori_loop(0, tm, drain, 0)


def _dispatch(dest, h2, n_slots):
    t, d = h2.shape
    tm = min(MOVE_TILE, t)
    init = jnp.zeros((n_slots, d), h2.dtype)
    return pl.pallas_call(
        _dispatch_kernel,
        grid=(t // tm,),
        in_specs=[
            pl.BlockSpec((TOP_K, tm), lambda i: (0, i), memory_space=pltpu.SMEM),
            pl.BlockSpec((tm, d), lambda i: (i, 0)),
            pl.BlockSpec(memory_space=pl.ANY),
        ],
        out_specs=pl.BlockSpec(memory_space=pl.ANY),
        out_shape=jax.ShapeDtypeStruct((n_slots, d), h2.dtype),
        scratch_shapes=[pltpu.SemaphoreType.DMA(())],
        input_output_aliases={2: 0},
        compiler_params=_params(1),
        name="moe_dispatch",
    )(dest, h2, init)


def _expert_kernel(be_ref, nused_ref, x_ref, wgu_ref, bgu_ref, wd_ref, bd_ref, y_ref):
    i = pl.program_id(0)
    d_e = wd_ref.shape[0]

    @pl.when(i < nused_ref[0])
    def _():
        x = x_ref[...].astype(MXU_DTYPE)
        gu = jnp.dot(x, wgu_ref[...], preferred_element_type=F32) + bgu_ref[...]
        gate = jnp.minimum(gu[:, :d_e], SWIGLU_LIMIT)
        up = jnp.clip(gu[:, d_e:], -SWIGLU_LIMIT, SWIGLU_LIMIT)
        act = (up + 1.0) * (gate * jax.nn.sigmoid(SWIGLU_ALPHA * gate))
        y_ref[...] = jnp.dot(act.astype(MXU_DTYPE), wd_ref[...],
                             preferred_element_type=F32) + bd_ref[...]

    @pl.when(i >= nused_ref[0])
    def _():
        y_ref[...] = jnp.zeros_like(y_ref)


def _experts(block_e, n_used, x_buf, w_gu, b_gu, w_down, b_down):
    n_slots, d = x_buf.shape
    n_e, _, d2 = w_gu.shape
    d_e = w_down.shape[1]
    n_blocks = n_slots // EXPERT_BLOCK
    return pl.pallas_call(
        _expert_kernel,
        grid_spec=pltpu.PrefetchScalarGridSpec(
            num_scalar_prefetch=2,
            grid=(n_blocks,),
            in_specs=[
                pl.BlockSpec((EXPERT_BLOCK, d), lambda i, be, nu: (i, 0)),
                pl.BlockSpec((None, d, d2), lambda i, be, nu: (be[i], 0, 0)),
                pl.BlockSpec((None, 1, d2), lambda i, be, nu: (be[i], 0, 0)),
                pl.BlockSpec((None, d_e, d), lambda i, be, nu: (be[i], 0, 0)),
                pl.BlockSpec((None, 1, d), lambda i, be, nu: (be[i], 0, 0)),
            ],
            out_specs=pl.BlockSpec((EXPERT_BLOCK, d), lambda i, be, nu: (i, 0)),
        ),
        out_shape=jax.ShapeDtypeStruct((n_slots, d), F32),
        compiler_params=_params(1),
        name="moe_experts",
    )(block_e, n_used, x_buf, w_gu, b_gu.reshape(n_e, 1, d2), w_down,
      b_down.reshape(n_e, 1, d))


def _combine_kernel(dest_ref, x1_ref, ada_ref, gate_ref, lng_ref, lnb_ref, ybuf_ref,
                    o_ref, rows_ref, sem):
    tm = x1_ref.shape[0]

    def row_copy(r, k):
        return pltpu.make_async_copy(ybuf_ref.at[pl.ds(dest_ref[k, r], 1), :],
                                     rows_ref.at[k, pl.ds(r, 1), :], sem)

    def issue(r, _):
        for k in range(TOP_K):
            row_copy(r, k).start()
        return 0

    def drain(r, _):
        for k in range(TOP_K):
            row_copy(r, k).wait()
        return 0

    lax.fori_loop(0, tm, issue, 0)
    lax.fori_loop(0, tm, drain, 0)
    gates = gate_ref[...]
    y = rows_ref[0] * gates[:, 0:1]
    for k in range(1, TOP_K):
        y = y + rows_ref[k] * gates[:, k:k + 1]
    gate2 = ada_ref[5:6, :]
    o_ref[...] = _layer_norm(DEEPNORM_ALPHA * x1_ref[...] + (1.0 + gate2) * y,
                             lng_ref[...], lnb_ref[...])


def _combine(dest, x1, ada_l, gates_tk, ln_g, ln_b, y_buf, seq):
    t, d = x1.shape
    tm = min(MOVE_TILE, seq)
    tiles_per_seq = seq // tm
    return pl.pallas_call(
        _combine_kernel,
        grid=(t // tm,),
        in_specs=[
            pl.BlockSpec((TOP_K, tm), lambda i: (0, i), memory_space=pltpu.SMEM),
            pl.BlockSpec((tm, d), lambda i: (i, 0)),
            pl.BlockSpec((None, 6, d), lambda i: (i // tiles_per_seq, 0, 0)),
            pl.BlockSpec((tm, TOP_K), lambda i: (i, 0)),
            pl.BlockSpec((1, d), lambda i: (0, 0)),
            pl.BlockSpec((1, d), lambda i: (0, 0)),
            pl.BlockSpec(memory_space=pl.ANY),
        ],
        out_specs=pl.BlockSpec((tm, d), lambda i: (i, 0)),
        out_shape=jax.ShapeDtypeStruct((t, d), F32),
        scratch_shapes=[pltpu.VMEM((TOP_K, tm, d), F32), pltpu.SemaphoreType.DMA(())],
        compiler_params=_params(1),
        name="moe_combine_ln",
    )(dest, x1, ada_l, gates_tk, ln_g, ln_b, y_buf)


def _split_w_in(w_in):
    d = w_in.shape[0]
    sizes = (ATT_WIDTH, ATT_WIDTH, ATT_WIDTH, N_HEADS, ATT_WIDTH, ATT_WIDTH, ATT_WIDTH,
             LRU_WIDTH, LRU_WIDTH)
    bounds = [sum(sizes[:n]) for n in range(len(sizes) + 1)]
    qa, ka, va, fa, qb, kb, vb, xc, gc = (
        w_in[:, bounds[n]:bounds[n + 1]] for n in range(len(sizes)))
    forget = jnp.concatenate([fa, jnp.zeros((d, LANES - N_HEADS), w_in.dtype)], axis=1)
    w_main = jnp.concatenate([qa, ka, qb, kb, xc, gc, forget], axis=1).astype(MXU_DTYPE)
    w_vt = jnp.concatenate([va, vb], axis=1).T.astype(MXU_DTYPE)
    return w_main, w_vt


def _block_diag(w):
    n, c, dd = w.shape
    eye = jnp.eye(n, dtype=w.dtype)
    return (eye[:, None, :, None] * w[:, :, None, :]).reshape(n * c, n * dd).astype(MXU_DTYPE)


def kernel(x, c, w_ada, b_ada, ln1_g, ln1_b, w_in, b_f, conv_w, conv_b, lru_wa, lru_ba, lru_wx,
           lru_bx, lru_lambda, w_gate, b_gate, w_pa, w_pb, w_pc, w_o, ln2_g, ln2_b, w_router,
           b_router, w_gu, b_gu, w_down, b_down):
    n_batch, seq, d = x.shape
    t = n_batch * seq
    n_layers = w_ada.shape[0]
    n_blocks = (t * TOP_K) // EXPERT_BLOCK + N_EXPERTS
    n_slots = n_blocks * EXPERT_BLOCK
    att_tile = min(ATT_TILE, seq)
    vec = lambda a: a.reshape(1, -1)

    ada = _ada(c, w_ada, b_ada).reshape(n_layers, n_batch, 6, d)
    x2d = x.reshape(t, d)
    for l in range(n_layers):
        ada_l = ada[l]
        bf_pad = jnp.concatenate([b_f[l], jnp.zeros((LANES - N_HEADS,), F32)]).reshape(1, LANES)
        w_main, w_vt = _split_w_in(w_in[l])
        qa, ka, qb, kb, xc, gc, vat, vbt, fcol, frow = _inproj(
            x2d, ada_l, w_main, w_vt, bf_pad, seq)
        frow4 = frow.reshape(n_batch, N_HEADS, seq // att_tile, att_tile)
        o_a = _fox(qa, ka, vat, fcol, frow4, seq)
        o_b = _sb(qb, kb, vbt, seq)
        o_c = _lru(xc, gc, conv_w[l], vec(conv_b[l]), _block_diag(lru_wa[l]), vec(lru_ba[l]),
                   _block_diag(lru_wx[l]), vec(lru_bx[l]), vec(lru_lambda[l]), seq)
        x1, h2, logits_t = _merge(
            x2d, ada_l, o_a, o_b, o_c, w_gate[l].astype(MXU_DTYPE), vec(b_gate[l]),
            w_pa[l].astype(MXU_DTYPE), w_pb[l].astype(MXU_DTYPE), w_pc[l].astype(MXU_DTYPE),
            w_o[l].astype(MXU_DTYPE), vec(ln1_g[l]), vec(ln1_b[l]),
            w_router[l].T.astype(MXU_DTYPE), b_router[l].reshape(N_EXPERTS, 1), seq)

        eid, gates, pos, cnt = _route(logits_t)
        counts = cnt[:, 0]
        padded = (counts + EXPERT_BLOCK - 1) // EXPERT_BLOCK * EXPERT_BLOCK
        pad_end = jnp.cumsum(padded)
        pad_start = (pad_end - padded).astype(jnp.int32)
        block_first = (jnp.arange(n_blocks) * EXPERT_BLOCK)[:, None]
        block_e = jnp.minimum(jnp.sum(pad_end[None, :] <= block_first, axis=1),
                              N_EXPERTS - 1).astype(jnp.int32)
        n_used = (pad_end[-1:] // EXPERT_BLOCK).astype(jnp.int32)
        dest = _dest(pad_start, eid, pos)

        x_buf = _dispatch(dest, h2, n_slots)
        y_buf = _experts(block_e, n_used, x_buf, w_gu[l].astype(MXU_DTYPE), b_gu[l],
                         w_down[l].astype(MXU_DTYPE), b_down[l])
        x2d = _combine(dest, x1, ada_l, gates.T, vec(ln2_g[l]), vec(ln2_b[l]), y_buf, seq)
    return x2d.reshape(n_batch, seq, d)
```

```python
import functools

import jax
import jax.numpy as jnp
from jax import lax
from jax.experimental import pallas as pl
from jax.experimental.pallas import tpu as pltpu

D_MODEL = 1024
DEPTH = 2
HEAD_DIM = 64
N_HEADS = 8
ATT_WIDTH = N_HEADS * HEAD_DIM
LRU_WIDTH = D_MODEL
LRU_BLOCKS = 16
CONV_WIDTH = 4
LRU_C = 8.0
N_EXPERTS = 32
TOP_K = 4
SWIGLU_LIMIT = 7.0
SWIGLU_ALPHA = 1.702
LN_EPS = 1e-5
DEEPNORM_ALPHA = (2.0 * DEPTH) ** 0.25
ATT_SCALE = HEAD_DIM ** -0.5
LOG2_E = 1.4426950408889634

LANES = 128
MXU_DTYPE = jnp.bfloat16
F32 = jnp.float32

ROW_TILE = 512
ATT_TILE = 256
ATT_GROUP = 8
LRU_TILE = 256
ROUTE_TILE = 512
EXPERT_BLOCK = 256
MOVE_TILE = 256
VMEM_LIMIT = 56 * 1024 * 1024


def _params(n_axes, vmem=VMEM_LIMIT):
    return pltpu.CompilerParams(
        dimension_semantics=("arbitrary",) * n_axes, vmem_limit_bytes=vmem)


def _log_sigmoid(x):
    return jnp.minimum(x, 0.0) - jnp.log1p(jnp.exp(-jnp.abs(x)))


def _softplus(x):
    return jnp.maximum(x, 0.0) + jnp.log1p(jnp.exp(-jnp.abs(x)))


def _layer_norm(v, g, b):
    mu = jnp.mean(v, axis=-1, keepdims=True)
    d = v - mu
    var = jnp.mean(d * d, axis=-1, keepdims=True)
    return d * lax.rsqrt(var + LN_EPS) * g + b


def _pack_halves(x):
    n = x.shape[1] // 2
    bits = lambda v: lax.bitcast_convert_type(v.astype(jnp.bfloat16).astype(F32), jnp.uint32)
    return (bits(x[:, :n]) >> 16) | bits(x[:, n:])


def _unpack_halves(p):
    lo = lax.bitcast_convert_type(p << 16, F32)
    hi = lax.bitcast_convert_type(p & jnp.uint32(0xFFFF0000), F32)
    return lo, hi


def _split3(x):
    hi = x.astype(MXU_DTYPE)
    r1 = x - hi.astype(F32)
    mid = r1.astype(MXU_DTYPE)
    lo = (r1 - mid.astype(F32)).astype(MXU_DTYPE)
    return hi, mid, lo


def _ada_kernel(c_ref, w_ref, b_ref, o_ref):
    c = c_ref[...]
    cond = c * jax.nn.sigmoid(c)
    o_ref[...] = jnp.dot(cond.astype(MXU_DTYPE), w_ref[...].astype(MXU_DTYPE),
                         preferred_element_type=F32) + b_ref[...]


def _ada(c, w_ada, b_ada):
    n_layers, d, n6 = w_ada.shape
    b = c.shape[0]
    tn = n6 // 4
    return pl.pallas_call(
        _ada_kernel,
        grid=(n_layers, n6 // tn),
        in_specs=[
            pl.BlockSpec((b, d), lambda l, j: (0, 0)),
            pl.BlockSpec((None, d, tn), lambda l, j: (l, 0, j)),
            pl.BlockSpec((None, 1, tn), lambda l, j: (l, 0, j)),
        ],
        out_specs=pl.BlockSpec((None, b, tn), lambda l, j: (l, 0, j)),
        out_shape=jax.ShapeDtypeStruct((n_layers, b, n6), F32),
        compiler_params=_params(2),
        name="ada",
    )(c, w_ada, b_ada.reshape(n_layers, 1, n6))


def _nt_dot(a, b):
    return lax.dot_general(a, b, (((1,), (1,)), ((), ())), preferred_element_type=F32)


def _inproj_kernel(x_ref, ada_ref, w_ref, wvt_ref, bf_ref, tri_ref,
                   qa_ref, ka_ref, qb_ref, kb_ref, xc_ref, gc_ref, vat_ref, vbt_ref,
                   fcol_ref, frow_ref, carry_ref, *, tiles_per_seq):
    i = pl.program_id(0)
    tm = x_ref.shape[0]
    shift = ada_ref[0:1, :]
    scale = ada_ref[1:2, :]
    h = (x_ref[...] * (1.0 + scale) + shift).astype(MXU_DTYPE)
    off = 0
    for ref in (qa_ref, ka_ref, qb_ref, kb_ref, xc_ref, gc_ref):
        width = ref.shape[1]
        ref[...] = jnp.dot(h, w_ref[:, off:off + width],
                           preferred_element_type=F32).astype(ref.dtype)
        off += width
    vt = _nt_dot(wvt_ref[...], h).astype(MXU_DTYPE)
    chunk = vat_ref.shape[2]
    for n in range(tm // chunk):
        cols = slice(n * chunk, (n + 1) * chunk)
        vat_ref[n] = vt[0:ATT_WIDTH, cols]
        vbt_ref[n] = vt[ATT_WIDTH:2 * ATT_WIDTH, cols]
    zf = jnp.dot(h, w_ref[:, off:off + LANES], preferred_element_type=F32) + bf_ref[...]
    logf = _log_sigmoid(zf)

    @pl.when(i % tiles_per_seq == 0)
    def _():
        carry_ref[...] = jnp.zeros_like(carry_ref)

    tri = tri_ref[...]
    f_cum = carry_ref[0:1, :]
    for piece in _split3(logf):
        f_cum = f_cum + jnp.dot(tri, piece, preferred_element_type=F32)
    carry_ref[0:1, :] = f_cum[tm - 1:tm, :]
    fcol_ref[...] = f_cum
    frow_ref[...] = f_cum.T[0:N_HEADS, :]


def _inproj(x2d, ada_l, w_main, w_vt, bf_pad, seq):
    t, d = x2d.shape
    tm = min(ROW_TILE, seq)
    chunk = min(ATT_TILE, seq)
    tiles_per_seq = seq // tm
    n_batch = t // seq
    tri = (lax.broadcasted_iota(jnp.int32, (tm, tm), 1)
           <= lax.broadcasted_iota(jnp.int32, (tm, tm), 0)).astype(MXU_DTYPE)
    row = lambda i: (i, 0)
    att = jax.ShapeDtypeStruct((t, ATT_WIDTH), MXU_DTYPE)
    att_t = jax.ShapeDtypeStruct((t // chunk, ATT_WIDTH, chunk), MXU_DTYPE)
    wide = jax.ShapeDtypeStruct((t, LRU_WIDTH), F32)
    att_spec = pl.BlockSpec((tm, ATT_WIDTH), row)
    att_t_spec = pl.BlockSpec((tm // chunk, ATT_WIDTH, chunk), lambda i: (i, 0, 0))
    wide_spec = pl.BlockSpec((tm, LRU_WIDTH), row)
    return pl.pallas_call(
        functools.partial(_inproj_kernel, tiles_per_seq=tiles_per_seq),
        grid=(t // tm,),
        in_specs=[
            pl.BlockSpec((tm, d), row),
            pl.BlockSpec((None, 6, d), lambda i: (i // tiles_per_seq, 0, 0)),
            pl.BlockSpec(w_main.shape, lambda i: (0, 0)),
            pl.BlockSpec(w_vt.shape, lambda i: (0, 0)),
            pl.BlockSpec((1, LANES), lambda i: (0, 0)),
            pl.BlockSpec((tm, tm), lambda i: (0, 0)),
        ],
        out_specs=[att_spec] * 4 + [wide_spec] * 2 + [att_t_spec] * 2 + [
            pl.BlockSpec((tm, LANES), row),
            pl.BlockSpec((None, N_HEADS, tm),
                         lambda i: (i // tiles_per_seq, 0, i % tiles_per_seq)),
        ],
        out_shape=[att] * 4 + [wide] * 2 + [att_t] * 2 + [
            jax.ShapeDtypeStruct((t, LANES), F32),
            jax.ShapeDtypeStruct((n_batch, N_HEADS, seq), F32),
        ],
        scratch_shapes=[pltpu.VMEM((8, LANES), F32)],
        compiler_params=_params(1),
        name="inproj",
    )(x2d, ada_l, w_main, w_vt, bf_pad, tri)


def _head_query(q2, c):
    lane = lax.broadcasted_iota(jnp.int32, (1, LANES), 1)
    in_head = (lane >= c * HEAD_DIM) & (lane < (c + 1) * HEAD_DIM)
    return jnp.where(in_head, q2, jnp.zeros_like(q2)) * ATT_SCALE


def _pair_cols(head):
    return slice((head // 2) * LANES, (head // 2 + 1) * LANES)


def _store_heads(o_ref, heads, outs_t):
    sub = lax.broadcasted_iota(jnp.int32, (LANES, 1), 0)
    for n in range(0, len(heads), 2):
        pair_t = jnp.where(sub < HEAD_DIM, outs_t[n], outs_t[n + 1])
        o_ref[:, _pair_cols(heads[n])] = pair_t.T.astype(o_ref.dtype)


def _fox_kernel(q_ref, k_ref, vt_ref, fcol_ref, frow_ref, o_ref):
    i = pl.program_id(1)
    tq = q_ref.shape[0]
    key = lax.broadcasted_iota(jnp.int32, (tq, tq), 0)
    qry = lax.broadcasted_iota(jnp.int32, (tq, tq), 1)
    causal = key <= qry
    for g in range(N_HEADS // ATT_GROUP):
        heads = list(range(g * ATT_GROUP, (g + 1) * ATT_GROUP))
        qms = [_head_query(q_ref[:, _pair_cols(hd)], hd % 2) for hd in heads]
        fqs = [frow_ref[hd, pl.ds(i, 1), :] for hd in heads]

        def qk(j, heads=heads, qms=qms):
            start = pl.multiple_of(j * tq, tq)
            return tuple(_nt_dot(k_ref[pl.ds(start, tq), _pair_cols(hd)], qms[n])
                         for n, hd in enumerate(heads))

        def block(j, scores, carry, masked, heads=heads, fqs=fqs):
            start = pl.multiple_of(j * tq, tq)
            stats, probs = [], []
            for n, hd in enumerate(heads):
                m, l, _ = carry[n]
                fk = fcol_ref[pl.ds(start, tq), hd:hd + 1]
                s = scores[n] + (fqs[n] - fk)
                if masked:
                    s = jnp.where(causal, s, -jnp.inf)
                m_new = jnp.maximum(m, jnp.max(s, axis=0, keepdims=True))
                alpha = jnp.exp(m - m_new)
                pm = jnp.exp(s - m_new)
                l = alpha * l + jnp.sum(pm, axis=0, keepdims=True)
                stats.append((m_new, l, alpha))
                probs.append(pm.astype(MXU_DTYPE))
            pvs = [jnp.dot(vt_ref[j, _pair_cols(hd), :], probs[n],
                           preferred_element_type=F32) for n, hd in enumerate(heads)]
            return tuple((stats[n][0], stats[n][1], stats[n][2] * carry[n][2] + pvs[n])
                         for n in range(len(heads)))

        init = tuple((jnp.full((1, tq), -jnp.inf, F32), jnp.zeros((1, tq), F32),
                      jnp.zeros((LANES, tq), F32)) for _ in heads)
        carry = lax.fori_loop(
            0, i, lambda j, cr, qk=qk, block=block: block(j, qk(j), cr, False), init)
        carry = block(i, qk(i), carry, True)
        _store_heads(o_ref, heads, [acc / l for _, l, acc in carry])


def _att_specs(t, seq):
    n_batch = t // seq
    tq = min(ATT_TILE, seq)
    nq = seq // tq
    q_spec = pl.BlockSpec((tq, ATT_WIDTH), lambda b, i: (b * nq + i, 0))
    k_spec = pl.BlockSpec((seq, ATT_WIDTH), lambda b, i: (b, 0))
    vt_spec = pl.BlockSpec((nq, ATT_WIDTH, tq), lambda b, i: (b, 0, 0))
    return n_batch, tq, nq, q_spec, k_spec, vt_spec


def _fox(q, k, vt, fcol, frow4, seq):
    t = q.shape[0]
    n_batch, tq, nq, q_spec, k_spec, vt_spec = _att_specs(t, seq)
    return pl.pallas_call(
        _fox_kernel,
        grid=(n_batch, nq),
        in_specs=[
            q_spec, k_spec, vt_spec,
            pl.BlockSpec((seq, LANES), lambda b, i: (b, 0)),
            pl.BlockSpec((None, N_HEADS, nq, tq), lambda b, i: (b, 0, 0, 0)),
        ],
        out_specs=q_spec,
        out_shape=jax.ShapeDtypeStruct((t, ATT_WIDTH), MXU_DTYPE),
        compiler_params=_params(2),
        name="fox_attention",
    )(q, k, vt, fcol, frow4)


def _sb_kernel(q_ref, k_ref, vt_ref, suf_ref, o_ref):
    i = pl.program_id(1)
    tq = q_ref.shape[0]
    key = lax.broadcasted_iota(jnp.int32, (tq, tq), 0)
    qry = lax.broadcasted_iota(jnp.int32, (tq, tq), 1)
    strict = key < qry
    suf2 = suf_ref[...]
    for g in range(N_HEADS // ATT_GROUP):
        heads = list(range(g * ATT_GROUP, (g + 1) * ATT_GROUP))
        qms = [_head_query(q_ref[:, _pair_cols(hd)], hd % 2) for hd in heads]

        def qk(j, heads=heads, qms=qms):
            start = pl.multiple_of(j * tq, tq)
            return tuple(_nt_dot(k_ref[pl.ds(start, tq), _pair_cols(hd)], qms[n])
                         for n, hd in enumerate(heads))

        def block(j, zs, carry, masked, heads=heads):
            log_betas, splits, laters = [], [], []
            for n in range(len(heads)):
                z = zs[n]
                sp = jnp.maximum(z, 0.0) + jnp.log(1.0 + jnp.exp2(jnp.abs(z) * -LOG2_E))
                log_betas.append(z - sp)
                if masked:
                    sp = jnp.where(strict, sp, 0.0)
                hi = sp.astype(MXU_DTYPE)
                lo = (sp - hi.astype(F32)).astype(MXU_DTYPE)
                splits.append(jnp.concatenate([hi, lo], axis=0))
                laters.append(carry[n][0] + jnp.sum(sp, axis=0, keepdims=True))
            afters = [jnp.dot(suf2, splits[n], preferred_element_type=F32)
                      for n in range(len(heads))]
            ws = []
            for n in range(len(heads)):
                w = jnp.exp(log_betas[n] - jnp.maximum(afters[n] + carry[n][0], 0.0))
                if masked:
                    w = jnp.where(strict, w, 0.0)
                ws.append(w.astype(MXU_DTYPE))
            return tuple(
                (laters[n], carry[n][1] + jnp.dot(vt_ref[j, _pair_cols(hd), :], ws[n],
                                                  preferred_element_type=F32))
                for n, hd in enumerate(heads))

        init = tuple((jnp.zeros((1, tq), F32), jnp.zeros((LANES, tq), F32)) for _ in heads)
        def step(s, state, qk=qk, block=block):
            zs, carry = state
            j = i - 1 - s
            nxt = qk(jnp.maximum(j - 1, 0))
            return nxt, block(j, zs, carry, False)

        diag = qk(i)
        first = qk(jnp.maximum(i - 1, 0))
        carry = block(i, diag, init, True)
        _, carry = lax.fori_loop(0, i, step, (first, carry))
        _store_heads(o_ref, heads, [acc for _, acc in carry])


def _sb(q, k, vt, seq):
    t = q.shape[0]
    n_batch, tq, nq, q_spec, k_spec, vt_spec = _att_specs(t, seq)
    suf = (lax.broadcasted_iota(jnp.int32, (tq, tq), 1)
           > lax.broadcasted_iota(jnp.int32, (tq, tq), 0)).astype(MXU_DTYPE)
    suf2 = jnp.concatenate([suf, suf], axis=1)
    return pl.pallas_call(
        _sb_kernel,
        grid=(n_batch, nq),
        in_specs=[q_spec, k_spec, vt_spec, pl.BlockSpec((tq, 2 * tq), lambda b, i: (0, 0))],
        out_specs=q_spec,
        out_shape=jax.ShapeDtypeStruct((t, ATT_WIDTH), MXU_DTYPE),
        compiler_params=_params(2),
        name="sb_attention",
    )(q, k, vt, suf2)


def _lru_kernel(xc_ref, gc_ref, cw_ref, cb_ref, wa_ref, ba_ref, wx_ref, bx_ref, lam_ref,
                o_ref, ext_ref, a_ref, u_ref, h_ref, state_ref):
    s_idx = pl.program_id(1)
    ts = xc_ref.shape[0]
    halo = 8

    @pl.when(s_idx == 0)
    def _():
        ext_ref[0:halo, :] = jnp.zeros((halo, LRU_WIDTH), F32)
        state_ref[...] = jnp.zeros_like(state_ref)

    ext_ref[halo:halo + ts, :] = xc_ref[...]
    xconv = cb_ref[...]
    for j in range(CONV_WIDTH):
        lo = halo - (CONV_WIDTH - 1) + j
        xconv = xconv + cw_ref[j:j + 1, :] * ext_ref[lo:lo + ts, :]
    ext_ref[0:halo, :] = ext_ref[ts:ts + halo, :]

    xb = xconv.astype(MXU_DTYPE)
    r = jax.nn.sigmoid(jnp.dot(xb, wa_ref[...], preferred_element_type=F32) + ba_ref[...])
    ig = jax.nn.sigmoid(jnp.dot(xb, wx_ref[...], preferred_element_type=F32) + bx_ref[...])
    log_a = (-LRU_C * _softplus(-lam_ref[...])) * r
    a = jnp.exp(log_a)
    a_ref[...] = a
    u_ref[...] = jnp.sqrt(1.0 - a * a) * (ig * xconv)

    def step(t, h):
        h = a_ref[pl.ds(t, 1), :] * h + u_ref[pl.ds(t, 1), :]
        h_ref[pl.ds(t, 1), :] = h
        return h

    state_ref[0:1, :] = lax.fori_loop(0, ts, step, state_ref[0:1, :], unroll=8)
    o_ref[...] = (h_ref[...] * jax.nn.gelu(gc_ref[...])).astype(o_ref.dtype)


def _lru(xc, gc, conv_w, conv_b, wa_bd, ba, wx_bd, bx, lam, seq):
    t, w = xc.shape
    n_batch = t // seq
    ts = min(LRU_TILE, seq)
    ns = seq // ts
    row_spec = pl.BlockSpec((ts, w), lambda b, s: (b * ns + s, 0))
    vec_spec = pl.BlockSpec((1, w), lambda b, s: (0, 0))
    mat_spec = pl.BlockSpec((w, w), lambda b, s: (0, 0))
    return pl.pallas_call(
        _lru_kernel,
        grid=(n_batch, ns),
        in_specs=[row_spec, row_spec,
                  pl.BlockSpec((CONV_WIDTH, w), lambda b, s: (0, 0)), vec_spec,
                  mat_spec, vec_spec, mat_spec, vec_spec, vec_spec],
        out_specs=row_spec,
        out_shape=jax.ShapeDtypeStruct((t, w), MXU_DTYPE),
        scratch_shapes=[pltpu.VMEM((ts + 8, w), F32), pltpu.VMEM((ts, w), F32),
                        pltpu.VMEM((ts, w), F32), pltpu.VMEM((ts, w), F32),
                        pltpu.VMEM((8, w), F32)],
        compiler_params=_params(2),
        name="rg_lru",
    )(xc, gc, conv_w, conv_b, wa_bd, ba, wx_bd, bx, lam)


def _merge_kernel(x_ref, ada_ref, oa_ref, ob_ref, oc_ref, wg_ref, bg_ref, wpa_ref, wpb_ref,
                  wpc_ref, wo_ref, lng_ref, lnb_ref, wr_ref, br_ref,
                  x1_ref, h2_ref, logit_ref):
    d = x_ref.shape[1]
    x = x_ref[...]
    shift1, scale1, gate1 = ada_ref[0:1, :], ada_ref[1:2, :], ada_ref[2:3, :]
    shift2, scale2 = ada_ref[3:4, :], ada_ref[4:5, :]
    h = (x * (1.0 + scale1) + shift1).astype(MXU_DTYPE)
    merged = None
    for n, (o_ref, w_ref) in enumerate(((oa_ref, wpa_ref), (ob_ref, wpb_ref), (oc_ref, wpc_ref))):
        cs = slice(n * d, (n + 1) * d)
        g = jax.nn.sigmoid(jnp.dot(h, wg_ref[:, cs], preferred_element_type=F32) + bg_ref[:, cs])
        term = g * jnp.dot(o_ref[...], w_ref[...], preferred_element_type=F32)
        merged = term if merged is None else merged + term
    y = jnp.dot(merged.astype(MXU_DTYPE), wo_ref[...], preferred_element_type=F32)
    x1 = _layer_norm(DEEPNORM_ALPHA * x + (1.0 + gate1) * y, lng_ref[...], lnb_ref[...])
    x1_ref[...] = x1
    h2 = x1 * (1.0 + scale2) + shift2
    h2_ref[...] = _pack_halves(h2)
    logit_ref[...] = _nt_dot(wr_ref[...], h2.astype(MXU_DTYPE)) + br_ref[...]


def _merge(x2d, ada_l, o_a, o_b, o_c, w_gate, b_gate, w_pa, w_pb, w_pc, w_o, ln_g, ln_b,
           w_router_t, b_router, seq):
    t, d = x2d.shape
    tm = min(ROW_TILE, seq)
    tiles_per_seq = seq // tm
    row = lambda i: (i, 0)
    whole = lambda a: pl.BlockSpec(a.shape, lambda i: (0,) * a.ndim)
    return pl.pallas_call(
        _merge_kernel,
        grid=(t // tm,),
        in_specs=[
            pl.BlockSpec((tm, d), row),
            pl.BlockSpec((None, 6, d), lambda i: (i // tiles_per_seq, 0, 0)),
            pl.BlockSpec((tm, ATT_WIDTH), row), pl.BlockSpec((tm, ATT_WIDTH), row),
            pl.BlockSpec((tm, LRU_WIDTH), row),
            whole(w_gate), whole(b_gate), whole(w_pa), whole(w_pb), whole(w_pc), whole(w_o),
            whole(ln_g), whole(ln_b), whole(w_router_t), whole(b_router),
        ],
        out_specs=[pl.BlockSpec((tm, d), row), pl.BlockSpec((tm, d // 2), row),
                   pl.BlockSpec((N_EXPERTS, tm), lambda i: (0, i))],
        out_shape=[jax.ShapeDtypeStruct((t, d), F32),
                   jax.ShapeDtypeStruct((t, d // 2), jnp.uint32),
                   jax.ShapeDtypeStruct((N_EXPERTS, t), F32)],
        compiler_params=_params(1),
        name="merge_outproj_ln",
    )(x2d, ada_l, o_a, o_b, o_c, w_gate, b_gate, w_pa, w_pb, w_pc, w_o, ln_g, ln_b,
      w_router_t, b_router)


def _route_kernel(logit_ref, tri_ref, eid_ref, gate_ref, pos_ref, cnt_ref, carry_ref):
    i = pl.program_id(0)
    tr = logit_ref.shape[1]

    @pl.when(i == 0)
    def _():
        carry_ref[...] = jnp.zeros_like(carry_ref)

    erow = lax.broadcasted_iota(jnp.int32, (N_EXPERTS, tr), 0)
    cur = logit_ref[...]
    vals, ids = [], []
    for _ in range(TOP_K):
        m = jnp.max(cur, axis=0, keepdims=True)
        idx = jnp.min(jnp.where(cur == m, erow, N_EXPERTS), axis=0, keepdims=True)
        vals.append(m)
        ids.append(idx)
        cur = jnp.where(erow == idx, -jnp.inf, cur)
    exps = [jnp.exp(v - vals[0]) for v in vals]
    denom = exps[0] + exps[1] + exps[2] + exps[3]
    chosen = jnp.zeros((N_EXPERTS, tr), F32)
    for idx in ids:
        chosen = chosen + jnp.where(erow == idx, 1.0, 0.0)
    prefix = (jnp.dot(chosen.astype(MXU_DTYPE), tri_ref[...], preferred_element_type=F32)
              + carry_ref[:, 0:1])
    for k in range(TOP_K):
        eid_ref[k:k + 1, :] = ids[k]
        gate_ref[k:k + 1, :] = exps[k] / denom
        pos_ref[k:k + 1, :] = jnp.sum(jnp.where(erow == ids[k], prefix, 0.0), axis=0,
                                      keepdims=True).astype(jnp.int32)
    carry_ref[...] = carry_ref[...] + jnp.sum(chosen, axis=1, keepdims=True)
    cnt_ref[...] = carry_ref[...].astype(jnp.int32)


def _route(logits_t):
    n_e, t = logits_t.shape
    tr = min(ROUTE_TILE, t)
    tri = (lax.broadcasted_iota(jnp.int32, (tr, tr), 0)
           < lax.broadcasted_iota(jnp.int32, (tr, tr), 1)).astype(MXU_DTYPE)
    tok = lambda i: (0, i)
    return pl.pallas_call(
        _route_kernel,
        grid=(t // tr,),
        in_specs=[pl.BlockSpec((n_e, tr), tok), pl.BlockSpec((tr, tr), lambda i: (0, 0))],
        out_specs=[pl.BlockSpec((TOP_K, tr), tok)] * 3
        + [pl.BlockSpec((n_e, LANES), lambda i: (0, 0))],
        out_shape=[jax.ShapeDtypeStruct((TOP_K, t), jnp.int32),
                   jax.ShapeDtypeStruct((TOP_K, t), F32),
                   jax.ShapeDtypeStruct((TOP_K, t), jnp.int32),
                   jax.ShapeDtypeStruct((n_e, LANES), jnp.int32)],
        scratch_shapes=[pltpu.VMEM((n_e, LANES), F32)],
        compiler_params=_params(1),
        name="route_topk",
    )(logits_t, tri)


def _dest_kernel(start_ref, eid_ref, pos_ref, dest_ref):
    eid = eid_ref[...]
    dest = pos_ref[...]
    for e in range(N_EXPERTS):
        dest = jnp.where(eid == e, dest + start_ref[e], dest)
    dest_ref[...] = dest


def _dest(pad_start, eid, pos):
    k, t = eid.shape
    tr = min(ROUTE_TILE, t)
    spec = pl.BlockSpec((k, tr), lambda i, s: (0, i))
    return pl.pallas_call(
        _dest_kernel,
        grid_spec=pltpu.PrefetchScalarGridSpec(
            num_scalar_prefetch=1, grid=(t // tr,), in_specs=[spec, spec], out_specs=spec),
        out_shape=jax.ShapeDtypeStruct((k, t), jnp.int32),
        compiler_params=_params(1),
        name="route_dest",
    )(pad_start, eid, pos)


def _dispatch_kernel(dest_ref, h_ref, init_ref, buf_ref, sem):
    del init_ref
    tm = h_ref.shape[0]

    def row_copy(r, k):
        return pltpu.make_async_copy(h_ref.at[pl.ds(r, 1), :],
                                     buf_ref.at[pl.ds(dest_ref[k, r], 1), :], sem)

    def issue(r, _):
        for k in range(TOP_K):
            row_copy(r, k).start(priority=k % 2)
        return 0

    def drain(r, _):
        for k in range(TOP_K):
            row_copy(r, k).wait()
        return 0

    lax.fori_loop(0, tm, issue, 0)
    lax.fori_loop(0, tm, drain, 0)


def _dispatch(dest, h2, n_slots):
    t, d = h2.shape
    tm = min(MOVE_TILE, t)
    init = jnp.zeros((n_slots, d), h2.dtype)
    return pl.pallas_call(
        _dispatch_kernel,
        grid=(t // tm,),
        in_specs=[
            pl.BlockSpec((TOP_K, tm), lambda i: (0, i), memory_space=pltpu.SMEM),
            pl.BlockSpec((tm, d), lambda i: (i, 0)),
            pl.BlockSpec(memory_space=pl.ANY),
        ],
        out_specs=pl.BlockSpec(memory_space=pl.ANY),
        out_shape=jax.ShapeDtypeStruct((n_slots, d), h2.dtype),
        scratch_shapes=[pltpu.SemaphoreType.DMA(())],
        input_output_aliases={2: 0},
        compiler_params=_params(1),
        name="moe_dispatch",
    )(dest, h2, init)


def _expert_kernel(be_ref, nused_ref, x_ref, wgu_ref, bgu_ref, wd_ref, bd_ref, y_ref,
                   wgu_lp, wd_lp):
    i = pl.program_id(0)
    d_e = wd_ref.shape[0]

    @pl.when((i == 0) | (be_ref[i] != be_ref[jnp.maximum(i - 1, 0)]))
    def _():
        wgu_lp[...] = wgu_ref[...].astype(MXU_DTYPE)
        wd_lp[...] = wd_ref[...].astype(MXU_DTYPE)

    @pl.when(i < nused_ref[0])
    def _():
        lo, hi = _unpack_halves(x_ref[...])
        x = jnp.concatenate([lo, hi], axis=1).astype(MXU_DTYPE)
        gu = jnp.dot(x, wgu_lp[...], preferred_element_type=F32) + bgu_ref[...]
        gate = jnp.minimum(gu[:, :d_e], SWIGLU_LIMIT)
        up = jnp.clip(gu[:, d_e:], -SWIGLU_LIMIT, SWIGLU_LIMIT)
        act = (up + 1.0) * (gate * jax.nn.sigmoid(SWIGLU_ALPHA * gate))
        y = jnp.dot(act.astype(MXU_DTYPE), wd_lp[...], preferred_element_type=F32) + bd_ref[...]
        y_ref[...] = _pack_halves(y)

    @pl.when(i >= nused_ref[0])
    def _():
        y_ref[...] = jnp.zeros_like(y_ref)


def _experts(layer, block_e, n_used, x_buf, w_gu, b_gu, w_down, b_down):
    n_slots, half = x_buf.shape
    _, n_e, d, d2 = w_gu.shape
    d_e = w_down.shape[2]
    n_blocks = n_slots // EXPERT_BLOCK
    n_layers = w_gu.shape[0]
    return pl.pallas_call(
        _expert_kernel,
        grid_spec=pltpu.PrefetchScalarGridSpec(
            num_scalar_prefetch=2,
            grid=(n_blocks,),
            in_specs=[
                pl.BlockSpec((EXPERT_BLOCK, half), lambda i, be, nu: (i, 0)),
                pl.BlockSpec((None, None, d, d2), lambda i, be, nu: (layer, be[i], 0, 0)),
                pl.BlockSpec((None, None, 1, d2), lambda i, be, nu: (layer, be[i], 0, 0)),
                pl.BlockSpec((None, None, d_e, d), lambda i, be, nu: (layer, be[i], 0, 0)),
                pl.BlockSpec((None, None, 1, d), lambda i, be, nu: (layer, be[i], 0, 0)),
            ],
            out_specs=pl.BlockSpec((EXPERT_BLOCK, half), lambda i, be, nu: (i, 0)),
            scratch_shapes=[pltpu.VMEM((d, d2), MXU_DTYPE), pltpu.VMEM((d_e, d), MXU_DTYPE)],
        ),
        out_shape=jax.ShapeDtypeStruct((n_slots, half), jnp.uint32),
        compiler_params=_params(1),
        name="moe_experts",
    )(block_e, n_used, x_buf, w_gu, b_gu.reshape(n_layers, n_e, 1, d2), w_down,
      b_down.reshape(n_layers, n_e, 1, d))


def _combine_kernel(dest_ref, x1_ref, ada_ref, gate_ref, lng_ref, lnb_ref, ybuf_ref,
                    o_ref, rows_ref, sem):
    tm = x1_ref.shape[0]

    def row_copy(r, k):
        return pltpu.make_async_copy(ybuf_ref.at[pl.ds(dest_ref[k, r], 1), :],
                                     rows_ref.at[k, pl.ds(r, 1), :], sem)

    def issue(r, _):
        for k in range(TOP_K):
            row_copy(r, k).start(priority=k % 2)
        return 0

    def drain(r, _):
        for k in range(TOP_K):
            row_copy(r, k).wait()
        return 0

    lax.fori_loop(0, tm, issue, 0)
    lax.fori_loop(0, tm, drain, 0)
    gates = gate_ref[...]
    y_lo = y_hi = None
    for k in range(TOP_K):
        lo, hi = _unpack_halves(rows_ref[k])
        g = gates[:, k:k + 1]
        y_lo = lo * g if y_lo is None else y_lo + lo * g
        y_hi = hi * g if y_hi is None else y_hi + hi * g
    y = jnp.concatenate([y_lo, y_hi], axis=1)
    gate2 = ada_ref[5:6, :]
    o_ref[...] = _layer_norm(DEEPNORM_ALPHA * x1_ref[...] + (1.0 + gate2) * y,
                             lng_ref[...], lnb_ref[...])


def _combine(dest, x1, ada_l, gates_tk, ln_g, ln_b, y_buf, seq):
    t, d = x1.shape
    tm = min(MOVE_TILE, seq)
    tiles_per_seq = seq // tm
    return pl.pallas_call(
        _combine_kernel,
        grid=(t // tm,),
        in_specs=[
            pl.BlockSpec((TOP_K, tm), lambda i: (0, i), memory_space=pltpu.SMEM),
            pl.BlockSpec((tm, d), lambda i: (i, 0)),
            pl.BlockSpec((None, 6, d), lambda i: (i // tiles_per_seq, 0, 0)),
            pl.BlockSpec((tm, TOP_K), lambda i: (i, 0)),
            pl.BlockSpec((1, d), lambda i: (0, 0)),
            pl.BlockSpec((1, d), lambda i: (0, 0)),
            pl.BlockSpec(memory_space=pl.ANY),
        ],
        out_specs=pl.BlockSpec((tm, d), lambda i: (i, 0)),
        out_shape=jax.ShapeDtypeStruct((t, d), F32),
        scratch_shapes=[pltpu.VMEM((TOP_K, tm, d // 2), jnp.uint32),
                        pltpu.SemaphoreType.DMA(())],
        compiler_params=_params(1),
        name="moe_combine_ln",
    )(dest, x1, ada_l, gates_tk, ln_g, ln_b, y_buf)


def _split_w_in(w_in):
    d = w_in.shape[0]
    sizes = (ATT_WIDTH, ATT_WIDTH, ATT_WIDTH, N_HEADS, ATT_WIDTH, ATT_WIDTH, ATT_WIDTH,
             LRU_WIDTH, LRU_WIDTH)
    bounds = [sum(sizes[:n]) for n in range(len(sizes) + 1)]
    qa, ka, va, fa, qb, kb, vb, xc, gc = (
        w_in[:, bounds[n]:bounds[n + 1]] for n in range(len(sizes)))
    forget = jnp.concatenate([fa, jnp.zeros((d, LANES - N_HEADS), w_in.dtype)], axis=1)
    w_main = jnp.concatenate([qa, ka, qb, kb, xc, gc, forget], axis=1).astype(MXU_DTYPE)
    w_vt = jnp.concatenate([va, vb], axis=1).T.astype(MXU_DTYPE)
    return w_main, w_vt


def _block_diag(w):
    n, c, dd = w.shape
    eye = jnp.eye(n, dtype=w.dtype)
    return (eye[:, None, :, None] * w[:, :, None, :]).reshape(n * c, n * dd).astype(MXU_DTYPE)


def kernel(x, c, w_ada, b_ada, ln1_g, ln1_b, w_in, b_f, conv_w, conv_b, lru_wa, lru_ba, lru_wx,
           lru_bx, lru_lambda, w_gate, b_gate, w_pa, w_pb, w_pc, w_o, ln2_g, ln2_b, w_router,
           b_router, w_gu, b_gu, w_down, b_down):
    n_batch, seq, d = x.shape
    t = n_batch * seq
    n_layers = w_ada.shape[0]
    n_blocks = (t * TOP_K) // EXPERT_BLOCK + N_EXPERTS
    n_slots = n_blocks * EXPERT_BLOCK
    att_tile = min(ATT_TILE, seq)
    vec = lambda a: a.reshape(1, -1)

    ada = _ada(c, w_ada, b_ada).reshape(n_layers, n_batch, 6, d)
    x2d = x.reshape(t, d)
    for l in range(n_layers):
        ada_l = ada[l]
        bf_pad = jnp.concatenate([b_f[l], jnp.zeros((LANES - N_HEADS,), F32)]).reshape(1, LANES)
        w_main, w_vt = _split_w_in(w_in[l])
        qa, ka, qb, kb, xc, gc, vat, vbt, fcol, frow = _inproj(
            x2d, ada_l, w_main, w_vt, bf_pad, seq)
        frow4 = frow.reshape(n_batch, N_HEADS, seq // att_tile, att_tile)
        o_a = _fox(qa, ka, vat, fcol, frow4, seq)
        o_b = _sb(qb, kb, vbt, seq)
        o_c = _lru(xc, gc, conv_w[l], vec(conv_b[l]), _block_diag(lru_wa[l]), vec(lru_ba[l]),
                   _block_diag(lru_wx[l]), vec(lru_bx[l]), vec(lru_lambda[l]), seq)
        x1, h2, logits_t = _merge(
            x2d, ada_l, o_a, o_b, o_c, w_gate[l].astype(MXU_DTYPE), vec(b_gate[l]),
            w_pa[l].astype(MXU_DTYPE), w_pb[l].astype(MXU_DTYPE), w_pc[l].astype(MXU_DTYPE),
            w_o[l].astype(MXU_DTYPE), vec(ln1_g[l]), vec(ln1_b[l]),
            w_router[l].T.astype(MXU_DTYPE), b_router[l].reshape(N_EXPERTS, 1), seq)

        eid, gates, pos, cnt = _route(logits_t)
        counts = cnt[:, 0]
        padded = (counts + EXPERT_BLOCK - 1) // EXPERT_BLOCK * EXPERT_BLOCK
        pad_end = jnp.cumsum(padded)
        pad_start = (pad_end - padded).astype(jnp.int32)
        block_first = (jnp.arange(n_blocks) * EXPERT_BLOCK)[:, None]
        block_e = jnp.minimum(jnp.sum(pad_end[None, :] <= block_first, axis=1),
                              N_EXPERTS - 1).astype(jnp.int32)
        n_used = (pad_end[-1:] // EXPERT_BLOCK).astype(jnp.int32)
        dest = _dest(pad_start, eid, pos)

        x_buf = _dispatch(dest, h2, n_slots)
        y_buf = _experts(l, block_e, n_used, x_buf, w_gu, b_gu, w_down, b_down)
        x2d = _combine(dest, x1, ada_l, gates.T, vec(ln2_g[l]), vec(ln2_b[l]), y_buf, seq)
    return x2d.reshape(n_batch, seq, d)
```

```python
import functools

import jax
import jax.numpy as jnp
from jax import lax
from jax.experimental import pallas as pl
from jax.experimental.pallas import tpu as pltpu

D_MODEL = 1024
DEPTH = 2
HEAD_DIM = 64
N_HEADS = 8
ATT_WIDTH = N_HEADS * HEAD_DIM
LRU_WIDTH = D_MODEL
LRU_BLOCKS = 16
CONV_WIDTH = 4
LRU_C = 8.0
N_EXPERTS = 32
TOP_K = 4
SWIGLU_LIMIT = 7.0
SWIGLU_ALPHA = 1.702
LN_EPS = 1e-5
DEEPNORM_ALPHA = (2.0 * DEPTH) ** 0.25
ATT_SCALE = HEAD_DIM ** -0.5
LOG2_E = 1.4426950408889634

LANES = 128
MXU_DTYPE = jnp.bfloat16
F32 = jnp.float32

ROW_TILE = 512
ATT_TILE = 256
ATT_GROUP = 8
LRU_TILE = 256
ROUTE_TILE = 512
EXPERT_BLOCK = 256
MOVE_TILE = 256
VMEM_LIMIT = 56 * 1024 * 1024


def _params(n_axes, vmem=VMEM_LIMIT):
    return pltpu.CompilerParams(
        dimension_semantics=("arbitrary",) * n_axes, vmem_limit_bytes=vmem)


def _log_sigmoid(x):
    return jnp.minimum(x, 0.0) - jnp.log1p(jnp.exp(-jnp.abs(x)))


def _softplus(x):
    return jnp.maximum(x, 0.0) + jnp.log1p(jnp.exp(-jnp.abs(x)))


def _layer_norm(v, g, b):
    mu = jnp.mean(v, axis=-1, keepdims=True)
    d = v - mu
    var = jnp.mean(d * d, axis=-1, keepdims=True)
    return d * lax.rsqrt(var + LN_EPS) * g + b


def _pack_halves(x):
    n = x.shape[1] // 2
    bits = lambda v: lax.bitcast_convert_type(v.astype(jnp.bfloat16).astype(F32), jnp.uint32)
    return (bits(x[:, :n]) >> 16) | bits(x[:, n:])


def _unpack_halves(p):
    lo = lax.bitcast_convert_type(p << 16, F32)
    hi = lax.bitcast_convert_type(p & jnp.uint32(0xFFFF0000), F32)
    return lo, hi


def _split3(x):
    hi = x.astype(MXU_DTYPE)
    r1 = x - hi.astype(F32)
    mid = r1.astype(MXU_DTYPE)
    lo = (r1 - mid.astype(F32)).astype(MXU_DTYPE)
    return hi, mid, lo


def _ada_kernel(c_ref, w_ref, b_ref, o_ref):
    c = c_ref[...]
    cond = c * jax.nn.sigmoid(c)
    o_ref[...] = jnp.dot(cond.astype(MXU_DTYPE), w_ref[...].astype(MXU_DTYPE),
                         preferred_element_type=F32) + b_ref[...]


def _ada(c, w_ada, b_ada):
    n_layers, d, n6 = w_ada.shape
    b = c.shape[0]
    tn = n6 // 4
    return pl.pallas_call(
        _ada_kernel,
        grid=(n_layers, n6 // tn),
        in_specs=[
            pl.BlockSpec((b, d), lambda l, j: (0, 0)),
            pl.BlockSpec((None, d, tn), lambda l, j: (l, 0, j)),
            pl.BlockSpec((None, 1, tn), lambda l, j: (l, 0, j)),
        ],
        out_specs=pl.BlockSpec((None, b, tn), lambda l, j: (l, 0, j)),
        out_shape=jax.ShapeDtypeStruct((n_layers, b, n6), F32),
        compiler_params=_params(2),
        name="ada",
    )(c, w_ada, b_ada.reshape(n_layers, 1, n6))


def _nt_dot(a, b):
    return lax.dot_general(a, b, (((1,), (1,)), ((), ())), preferred_element_type=F32)


def _inproj_kernel(x_ref, ada_ref, w_ref, wvt_ref, bf_ref, tri_ref,
                   qa_ref, ka_ref, qb_ref, kb_ref, xc_ref, gc_ref, vat_ref, vbt_ref,
                   fcol_ref, frow_ref, carry_ref, *, tiles_per_seq):
    i = pl.program_id(0)
    tm = x_ref.shape[0]
    shift = ada_ref[0:1, :]
    scale = ada_ref[1:2, :]
    h = (x_ref[...] * (1.0 + scale) + shift).astype(MXU_DTYPE)
    off = 0
    for ref in (qa_ref, ka_ref, qb_ref, kb_ref, xc_ref, gc_ref):
        width = ref.shape[1]
        ref[...] = jnp.dot(h, w_ref[:, off:off + width],
                           preferred_element_type=F32).astype(ref.dtype)
        off += width
    vt = _nt_dot(wvt_ref[...], h).astype(MXU_DTYPE)
    chunk = vat_ref.shape[2]
    for n in range(tm // chunk):
        cols = slice(n * chunk, (n + 1) * chunk)
        vat_ref[n] = vt[0:ATT_WIDTH, cols]
        vbt_ref[n] = vt[ATT_WIDTH:2 * ATT_WIDTH, cols]
    zf = jnp.dot(h, w_ref[:, off:off + LANES], preferred_element_type=F32) + bf_ref[...]
    logf = _log_sigmoid(zf)

    @pl.when(i % tiles_per_seq == 0)
    def _():
        carry_ref[...] = jnp.zeros_like(carry_ref)

    tri = tri_ref[...]
    f_cum = carry_ref[0:1, :]
    for piece in _split3(logf):
        f_cum = f_cum + jnp.dot(tri, piece, preferred_element_type=F32)
    carry_ref[0:1, :] = f_cum[tm - 1:tm, :]
    fcol_ref[...] = f_cum
    frow_ref[...] = f_cum.T[0:N_HEADS, :]


def _inproj(x2d, ada_l, w_main, w_vt, bf_pad, seq):
    t, d = x2d.shape
    tm = min(ROW_TILE, seq)
    chunk = min(ATT_TILE, seq)
    tiles_per_seq = seq // tm
    n_batch = t // seq
    tri = (lax.broadcasted_iota(jnp.int32, (tm, tm), 1)
           <= lax.broadcasted_iota(jnp.int32, (tm, tm), 0)).astype(MXU_DTYPE)
    row = lambda i: (i, 0)
    att = jax.ShapeDtypeStruct((t, ATT_WIDTH), MXU_DTYPE)
    att_t = jax.ShapeDtypeStruct((t // chunk, ATT_WIDTH, chunk), MXU_DTYPE)
    wide = jax.ShapeDtypeStruct((t, LRU_WIDTH), F32)
    att_spec = pl.BlockSpec((tm, ATT_WIDTH), row)
    att_t_spec = pl.BlockSpec((tm // chunk, ATT_WIDTH, chunk), lambda i: (i, 0, 0))
    wide_spec = pl.BlockSpec((tm, LRU_WIDTH), row)
    return pl.pallas_call(
        functools.partial(_inproj_kernel, tiles_per_seq=tiles_per_seq),
        grid=(t // tm,),
        in_specs=[
            pl.BlockSpec((tm, d), row),
            pl.BlockSpec((None, 6, d), lambda i: (i // tiles_per_seq, 0, 0)),
            pl.BlockSpec(w_main.shape, lambda i: (0, 0)),
            pl.BlockSpec(w_vt.shape, lambda i: (0, 0)),
            pl.BlockSpec((1, LANES), lambda i: (0, 0)),
            pl.BlockSpec((tm, tm), lambda i: (0, 0)),
        ],
        out_specs=[att_spec] * 4 + [wide_spec] * 2 + [att_t_spec] * 2 + [
            pl.BlockSpec((tm, LANES), row),
            pl.BlockSpec((None, N_HEADS, tm),
                         lambda i: (i // tiles_per_seq, 0, i % tiles_per_seq)),
        ],
        out_shape=[att] * 4 + [wide] * 2 + [att_t] * 2 + [
            jax.ShapeDtypeStruct((t, LANES), F32),
            jax.ShapeDtypeStruct((n_batch, N_HEADS, seq), F32),
        ],
        scratch_shapes=[pltpu.VMEM((8, LANES), F32)],
        compiler_params=_params(1),
        name="inproj",
    )(x2d, ada_l, w_main, w_vt, bf_pad, tri)


def _head_query(q2, c):
    lane = lax.broadcasted_iota(jnp.int32, (1, LANES), 1)
    in_head = (lane >= c * HEAD_DIM) & (lane < (c + 1) * HEAD_DIM)
    return jnp.where(in_head, q2, jnp.zeros_like(q2)) * ATT_SCALE


def _pair_cols(head):
    return slice((head // 2) * LANES, (head // 2 + 1) * LANES)


def _store_heads(o_ref, heads, outs_t):
    sub = lax.broadcasted_iota(jnp.int32, (LANES, 1), 0)
    for n in range(0, len(heads), 2):
        pair_t = jnp.where(sub < HEAD_DIM, outs_t[n], outs_t[n + 1])
        o_ref[:, _pair_cols(heads[n])] = pair_t.T.astype(o_ref.dtype)


def _fox_kernel(q_ref, k_ref, vt_ref, fcol_ref, frow_ref, o_ref):
    i = pl.program_id(1)
    tq = q_ref.shape[0]
    key = lax.broadcasted_iota(jnp.int32, (tq, tq), 0)
    qry = lax.broadcasted_iota(jnp.int32, (tq, tq), 1)
    causal = key <= qry
    for g in range(N_HEADS // ATT_GROUP):
        heads = list(range(g * ATT_GROUP, (g + 1) * ATT_GROUP))
        qms = [_head_query(q_ref[:, _pair_cols(hd)], hd % 2) for hd in heads]
        fqs = [frow_ref[hd, pl.ds(i, 1), :] for hd in heads]

        def qk(j, heads=heads, qms=qms):
            start = pl.multiple_of(j * tq, tq)
            return tuple(_nt_dot(k_ref[pl.ds(start, tq), _pair_cols(hd)], qms[n])
                         for n, hd in enumerate(heads))

        def block(j, scores, carry, masked, heads=heads, fqs=fqs):
            start = pl.multiple_of(j * tq, tq)
            stats, probs = [], []
            for n, hd in enumerate(heads):
                m, l, _ = carry[n]
                fk = fcol_ref[pl.ds(start, tq), hd:hd + 1]
                s = scores[n] + (fqs[n] - fk)
                if masked:
                    s = jnp.where(causal, s, -jnp.inf)
                m_new = jnp.maximum(m, jnp.max(s, axis=0, keepdims=True))
                alpha = jnp.exp(m - m_new)
                pm = jnp.exp(s - m_new)
                l = alpha * l + jnp.sum(pm, axis=0, keepdims=True)
                stats.append((m_new, l, alpha))
                probs.append(pm.astype(MXU_DTYPE))
            pvs = [jnp.dot(vt_ref[j, _pair_cols(hd), :], probs[n],
                           preferred_element_type=F32) for n, hd in enumerate(heads)]
            return tuple((stats[n][0], stats[n][1], stats[n][2] * carry[n][2] + pvs[n])
                         for n in range(len(heads)))

        init = tuple((jnp.full((1, tq), -jnp.inf, F32), jnp.zeros((1, tq), F32),
                      jnp.zeros((LANES, tq), F32)) for _ in heads)
        carry = lax.fori_loop(
            0, i, lambda j, cr, qk=qk, block=block: block(j, qk(j), cr, False), init)
        carry = block(i, qk(i), carry, True)
        _store_heads(o_ref, heads, [acc / l for _, l, acc in carry])


def _att_specs(t, seq):
    n_batch = t // seq
    tq = min(ATT_TILE, seq)
    nq = seq // tq
    q_spec = pl.BlockSpec((tq, ATT_WIDTH), lambda b, i: (b * nq + i, 0))
    k_spec = pl.BlockSpec((seq, ATT_WIDTH), lambda b, i: (b, 0))
    vt_spec = pl.BlockSpec((nq, ATT_WIDTH, tq), lambda b, i: (b, 0, 0))
    return n_batch, tq, nq, q_spec, k_spec, vt_spec


def _fox(q, k, vt, fcol, frow4, seq):
    t = q.shape[0]
    n_batch, tq, nq, q_spec, k_spec, vt_spec = _att_specs(t, seq)
    return pl.pallas_call(
        _fox_kernel,
        grid=(n_batch, nq),
        in_specs=[
            q_spec, k_spec, vt_spec,
            pl.BlockSpec((seq, LANES), lambda b, i: (b, 0)),
            pl.BlockSpec((None, N_HEADS, nq, tq), lambda b, i: (b, 0, 0, 0)),
        ],
        out_specs=q_spec,
        out_shape=jax.ShapeDtypeStruct((t, ATT_WIDTH), MXU_DTYPE),
        compiler_params=_params(2),
        name="fox_attention",
    )(q, k, vt, fcol, frow4)


def _sb_kernel(q_ref, k_ref, vt_ref, suf_ref, o_ref):
    i = pl.program_id(1)
    tq = q_ref.shape[0]
    key = lax.broadcasted_iota(jnp.int32, (tq, tq), 0)
    qry = lax.broadcasted_iota(jnp.int32, (tq, tq), 1)
    strict = key < qry
    suf = suf_ref[...]
    for g in range(N_HEADS // ATT_GROUP):
        heads = list(range(g * ATT_GROUP, (g + 1) * ATT_GROUP))
        qms = [_head_query(q_ref[:, _pair_cols(hd)], hd % 2) for hd in heads]

        def qk(j, heads=heads, qms=qms):
            start = pl.multiple_of(j * tq, tq)
            return tuple(_nt_dot(k_ref[pl.ds(start, tq), _pair_cols(hd)], qms[n])
                         for n, hd in enumerate(heads))

        def block(j, zs, carry, masked, heads=heads):
            log_betas, splits, laters = [], [], []
            for n in range(len(heads)):
                z = zs[n]
                sp = jnp.maximum(z, 0.0) + jnp.log(1.0 + jnp.exp2(jnp.abs(z) * -LOG2_E))
                log_betas.append(z - sp)
                if masked:
                    sp = jnp.where(strict, sp, 0.0)
                splits.append(sp.astype(MXU_DTYPE))
                laters.append(carry[n][0] + jnp.sum(sp, axis=0, keepdims=True))
            afters = [jnp.dot(suf, splits[n], preferred_element_type=F32)
                      for n in range(len(heads))]
            ws = []
            for n in range(len(heads)):
                w = jnp.exp(log_betas[n] - jnp.maximum(afters[n] + carry[n][0], 0.0))
                if masked:
                    w = jnp.where(strict, w, 0.0)
                ws.append(w.astype(MXU_DTYPE))
            return tuple(
                (laters[n], carry[n][1] + jnp.dot(vt_ref[j, _pair_cols(hd), :], ws[n],
                                                  preferred_element_type=F32))
                for n, hd in enumerate(heads))

        init = tuple((jnp.zeros((1, tq), F32), jnp.zeros((LANES, tq), F32)) for _ in heads)
        def step(s, state, qk=qk, block=block):
            zs, carry = state
            j = i - 1 - s
            nxt = qk(jnp.maximum(j - 1, 0))
            return nxt, block(j, zs, carry, False)

        diag = qk(i)
        first = qk(jnp.maximum(i - 1, 0))
        carry = block(i, diag, init, True)
        _, carry = lax.fori_loop(0, i, step, (first, carry))
        _store_heads(o_ref, heads, [acc for _, acc in carry])


def _sb(q, k, vt, seq):
    t = q.shape[0]
    n_batch, tq, nq, q_spec, k_spec, vt_spec = _att_specs(t, seq)
    suf = (lax.broadcasted_iota(jnp.int32, (tq, tq), 1)
           > lax.broadcasted_iota(jnp.int32, (tq, tq), 0)).astype(MXU_DTYPE)
    return pl.pallas_call(
        _sb_kernel,
        grid=(n_batch, nq),
        in_specs=[q_spec, k_spec, vt_spec, pl.BlockSpec((tq, tq), lambda b, i: (0, 0))],
        out_specs=q_spec,
        out_shape=jax.ShapeDtypeStruct((t, ATT_WIDTH), MXU_DTYPE),
        compiler_params=_params(2),
        name="sb_attention",
    )(q, k, vt, suf)


def _lru_kernel(xc_ref, gc_ref, cw_ref, cb_ref, wa_ref, ba_ref, wx_ref, bx_ref, lam_ref,
                o_ref, ext_ref, a_ref, u_ref, h_ref, state_ref):
    s_idx = pl.program_id(1)
    ts = xc_ref.shape[0]
    halo = 8

    @pl.when(s_idx == 0)
    def _():
        ext_ref[0:halo, :] = jnp.zeros((halo, LRU_WIDTH), F32)
        state_ref[...] = jnp.zeros_like(state_ref)

    ext_ref[halo:halo + ts, :] = xc_ref[...]
    xconv = cb_ref[...]
    for j in range(CONV_WIDTH):
        lo = halo - (CONV_WIDTH - 1) + j
        xconv = xconv + cw_ref[j:j + 1, :] * ext_ref[lo:lo + ts, :]
    ext_ref[0:halo, :] = ext_ref[ts:ts + halo, :]

    xb = xconv.astype(MXU_DTYPE)
    r = jax.nn.sigmoid(jnp.dot(xb, wa_ref[...], preferred_element_type=F32) + ba_ref[...])
    ig = jax.nn.sigmoid(jnp.dot(xb, wx_ref[...], preferred_element_type=F32) + bx_ref[...])
    log_a = (-LRU_C * _softplus(-lam_ref[...])) * r
    a = jnp.exp(log_a)
    a_ref[...] = a
    u_ref[...] = jnp.sqrt(1.0 - a * a) * (ig * xconv)

    def step(t, h):
        h = a_ref[pl.ds(t, 1), :] * h + u_ref[pl.ds(t, 1), :]
        h_ref[pl.ds(t, 1), :] = h
        return h

    state_ref[0:1, :] = lax.fori_loop(0, ts, step, state_ref[0:1, :], unroll=8)
    o_ref[...] = (h_ref[...] * jax.nn.gelu(gc_ref[...])).astype(o_ref.dtype)


def _lru(xc, gc, conv_w, conv_b, wa_bd, ba, wx_bd, bx, lam, seq):
    t, w = xc.shape
    n_batch = t // seq
    ts = min(LRU_TILE, seq)
    ns = seq // ts
    row_spec = pl.BlockSpec((ts, w), lambda b, s: (b * ns + s, 0))
    vec_spec = pl.BlockSpec((1, w), lambda b, s: (0, 0))
    mat_spec = pl.BlockSpec((w, w), lambda b, s: (0, 0))
    return pl.pallas_call(
        _lru_kernel,
        grid=(n_batch, ns),
        in_specs=[row_spec, row_spec,
                  pl.BlockSpec((CONV_WIDTH, w), lambda b, s: (0, 0)), vec_spec,
                  mat_spec, vec_spec, mat_spec, vec_spec, vec_spec],
        out_specs=row_spec,
        out_shape=jax.ShapeDtypeStruct((t, w), MXU_DTYPE),
        scratch_shapes=[pltpu.VMEM((ts + 8, w), F32), pltpu.VMEM((ts, w), F32),
                        pltpu.VMEM((ts, w), F32), pltpu.VMEM((ts, w), F32),
                        pltpu.VMEM((8, w), F32)],
        compiler_params=_params(2),
        name="rg_lru",
    )(xc, gc, conv_w, conv_b, wa_bd, ba, wx_bd, bx, lam)


def _merge_kernel(x_ref, ada_ref, oa_ref, ob_ref, oc_ref, wg_ref, bg_ref, wpa_ref, wpb_ref,
                  wpc_ref, wo_ref, lng_ref, lnb_ref, wr_ref, br_ref,
                  x1_ref, h2_ref, logit_ref):
    d = x_ref.shape[1]
    x = x_ref[...]
    shift1, scale1, gate1 = ada_ref[0:1, :], ada_ref[1:2, :], ada_ref[2:3, :]
    shift2, scale2 = ada_ref[3:4, :], ada_ref[4:5, :]
    h = (x * (1.0 + scale1) + shift1).astype(MXU_DTYPE)
    merged = None
    for n, (o_ref, w_ref) in enumerate(((oa_ref, wpa_ref), (ob_ref, wpb_ref), (oc_ref, wpc_ref))):
        cs = slice(n * d, (n + 1) * d)
        g = jax.nn.sigmoid(jnp.dot(h, wg_ref[:, cs], preferred_element_type=F32) + bg_ref[:, cs])
        term = g * jnp.dot(o_ref[...], w_ref[...], preferred_element_type=F32)
        merged = term if merged is None else merged + term
    y = jnp.dot(merged.astype(MXU_DTYPE), wo_ref[...], preferred_element_type=F32)
    x1 = _layer_norm(DEEPNORM_ALPHA * x + (1.0 + gate1) * y, lng_ref[...], lnb_ref[...])
    x1_ref[...] = x1
    h2 = x1 * (1.0 + scale2) + shift2
    h2_ref[...] = _pack_halves(h2)
    logit_ref[...] = _nt_dot(wr_ref[...], h2.astype(MXU_DTYPE)) + br_ref[...]


def _merge(x2d, ada_l, o_a, o_b, o_c, w_gate, b_gate, w_pa, w_pb, w_pc, w_o, ln_g, ln_b,
           w_router_t, b_router, seq):
    t, d = x2d.shape
    tm = min(ROW_TILE, seq)
    tiles_per_seq = seq // tm
    row = lambda i: (i, 0)
    whole = lambda a: pl.BlockSpec(a.shape, lambda i: (0,) * a.ndim)
    return pl.pallas_call(
        _merge_kernel,
        grid=(t // tm,),
        in_specs=[
            pl.BlockSpec((tm, d), row),
            pl.BlockSpec((None, 6, d), lambda i: (i // tiles_per_seq, 0, 0)),
            pl.BlockSpec((tm, ATT_WIDTH), row), pl.BlockSpec((tm, ATT_WIDTH), row),
            pl.BlockSpec((tm, LRU_WIDTH), row),
            whole(w_gate), whole(b_gate), whole(w_pa), whole(w_pb), whole(w_pc), whole(w_o),
            whole(ln_g), whole(ln_b), whole(w_router_t), whole(b_router),
        ],
        out_specs=[pl.BlockSpec((tm, d), row), pl.BlockSpec((tm, d // 2), row),
                   pl.BlockSpec((N_EXPERTS, tm), lambda i: (0, i))],
        out_shape=[jax.ShapeDtypeStruct((t, d), F32),
                   jax.ShapeDtypeStruct((t, d // 2), jnp.uint32),
                   jax.ShapeDtypeStruct((N_EXPERTS, t), F32)],
        compiler_params=_params(1),
        name="merge_outproj_ln",
    )(x2d, ada_l, o_a, o_b, o_c, w_gate, b_gate, w_pa, w_pb, w_pc, w_o, ln_g, ln_b,
      w_router_t, b_router)


def _route_kernel(logit_ref, tri_ref, eid_ref, gate_ref, pos_ref, cnt_ref, carry_ref):
    i = pl.program_id(0)
    tr = logit_ref.shape[1]

    @pl.when(i == 0)
    def _():
        carry_ref[...] = jnp.zeros_like(carry_ref)

    erow = lax.broadcasted_iota(jnp.int32, (N_EXPERTS, tr), 0)
    cur = logit_ref[...]
    vals, ids = [], []
    for _ in range(TOP_K):
        m = jnp.max(cur, axis=0, keepdims=True)
        idx = jnp.min(jnp.where(cur == m, erow, N_EXPERTS), axis=0, keepdims=True)
        vals.append(m)
        ids.append(idx)
        cur = jnp.where(erow == idx, -jnp.inf, cur)
    exps = [jnp.exp(v - vals[0]) for v in vals]
    denom = exps[0] + exps[1] + exps[2] + exps[3]
    chosen = jnp.zeros((N_EXPERTS, tr), F32)
    for idx in ids:
        chosen = chosen + jnp.where(erow == idx, 1.0, 0.0)
    prefix = (jnp.dot(chosen.astype(MXU_DTYPE), tri_ref[...], preferred_element_type=F32)
              + carry_ref[:, 0:1])
    for k in range(TOP_K):
        eid_ref[k:k + 1, :] = ids[k]
        gate_ref[k:k + 1, :] = exps[k] / denom
        pos_ref[k:k + 1, :] = jnp.sum(jnp.where(erow == ids[k], prefix, 0.0), axis=0,
                                      keepdims=True).astype(jnp.int32)
    carry_ref[...] = carry_ref[...] + jnp.sum(chosen, axis=1, keepdims=True)
    cnt_ref[...] = carry_ref[...].astype(jnp.int32)


def _route(logits_t):
    n_e, t = logits_t.shape
    tr = min(ROUTE_TILE, t)
    tri = (lax.broadcasted_iota(jnp.int32, (tr, tr), 0)
           < lax.broadcasted_iota(jnp.int32, (tr, tr), 1)).astype(MXU_DTYPE)
    tok = lambda i: (0, i)
    return pl.pallas_call(
        _route_kernel,
        grid=(t // tr,),
        in_specs=[pl.BlockSpec((n_e, tr), tok), pl.BlockSpec((tr, tr), lambda i: (0, 0))],
        out_specs=[pl.BlockSpec((TOP_K, tr), tok)] * 3
        + [pl.BlockSpec((n_e, LANES), lambda i: (0, 0))],
        out_shape=[jax.ShapeDtypeStruct((TOP_K, t), jnp.int32),
                   jax.ShapeDtypeStruct((TOP_K, t), F32),
                   jax.ShapeDtypeStruct((TOP_K, t), jnp.int32),
                   jax.ShapeDtypeStruct((n_e, LANES), jnp.int32)],
        scratch_shapes=[pltpu.VMEM((n_e, LANES), F32)],
        compiler_params=_params(1),
        name="route_topk",
    )(logits_t, tri)


def _dest_kernel(start_ref, eid_ref, pos_ref, dest_ref):
    eid = eid_ref[...]
    dest = pos_ref[...]
    for e in range(N_EXPERTS):
        dest = jnp.where(eid == e, dest + start_ref[e], dest)
    dest_ref[...] = dest


def _dest(pad_start, eid, pos):
    k, t = eid.shape
    tr = min(ROUTE_TILE, t)
    spec = pl.BlockSpec((k, tr), lambda i, s: (0, i))
    return pl.pallas_call(
        _dest_kernel,
        grid_spec=pltpu.PrefetchScalarGridSpec(
            num_scalar_prefetch=1, grid=(t // tr,), in_specs=[spec, spec], out_specs=spec),
        out_shape=jax.ShapeDtypeStruct((k, t), jnp.int32),
        compiler_params=_params(1),
        name="route_dest",
    )(pad_start, eid, pos)


def _dispatch_kernel(dest_ref, h_ref, init_ref, buf_ref, sem):
    del init_ref
    tm = h_ref.shape[0]

    def row_copy(r, k):
        return pltpu.make_async_copy(h_ref.at[pl.ds(r, 1), :],
                                     buf_ref.at[pl.ds(dest_ref[r * TOP_K + k], 1), :], sem)

    def issue(r, _):
        for k in range(TOP_K):
            row_copy(r, k).start(priority=k % 2)
        return 0

    lax.fori_loop(0, tm, issue, 0)
    for k in range(TOP_K):
        pltpu.make_async_copy(h_ref, buf_ref.at[pl.ds(0, tm), :], sem).wait()


def _dispatch(dest, h2, n_slots):
    t, d = h2.shape
    tm = min(MOVE_TILE, t)
    init = jnp.zeros((n_slots, d), h2.dtype)
    return pl.pallas_call(
        _dispatch_kernel,
        grid=(t // tm,),
        in_specs=[
            pl.BlockSpec((TOP_K * tm,), lambda i: (i,), memory_space=pltpu.SMEM),
            pl.BlockSpec((tm, d), lambda i: (i, 0)),
            pl.BlockSpec(memory_space=pl.ANY),
        ],
        out_specs=pl.BlockSpec(memory_space=pl.ANY),
        out_shape=jax.ShapeDtypeStruct((n_slots, d), h2.dtype),
        scratch_shapes=[pltpu.SemaphoreType.DMA(())],
        input_output_aliases={2: 0},
        compiler_params=_params(1),
        name="moe_dispatch",
    )(dest, h2, init)


def _expert_kernel(be_ref, nused_ref, x_ref, wgu_ref, bgu_ref, wd_ref, bd_ref, y_ref,
                   wgu_lp, wd_lp):
    i = pl.program_id(0)
    d_e = wd_ref.shape[0]

    @pl.when((i == 0) | (be_ref[i] != be_ref[jnp.maximum(i - 1, 0)]))
    def _():
        wgu_lp[...] = wgu_ref[...].astype(MXU_DTYPE)
        wd_lp[...] = wd_ref[...].astype(MXU_DTYPE)

    @pl.when(i < nused_ref[0])
    def _():
        lo, hi = _unpack_halves(x_ref[...])
        x = jnp.concatenate([lo, hi], axis=1).astype(MXU_DTYPE)
        gu = jnp.dot(x, wgu_lp[...], preferred_element_type=F32) + bgu_ref[...]
        gate = jnp.minimum(gu[:, :d_e], SWIGLU_LIMIT)
        up = jnp.clip(gu[:, d_e:], -SWIGLU_LIMIT, SWIGLU_LIMIT)
        act = (up + 1.0) * (gate * jax.nn.sigmoid(SWIGLU_ALPHA * gate))
        y = jnp.dot(act.astype(MXU_DTYPE), wd_lp[...], preferred_element_type=F32) + bd_ref[...]
        y_ref[...] = _pack_halves(y)

    @pl.when(i >= nused_ref[0])
    def _():
        y_ref[...] = jnp.zeros_like(y_ref)


def _experts(layer, block_e, n_used, x_buf, w_gu, b_gu, w_down, b_down):
    n_slots, half = x_buf.shape
    _, n_e, d, d2 = w_gu.shape
    d_e = w_down.shape[2]
    n_blocks = n_slots // EXPERT_BLOCK
    n_layers = w_gu.shape[0]
    return pl.pallas_call(
        _expert_kernel,
        grid_spec=pltpu.PrefetchScalarGridSpec(
            num_scalar_prefetch=2,
            grid=(n_blocks,),
            in_specs=[
                pl.BlockSpec((EXPERT_BLOCK, half), lambda i, be, nu: (i, 0)),
                pl.BlockSpec((None, None, d, d2), lambda i, be, nu: (layer, be[i], 0, 0)),
                pl.BlockSpec((None, None, 1, d2), lambda i, be, nu: (layer, be[i], 0, 0)),
                pl.BlockSpec((None, None, d_e, d), lambda i, be, nu: (layer, be[i], 0, 0)),
                pl.BlockSpec((None, None, 1, d), lambda i, be, nu: (layer, be[i], 0, 0)),
            ],
            out_specs=pl.BlockSpec((EXPERT_BLOCK, half), lambda i, be, nu: (i, 0)),
            scratch_shapes=[pltpu.VMEM((d, d2), MXU_DTYPE), pltpu.VMEM((d_e, d), MXU_DTYPE)],
        ),
        out_shape=jax.ShapeDtypeStruct((n_slots, half), jnp.uint32),
        compiler_params=_params(1),
        name="moe_experts",
    )(block_e, n_used, x_buf, w_gu, b_gu.reshape(n_layers, n_e, 1, d2), w_down,
      b_down.reshape(n_layers, n_e, 1, d))


def _combine_kernel(dest_ref, x1_ref, ada_ref, gate_ref, lng_ref, lnb_ref, ybuf_ref,
                    o_ref, rows_ref, sem):
    tm = x1_ref.shape[0]
    half = ybuf_ref.shape[1]

    def row_copy(r, k):
        return pltpu.make_async_copy(ybuf_ref.at[pl.ds(dest_ref[r * TOP_K + k], 1), :],
                                     rows_ref.at[pl.ds(r, 1), pl.ds(k * half, half)], sem)

    def issue(r, _):
        for k in range(TOP_K):
            row_copy(r, k).start(priority=k % 2)
        return 0

    lax.fori_loop(0, tm, issue, 0)
    for k in range(TOP_K):
        pltpu.make_async_copy(ybuf_ref.at[pl.ds(0, tm), :],
                              rows_ref.at[:, pl.ds(k * half, half)], sem).wait()
    gates = gate_ref[...]
    y_lo = y_hi = None
    for k in range(TOP_K):
        lo, hi = _unpack_halves(rows_ref[:, k * half:(k + 1) * half])
        g = gates[:, k:k + 1]
        y_lo = lo * g if y_lo is None else y_lo + lo * g
        y_hi = hi * g if y_hi is None else y_hi + hi * g
    y = jnp.concatenate([y_lo, y_hi], axis=1)
    gate2 = ada_ref[5:6, :]
    o_ref[...] = _layer_norm(DEEPNORM_ALPHA * x1_ref[...] + (1.0 + gate2) * y,
                             lng_ref[...], lnb_ref[...])


def _combine(dest, x1, ada_l, gates_tk, ln_g, ln_b, y_buf, seq):
    t, d = x1.shape
    tm = min(MOVE_TILE, seq)
    tiles_per_seq = seq // tm
    return pl.pallas_call(
        _combine_kernel,
        grid=(t // tm,),
        in_specs=[
            pl.BlockSpec((TOP_K * tm,), lambda i: (i,), memory_space=pltpu.SMEM),
            pl.BlockSpec((tm, d), lambda i: (i, 0)),
            pl.BlockSpec((None, 6, d), lambda i: (i // tiles_per_seq, 0, 0)),
            pl.BlockSpec((tm, TOP_K), lambda i: (i, 0)),
            pl.BlockSpec((1, d), lambda i: (0, 0)),
            pl.BlockSpec((1, d), lambda i: (0, 0)),
            pl.BlockSpec(memory_space=pl.ANY),
        ],
        out_specs=pl.BlockSpec((tm, d), lambda i: (i, 0)),
        out_shape=jax.ShapeDtypeStruct((t, d), F32),
        scratch_shapes=[pltpu.VMEM((tm, TOP_K * (d // 2)), jnp.uint32),
                        pltpu.SemaphoreType.DMA(())],
        compiler_params=_params(1),
        name="moe_combine_ln",
    )(dest, x1, ada_l, gates_tk, ln_g, ln_b, y_buf)


def _split_w_in(w_in):
    d = w_in.shape[0]
    sizes = (ATT_WIDTH, ATT_WIDTH, ATT_WIDTH, N_HEADS, ATT_WIDTH, ATT_WIDTH, ATT_WIDTH,
             LRU_WIDTH, LRU_WIDTH)
    bounds = [sum(sizes[:n]) for n in range(len(sizes) + 1)]
    qa, ka, va, fa, qb, kb, vb, xc, gc = (
        w_in[:, bounds[n]:bounds[n + 1]] for n in range(len(sizes)))
    forget = jnp.concatenate([fa, jnp.zeros((d, LANES - N_HEADS), w_in.dtype)], axis=1)
    w_main = jnp.concatenate([qa, ka, qb, kb, xc, gc, forget], axis=1).astype(MXU_DTYPE)
    w_vt = jnp.concatenate([va, vb], axis=1).T.astype(MXU_DTYPE)
    return w_main, w_vt


def _block_diag(w):
    n, c, dd = w.shape
    eye = jnp.eye(n, dtype=w.dtype)
    return (eye[:, None, :, None] * w[:, :, None, :]).reshape(n * c, n * dd).astype(MXU_DTYPE)


def kernel(x, c, w_ada, b_ada, ln1_g, ln1_b, w_in, b_f, conv_w, conv_b, lru_wa, lru_ba, lru_wx,
           lru_bx, lru_lambda, w_gate, b_gate, w_pa, w_pb, w_pc, w_o, ln2_g, ln2_b, w_router,
           b_router, w_gu, b_gu, w_down, b_down):
    n_batch, seq, d = x.shape
    t = n_batch * seq
    n_layers = w_ada.shape[0]
    n_blocks = (t * TOP_K) // EXPERT_BLOCK + N_EXPERTS
    n_slots = n_blocks * EXPERT_BLOCK
    att_tile = min(ATT_TILE, seq)
    vec = lambda a: a.reshape(1, -1)

    ada = _ada(c, w_ada, b_ada).reshape(n_layers, n_batch, 6, d)
    x2d = x.reshape(t, d)
    for l in range(n_layers):
        ada_l = ada[l]
        bf_pad = jnp.concatenate([b_f[l], jnp.zeros((LANES - N_HEADS,), F32)]).reshape(1, LANES)
        w_main, w_vt = _split_w_in(w_in[l])
        qa, ka, qb, kb, xc, gc, vat, vbt, fcol, frow = _inproj(
            x2d, ada_l, w_main, w_vt, bf_pad, seq)
        frow4 = frow.reshape(n_batch, N_HEADS, seq // att_tile, att_tile)
        o_a = _fox(qa, ka, vat, fcol, frow4, seq)
        o_b = _sb(qb, kb, vbt, seq)
        o_c = _lru(xc, gc, conv_w[l], vec(conv_b[l]), _block_diag(lru_wa[l]), vec(lru_ba[l]),
                   _block_diag(lru_wx[l]), vec(lru_bx[l]), vec(lru_lambda[l]), seq)
        x1, h2, logits_t = _merge(
            x2d, ada_l, o_a, o_b, o_c, w_gate[l].astype(MXU_DTYPE), vec(b_gate[l]),
            w_pa[l].astype(MXU_DTYPE), w_pb[l].astype(MXU_DTYPE), w_pc[l].astype(MXU_DTYPE),
            w_o[l].astype(MXU_DTYPE), vec(ln1_g[l]), vec(ln1_b[l]),
            w_router[l].T.astype(MXU_DTYPE), b_router[l].reshape(N_EXPERTS, 1), seq)

        eid, gates, pos, cnt = _route(logits_t)
        counts = cnt[:, 0]
        padded = (counts + EXPERT_BLOCK - 1) // EXPERT_BLOCK * EXPERT_BLOCK
        pad_end = jnp.cumsum(padded)
        pad_start = (pad_end - padded).astype(jnp.int32)
        block_first = (jnp.arange(n_blocks) * EXPERT_BLOCK)[:, None]
        block_e = jnp.minimum(jnp.sum(pad_end[None, :] <= block_first, axis=1),
                              N_EXPERTS - 1).astype(jnp.int32)
        n_used = (pad_end[-1:] // EXPERT_BLOCK).astype(jnp.int32)
        dest = _dest(pad_start, eid, pos).T.reshape(-1)

        x_buf = _dispatch(dest, h2, n_slots)
        y_buf = _experts(l, block_e, n_used, x_buf, w_gu, b_gu, w_down, b_down)
        x2d = _combine(dest, x1, ada_l, gates.T, vec(ln2_g[l]), vec(ln2_b[l]), y_buf, seq)
    return x2d.reshape(n_batch, seq, d)
```

```python
import functools

import jax
import jax.numpy as jnp
from jax import lax
from jax.experimental import pallas as pl
from jax.experimental.pallas import tpu as pltpu

D_MODEL = 1024
DEPTH = 2
HEAD_DIM = 64
N_HEADS = 8
ATT_WIDTH = N_HEADS * HEAD_DIM
LRU_WIDTH = D_MODEL
LRU_BLOCKS = 16
CONV_WIDTH = 4
LRU_C = 8.0
N_EXPERTS = 32
TOP_K = 4
SWIGLU_LIMIT = 7.0
SWIGLU_ALPHA = 1.702
LN_EPS = 1e-5
DEEPNORM_ALPHA = (2.0 * DEPTH) ** 0.25
ATT_SCALE = HEAD_DIM ** -0.5
LOG2_E = 1.4426950408889634

LANES = 128
MXU_DTYPE = jnp.bfloat16
F32 = jnp.float32

ROW_TILE = 512
ATT_TILE = 256
ATT_GROUP = 8
LRU_TILE = 256
ROUTE_TILE = 512
EXPERT_BLOCK = 256
PAIR_ROWS = 2 * EXPERT_BLOCK
MOVE_TILE = 256
VMEM_LIMIT = 56 * 1024 * 1024


def _params(n_axes, vmem=VMEM_LIMIT):
    return pltpu.CompilerParams(
        dimension_semantics=("arbitrary",) * n_axes, vmem_limit_bytes=vmem)


def _log_sigmoid(x):
    return jnp.minimum(x, 0.0) - jnp.log1p(jnp.exp(-jnp.abs(x)))


def _softplus(x):
    return jnp.maximum(x, 0.0) + jnp.log1p(jnp.exp(-jnp.abs(x)))


def _layer_norm(v, g, b):
    mu = jnp.mean(v, axis=-1, keepdims=True)
    d = v - mu
    var = jnp.mean(d * d, axis=-1, keepdims=True)
    return d * lax.rsqrt(var + LN_EPS) * g + b


def _pack_halves(x):
    n = x.shape[1] // 2
    bits = lambda v: lax.bitcast_convert_type(v.astype(jnp.bfloat16).astype(F32), jnp.uint32)
    return (bits(x[:, :n]) >> 16) | bits(x[:, n:])


def _unpack_halves(p):
    lo = lax.bitcast_convert_type(p << 16, F32)
    hi = lax.bitcast_convert_type(p & jnp.uint32(0xFFFF0000), F32)
    return lo, hi


def _split3(x):
    hi = x.astype(MXU_DTYPE)
    r1 = x - hi.astype(F32)
    mid = r1.astype(MXU_DTYPE)
    lo = (r1 - mid.astype(F32)).astype(MXU_DTYPE)
    return hi, mid, lo


def _ada_kernel(c_ref, w_ref, b_ref, o_ref):
    c = c_ref[...]
    cond = c * jax.nn.sigmoid(c)
    o_ref[...] = jnp.dot(cond.astype(MXU_DTYPE), w_ref[...].astype(MXU_DTYPE),
                         preferred_element_type=F32) + b_ref[...]


def _ada(c, w_ada, b_ada):
    n_layers, d, n6 = w_ada.shape
    b = c.shape[0]
    tn = n6 // 4
    return pl.pallas_call(
        _ada_kernel,
        grid=(n_layers, n6 // tn),
        in_specs=[
            pl.BlockSpec((b, d), lambda l, j: (0, 0)),
            pl.BlockSpec((None, d, tn), lambda l, j: (l, 0, j)),
            pl.BlockSpec((None, 1, tn), lambda l, j: (l, 0, j)),
        ],
        out_specs=pl.BlockSpec((None, b, tn), lambda l, j: (l, 0, j)),
        out_shape=jax.ShapeDtypeStruct((n_layers, b, n6), F32),
        compiler_params=_params(2),
        name="ada",
    )(c, w_ada, b_ada.reshape(n_layers, 1, n6))


def _nt_dot(a, b):
    return lax.dot_general(a, b, (((1,), (1,)), ((), ())), preferred_element_type=F32)


def _inproj_kernel(x_ref, ada_ref, w_ref, wvt_ref, bf_ref, tri_ref,
                   qa_ref, ka_ref, qb_ref, kb_ref, xc_ref, gc_ref, vat_ref, vbt_ref,
                   fcol_ref, frow_ref, carry_ref, *, tiles_per_seq):
    i = pl.program_id(0)
    tm = x_ref.shape[0]
    shift = ada_ref[0:1, :]
    scale = ada_ref[1:2, :]
    h = (x_ref[...] * (1.0 + scale) + shift).astype(MXU_DTYPE)
    off = 0
    for ref in (qa_ref, ka_ref, qb_ref, kb_ref, xc_ref, gc_ref):
        width = ref.shape[1]
        ref[...] = jnp.dot(h, w_ref[:, off:off + width],
                           preferred_element_type=F32).astype(ref.dtype)
        off += width
    vt = _nt_dot(wvt_ref[...], h).astype(MXU_DTYPE)
    chunk = vat_ref.shape[2]
    for n in range(tm // chunk):
        cols = slice(n * chunk, (n + 1) * chunk)
        vat_ref[n] = vt[0:ATT_WIDTH, cols]
        vbt_ref[n] = vt[ATT_WIDTH:2 * ATT_WIDTH, cols]
    zf = jnp.dot(h, w_ref[:, off:off + LANES], preferred_element_type=F32) + bf_ref[...]
    logf = _log_sigmoid(zf)

    @pl.when(i % tiles_per_seq == 0)
    def _():
        carry_ref[...] = jnp.zeros_like(carry_ref)

    tri = tri_ref[...]
    f_cum = carry_ref[0:1, :]
    for piece in _split3(logf):
        f_cum = f_cum + jnp.dot(tri, piece, preferred_element_type=F32)
    carry_ref[0:1, :] = f_cum[tm - 1:tm, :]
    fcol_ref[...] = f_cum
    frow_ref[...] = f_cum.T[0:N_HEADS, :]


def _inproj(x2d, ada_l, w_main, w_vt, bf_pad, seq):
    t, d = x2d.shape
    tm = min(ROW_TILE, seq)
    chunk = min(ATT_TILE, seq)
    tiles_per_seq = seq // tm
    n_batch = t // seq
    tri = (lax.broadcasted_iota(jnp.int32, (tm, tm), 1)
           <= lax.broadcasted_iota(jnp.int32, (tm, tm), 0)).astype(MXU_DTYPE)
    row = lambda i: (i, 0)
    att = jax.ShapeDtypeStruct((t, ATT_WIDTH), MXU_DTYPE)
    att_t = jax.ShapeDtypeStruct((t // chunk, ATT_WIDTH, chunk), MXU_DTYPE)
    wide = jax.ShapeDtypeStruct((t, LRU_WIDTH), F32)
    att_spec = pl.BlockSpec((tm, ATT_WIDTH), row)
    att_t_spec = pl.BlockSpec((tm // chunk, ATT_WIDTH, chunk), lambda i: (i, 0, 0))
    wide_spec = pl.BlockSpec((tm, LRU_WIDTH), row)
    return pl.pallas_call(
        functools.partial(_inproj_kernel, tiles_per_seq=tiles_per_seq),
        grid=(t // tm,),
        in_specs=[
            pl.BlockSpec((tm, d), row),
            pl.BlockSpec((None, 6, d), lambda i: (i // tiles_per_seq, 0, 0)),
            pl.BlockSpec(w_main.shape, lambda i: (0, 0)),
            pl.BlockSpec(w_vt.shape, lambda i: (0, 0)),
            pl.BlockSpec((1, LANES), lambda i: (0, 0)),
            pl.BlockSpec((tm, tm), lambda i: (0, 0)),
        ],
        out_specs=[att_spec] * 4 + [wide_spec] * 2 + [att_t_spec] * 2 + [
            pl.BlockSpec((tm, LANES), row),
            pl.BlockSpec((None, N_HEADS, tm),
                         lambda i: (i // tiles_per_seq, 0, i % tiles_per_seq)),
        ],
        out_shape=[att] * 4 + [wide] * 2 + [att_t] * 2 + [
            jax.ShapeDtypeStruct((t, LANES), F32),
            jax.ShapeDtypeStruct((n_batch, N_HEADS, seq), F32),
        ],
        scratch_shapes=[pltpu.VMEM((8, LANES), F32)],
        compiler_params=_params(1),
        name="inproj",
    )(x2d, ada_l, w_main, w_vt, bf_pad, tri)


def _head_query(q2, c):
    lane = lax.broadcasted_iota(jnp.int32, (1, LANES), 1)
    in_head = (lane >= c * HEAD_DIM) & (lane < (c + 1) * HEAD_DIM)
    return jnp.where(in_head, q2, jnp.zeros_like(q2)) * ATT_SCALE


def _pair_cols(head):
    return slice((head // 2) * LANES, (head // 2 + 1) * LANES)


def _store_heads(o_ref, heads, outs_t):
    sub = lax.broadcasted_iota(jnp.int32, (LANES, 1), 0)
    for n in range(0, len(heads), 2):
        pair_t = jnp.where(sub < HEAD_DIM, outs_t[n], outs_t[n + 1])
        o_ref[:, _pair_cols(heads[n])] = pair_t.T.astype(o_ref.dtype)


def _fox_kernel(q_ref, k_ref, vt_ref, fcol_ref, frow_ref, o_ref):
    i = pl.program_id(1)
    tq = q_ref.shape[0]
    key = lax.broadcasted_iota(jnp.int32, (tq, tq), 0)
    qry = lax.broadcasted_iota(jnp.int32, (tq, tq), 1)
    causal = key <= qry
    for g in range(N_HEADS // ATT_GROUP):
        heads = list(range(g * ATT_GROUP, (g + 1) * ATT_GROUP))
        qms = [_head_query(q_ref[:, _pair_cols(hd)], hd % 2) for hd in heads]
        fqs = [frow_ref[hd, pl.ds(i, 1), :] for hd in heads]

        def qk(j, heads=heads, qms=qms):
            start = pl.multiple_of(j * tq, tq)
            return tuple(_nt_dot(k_ref[pl.ds(start, tq), _pair_cols(hd)], qms[n])
                         for n, hd in enumerate(heads))

        def block(j, scores, carry, masked, heads=heads, fqs=fqs):
            start = pl.multiple_of(j * tq, tq)
            stats, probs = [], []
            for n, hd in enumerate(heads):
                m, l, _ = carry[n]
                fk = fcol_ref[pl.ds(start, tq), hd:hd + 1]
                s = scores[n] + (fqs[n] - fk)
                if masked:
                    s = jnp.where(causal, s, -jnp.inf)
                m_new = jnp.maximum(m, jnp.max(s, axis=0, keepdims=True))
                alpha = jnp.exp(m - m_new)
                pm = jnp.exp(s - m_new)
                l = alpha * l + jnp.sum(pm, axis=0, keepdims=True)
                stats.append((m_new, l, alpha))
                probs.append(pm.astype(MXU_DTYPE))
            pvs = [jnp.dot(vt_ref[j, _pair_cols(hd), :], probs[n],
                           preferred_element_type=F32) for n, hd in enumerate(heads)]
            return tuple((stats[n][0], stats[n][1], stats[n][2] * carry[n][2] + pvs[n])
                         for n in range(len(heads)))

        init = tuple((jnp.full((1, tq), -jnp.inf, F32), jnp.zeros((1, tq), F32),
                      jnp.zeros((LANES, tq), F32)) for _ in heads)
        carry = lax.fori_loop(
            0, i, lambda j, cr, qk=qk, block=block: block(j, qk(j), cr, False), init)
        carry = block(i, qk(i), carry, True)
        _store_heads(o_ref, heads, [acc / l for _, l, acc in carry])


def _att_specs(t, seq):
    n_batch = t // seq
    tq = min(ATT_TILE, seq)
    nq = seq // tq
    q_spec = pl.BlockSpec((tq, ATT_WIDTH), lambda b, i: (b * nq + i, 0))
    k_spec = pl.BlockSpec((seq, ATT_WIDTH), lambda b, i: (b, 0))
    vt_spec = pl.BlockSpec((nq, ATT_WIDTH, tq), lambda b, i: (b, 0, 0))
    return n_batch, tq, nq, q_spec, k_spec, vt_spec


def _fox(q, k, vt, fcol, frow4, seq):
    t = q.shape[0]
    n_batch, tq, nq, q_spec, k_spec, vt_spec = _att_specs(t, seq)
    return pl.pallas_call(
        _fox_kernel,
        grid=(n_batch, nq),
        in_specs=[
            q_spec, k_spec, vt_spec,
            pl.BlockSpec((seq, LANES), lambda b, i: (b, 0)),
            pl.BlockSpec((None, N_HEADS, nq, tq), lambda b, i: (b, 0, 0, 0)),
        ],
        out_specs=q_spec,
        out_shape=jax.ShapeDtypeStruct((t, ATT_WIDTH), MXU_DTYPE),
        compiler_params=_params(2),
        name="fox_attention",
    )(q, k, vt, fcol, frow4)


def _sb_kernel(q_ref, k_ref, vt_ref, suf_ref, o_ref):
    i = pl.program_id(1)
    tq = q_ref.shape[0]
    key = lax.broadcasted_iota(jnp.int32, (tq, tq), 0)
    qry = lax.broadcasted_iota(jnp.int32, (tq, tq), 1)
    strict = key < qry
    suf = suf_ref[...]
    for g in range(N_HEADS // ATT_GROUP):
        heads = list(range(g * ATT_GROUP, (g + 1) * ATT_GROUP))
        qms = [_head_query(q_ref[:, _pair_cols(hd)], hd % 2) for hd in heads]

        def qk(j, heads=heads, qms=qms):
            start = pl.multiple_of(j * tq, tq)
            return tuple(_nt_dot(k_ref[pl.ds(start, tq), _pair_cols(hd)], qms[n])
                         for n, hd in enumerate(heads))

        def block(j, zs, carry, masked, heads=heads):
            log_betas, splits, laters = [], [], []
            for n in range(len(heads)):
                z = zs[n]
                sp = jnp.maximum(z, 0.0) + jnp.log(1.0 + jnp.exp2(jnp.abs(z) * -LOG2_E))
                log_betas.append(z - sp)
                if masked:
                    sp = jnp.where(strict, sp, 0.0)
                splits.append(sp.astype(MXU_DTYPE))
                laters.append(carry[n][0] + jnp.sum(sp, axis=0, keepdims=True))
            afters = [jnp.dot(suf, splits[n], preferred_element_type=F32)
                      for n in range(len(heads))]
            ws = []
            for n in range(len(heads)):
                w = jnp.exp(log_betas[n] - jnp.maximum(afters[n] + carry[n][0], 0.0))
                if masked:
                    w = jnp.where(strict, w, 0.0)
                ws.append(w.astype(MXU_DTYPE))
            return tuple(
                (laters[n], carry[n][1] + jnp.dot(vt_ref[j, _pair_cols(hd), :], ws[n],
                                                  preferred_element_type=F32))
                for n, hd in enumerate(heads))

        init = tuple((jnp.zeros((1, tq), F32), jnp.zeros((LANES, tq), F32)) for _ in heads)
        def step(s, state, qk=qk, block=block):
            zs, carry = state
            j = i - 1 - s
            nxt = qk(jnp.maximum(j - 1, 0))
            return nxt, block(j, zs, carry, False)

        diag = qk(i)
        first = qk(jnp.maximum(i - 1, 0))
        carry = block(i, diag, init, True)
        _, carry = lax.fori_loop(0, i, step, (first, carry))
        _store_heads(o_ref, heads, [acc for _, acc in carry])


def _sb(q, k, vt, seq):
    t = q.shape[0]
    n_batch, tq, nq, q_spec, k_spec, vt_spec = _att_specs(t, seq)
    suf = (lax.broadcasted_iota(jnp.int32, (tq, tq), 1)
           > lax.broadcasted_iota(jnp.int32, (tq, tq), 0)).astype(MXU_DTYPE)
    return pl.pallas_call(
        _sb_kernel,
        grid=(n_batch, nq),
        in_specs=[q_spec, k_spec, vt_spec, pl.BlockSpec((tq, tq), lambda b, i: (0, 0))],
        out_specs=q_spec,
        out_shape=jax.ShapeDtypeStruct((t, ATT_WIDTH), MXU_DTYPE),
        compiler_params=_params(2),
        name="sb_attention",
    )(q, k, vt, suf)


def _lru_kernel(xc_ref, gc_ref, cw_ref, cb_ref, wa_ref, ba_ref, wx_ref, bx_ref, lam_ref,
                o_ref, ext_ref, a_ref, u_ref, h_ref, state_ref):
    s_idx = pl.program_id(1)
    ts = xc_ref.shape[0]
    halo = 8

    @pl.when(s_idx == 0)
    def _():
        ext_ref[0:halo, :] = jnp.zeros((halo, LRU_WIDTH), F32)
        state_ref[...] = jnp.zeros_like(state_ref)

    ext_ref[halo:halo + ts, :] = xc_ref[...]
    xconv = cb_ref[...]
    for j in range(CONV_WIDTH):
        lo = halo - (CONV_WIDTH - 1) + j
        xconv = xconv + cw_ref[j:j + 1, :] * ext_ref[lo:lo + ts, :]
    ext_ref[0:halo, :] = ext_ref[ts:ts + halo, :]

    xb = xconv.astype(MXU_DTYPE)
    r = jax.nn.sigmoid(jnp.dot(xb, wa_ref[...], preferred_element_type=F32) + ba_ref[...])
    ig = jax.nn.sigmoid(jnp.dot(xb, wx_ref[...], preferred_element_type=F32) + bx_ref[...])
    log_a = (-LRU_C * _softplus(-lam_ref[...])) * r
    a = jnp.exp(log_a)
    a_ref[...] = a
    u_ref[...] = jnp.sqrt(1.0 - a * a) * (ig * xconv)

    def step(t, h):
        h = a_ref[pl.ds(t, 1), :] * h + u_ref[pl.ds(t, 1), :]
        h_ref[pl.ds(t, 1), :] = h
        return h

    state_ref[0:1, :] = lax.fori_loop(0, ts, step, state_ref[0:1, :], unroll=8)
    o_ref[...] = (h_ref[...] * jax.nn.gelu(gc_ref[...])).astype(o_ref.dtype)


def _lru(xc, gc, conv_w, conv_b, wa_bd, ba, wx_bd, bx, lam, seq):
    t, w = xc.shape
    n_batch = t // seq
    ts = min(LRU_TILE, seq)
    ns = seq // ts
    row_spec = pl.BlockSpec((ts, w), lambda b, s: (b * ns + s, 0))
    vec_spec = pl.BlockSpec((1, w), lambda b, s: (0, 0))
    mat_spec = pl.BlockSpec((w, w), lambda b, s: (0, 0))
    return pl.pallas_call(
        _lru_kernel,
        grid=(n_batch, ns),
        in_specs=[row_spec, row_spec,
                  pl.BlockSpec((CONV_WIDTH, w), lambda b, s: (0, 0)), vec_spec,
                  mat_spec, vec_spec, mat_spec, vec_spec, vec_spec],
        out_specs=row_spec,
        out_shape=jax.ShapeDtypeStruct((t, w), MXU_DTYPE),
        scratch_shapes=[pltpu.VMEM((ts + 8, w), F32), pltpu.VMEM((ts, w), F32),
                        pltpu.VMEM((ts, w), F32), pltpu.VMEM((ts, w), F32),
                        pltpu.VMEM((8, w), F32)],
        compiler_params=_params(2),
        name="rg_lru",
    )(xc, gc, conv_w, conv_b, wa_bd, ba, wx_bd, bx, lam)


def _merge_kernel(x_ref, ada_ref, oa_ref, ob_ref, oc_ref, wg_ref, bg_ref, wpa_ref, wpb_ref,
                  wpc_ref, wo_ref, lng_ref, lnb_ref, wr_ref, br_ref,
                  x1_ref, h2_ref, logit_ref):
    d = x_ref.shape[1]
    x = x_ref[...]
    shift1, scale1, gate1 = ada_ref[0:1, :], ada_ref[1:2, :], ada_ref[2:3, :]
    shift2, scale2 = ada_ref[3:4, :], ada_ref[4:5, :]
    h = (x * (1.0 + scale1) + shift1).astype(MXU_DTYPE)
    merged = None
    for n, (o_ref, w_ref) in enumerate(((oa_ref, wpa_ref), (ob_ref, wpb_ref), (oc_ref, wpc_ref))):
        cs = slice(n * d, (n + 1) * d)
        g = jax.nn.sigmoid(jnp.dot(h, wg_ref[:, cs], preferred_element_type=F32) + bg_ref[:, cs])
        term = g * jnp.dot(o_ref[...], w_ref[...], preferred_element_type=F32)
        merged = term if merged is None else merged + term
    y = jnp.dot(merged.astype(MXU_DTYPE), wo_ref[...], preferred_element_type=F32)
    x1 = _layer_norm(DEEPNORM_ALPHA * x + (1.0 + gate1) * y, lng_ref[...], lnb_ref[...])
    x1_ref[...] = x1
    h2 = x1 * (1.0 + scale2) + shift2
    h2_ref[...] = _pack_halves(h2)
    logit_ref[...] = _nt_dot(wr_ref[...], h2.astype(MXU_DTYPE)) + br_ref[...]


def _merge(x2d, ada_l, o_a, o_b, o_c, w_gate, b_gate, w_pa, w_pb, w_pc, w_o, ln_g, ln_b,
           w_router_t, b_router, seq):
    t, d = x2d.shape
    tm = min(ROW_TILE, seq)
    tiles_per_seq = seq // tm
    row = lambda i: (i, 0)
    whole = lambda a: pl.BlockSpec(a.shape, lambda i: (0,) * a.ndim)
    return pl.pallas_call(
        _merge_kernel,
        grid=(t // tm,),
        in_specs=[
            pl.BlockSpec((tm, d), row),
            pl.BlockSpec((None, 6, d), lambda i: (i // tiles_per_seq, 0, 0)),
            pl.BlockSpec((tm, ATT_WIDTH), row), pl.BlockSpec((tm, ATT_WIDTH), row),
            pl.BlockSpec((tm, LRU_WIDTH), row),
            whole(w_gate), whole(b_gate), whole(w_pa), whole(w_pb), whole(w_pc), whole(w_o),
            whole(ln_g), whole(ln_b), whole(w_router_t), whole(b_router),
        ],
        out_specs=[pl.BlockSpec((tm, d), row), pl.BlockSpec((tm, d // 2), row),
                   pl.BlockSpec((N_EXPERTS, tm), lambda i: (0, i))],
        out_shape=[jax.ShapeDtypeStruct((t, d), F32),
                   jax.ShapeDtypeStruct((t, d // 2), jnp.uint32),
                   jax.ShapeDtypeStruct((N_EXPERTS, t), F32)],
        compiler_params=_params(1),
        name="merge_outproj_ln",
    )(x2d, ada_l, o_a, o_b, o_c, w_gate, b_gate, w_pa, w_pb, w_pc, w_o, ln_g, ln_b,
      w_router_t, b_router)


def _route_kernel(logit_ref, tri_ref, eid_ref, gate_ref, pos_ref, cnt_ref, carry_ref):
    i = pl.program_id(0)
    tr = logit_ref.shape[1]

    @pl.when(i == 0)
    def _():
        carry_ref[...] = jnp.zeros_like(carry_ref)

    erow = lax.broadcasted_iota(jnp.int32, (N_EXPERTS, tr), 0)
    cur = logit_ref[...]
    vals, ids = [], []
    for _ in range(TOP_K):
        m = jnp.max(cur, axis=0, keepdims=True)
        idx = jnp.min(jnp.where(cur == m, erow, N_EXPERTS), axis=0, keepdims=True)
        vals.append(m)
        ids.append(idx)
        cur = jnp.where(erow == idx, -jnp.inf, cur)
    exps = [jnp.exp(v - vals[0]) for v in vals]
    denom = exps[0] + exps[1] + exps[2] + exps[3]
    chosen = jnp.zeros((N_EXPERTS, tr), F32)
    for idx in ids:
        chosen = chosen + jnp.where(erow == idx, 1.0, 0.0)
    prefix = (jnp.dot(chosen.astype(MXU_DTYPE), tri_ref[...], preferred_element_type=F32)
              + carry_ref[:, 0:1])
    for k in range(TOP_K):
        eid_ref[k:k + 1, :] = ids[k]
        gate_ref[k:k + 1, :] = exps[k] / denom
        pos_ref[k:k + 1, :] = jnp.sum(jnp.where(erow == ids[k], prefix, 0.0), axis=0,
                                      keepdims=True).astype(jnp.int32)
    carry_ref[...] = carry_ref[...] + jnp.sum(chosen, axis=1, keepdims=True)
    cnt_ref[...] = carry_ref[...].astype(jnp.int32)


def _route(logits_t):
    n_e, t = logits_t.shape
    tr = min(ROUTE_TILE, t)
    tri = (lax.broadcasted_iota(jnp.int32, (tr, tr), 0)
           < lax.broadcasted_iota(jnp.int32, (tr, tr), 1)).astype(MXU_DTYPE)
    tok = lambda i: (0, i)
    return pl.pallas_call(
        _route_kernel,
        grid=(t // tr,),
        in_specs=[pl.BlockSpec((n_e, tr), tok), pl.BlockSpec((tr, tr), lambda i: (0, 0))],
        out_specs=[pl.BlockSpec((TOP_K, tr), tok)] * 3
        + [pl.BlockSpec((n_e, LANES), lambda i: (0, 0))],
        out_shape=[jax.ShapeDtypeStruct((TOP_K, t), jnp.int32),
                   jax.ShapeDtypeStruct((TOP_K, t), F32),
                   jax.ShapeDtypeStruct((TOP_K, t), jnp.int32),
                   jax.ShapeDtypeStruct((n_e, LANES), jnp.int32)],
        scratch_shapes=[pltpu.VMEM((n_e, LANES), F32)],
        compiler_params=_params(1),
        name="route_topk",
    )(logits_t, tri)


def _dest_kernel(start_ref, eid_ref, pos_ref, dest_ref):
    eid = eid_ref[...]
    dest = pos_ref[...]
    for e in range(N_EXPERTS):
        dest = jnp.where(eid == e, dest + start_ref[e], dest)
    dest_ref[...] = dest


def _dest(pad_start, eid, pos):
    k, t = eid.shape
    tr = min(ROUTE_TILE, t)
    spec = pl.BlockSpec((k, tr), lambda i, s: (0, i))
    return pl.pallas_call(
        _dest_kernel,
        grid_spec=pltpu.PrefetchScalarGridSpec(
            num_scalar_prefetch=1, grid=(t // tr,), in_specs=[spec, spec], out_specs=spec),
        out_shape=jax.ShapeDtypeStruct((k, t), jnp.int32),
        compiler_params=_params(1),
        name="route_dest",
    )(pad_start, eid, pos)


def _ffn_block(x_ref, wgu_lp, bgu_ref, wd_lp, bd_ref, y_ref):
    d_e = wd_lp.shape[0]
    lo, hi = _unpack_halves(x_ref[...])
    x = jnp.concatenate([lo, hi], axis=1).astype(MXU_DTYPE)
    gu = jnp.dot(x, wgu_lp[...], preferred_element_type=F32) + bgu_ref[...]
    gate = jnp.minimum(gu[:, :d_e], SWIGLU_LIMIT)
    up = jnp.clip(gu[:, d_e:], -SWIGLU_LIMIT, SWIGLU_LIMIT)
    act = (up + 1.0) * (gate * jax.nn.sigmoid(SWIGLU_ALPHA * gate))
    y = jnp.dot(act.astype(MXU_DTYPE), wd_lp[...], preferred_element_type=F32) + bd_ref[...]
    y_ref[...] = _pack_halves(y)


def _expert_kernel(pe_ref, tok_first, tok_b, tok_a_next, dst_b_prev, dst_a, dst_b, h_hbm,
                   wgu_ref, bgu_ref, wd_ref, bd_ref, y_hbm,
                   xa, xb, ya, yb, wgu_lp, wd_lp, sem_ga, sem_gb, sem_sa, sem_sb):
    j = pl.program_id(0)
    rows = xa.shape[0]

    def gather(tok_ref, x_ref, sem):
        for r in range(rows):
            pltpu.make_async_copy(h_hbm.at[pl.ds(tok_ref[r], 1), :],
                                  x_ref.at[pl.ds(r, 1), :], sem).start(priority=r % 2)

    def scatter(dst_ref, yv_ref, sem):
        for r in range(rows):
            pltpu.make_async_copy(yv_ref.at[pl.ds(r, 1), :],
                                  y_hbm.at[pl.ds(dst_ref[r], 1), :], sem).start(priority=r % 2)

    def wait_rows(buf_ref, sem):
        pltpu.make_async_copy(h_hbm.at[pl.ds(0, rows), :], buf_ref, sem).wait()

    @pl.when(j == 0)
    def _():
        gather(tok_first, xa, sem_ga)
        yb[...] = jnp.zeros_like(yb)

    @pl.when((j == 0) | (pe_ref[j] != pe_ref[jnp.maximum(j - 1, 0)]))
    def _():
        wgu_lp[...] = wgu_ref[...].astype(MXU_DTYPE)
        wd_lp[...] = wd_ref[...].astype(MXU_DTYPE)

    wait_rows(xa, sem_ga)
    gather(tok_b, xb, sem_gb)
    scatter(dst_b_prev, yb, sem_sb)
    _ffn_block(xa, wgu_lp, bgu_ref, wd_lp, bd_ref, ya)
    wait_rows(xb, sem_gb)
    wait_rows(yb, sem_sb)
    gather(tok_a_next, xa, sem_ga)
    scatter(dst_a, ya, sem_sa)
    _ffn_block(xb, wgu_lp, bgu_ref, wd_lp, bd_ref, yb)
    wait_rows(ya, sem_sa)

    @pl.when(j == pl.num_programs(0) - 1)
    def _():
        wait_rows(xa, sem_ga)
        scatter(dst_b, yb, sem_sb)
        wait_rows(yb, sem_sb)


def _experts(layer, pair_e, tok_ext, dst_ext, h2p, n_out_rows, w_gu, b_gu, w_down, b_down):
    half = h2p.shape[1]
    _, n_e, d, d2 = w_gu.shape
    d_e = w_down.shape[2]
    n_layers = w_gu.shape[0]
    n_pairs = pair_e.shape[0]
    blk = EXPERT_BLOCK
    smem = lambda fn: pl.BlockSpec((blk,), fn, memory_space=pltpu.SMEM)
    return pl.pallas_call(
        _expert_kernel,
        grid_spec=pltpu.PrefetchScalarGridSpec(
            num_scalar_prefetch=1,
            grid=(n_pairs,),
            in_specs=[
                smem(lambda j, pe: (0,)),
                smem(lambda j, pe: (2 * j + 1,)),
                smem(lambda j, pe: (2 * j + 2,)),
                smem(lambda j, pe: (2 * j,)),
                smem(lambda j, pe: (2 * j + 1,)),
                smem(lambda j, pe: (2 * j + 2,)),
                pl.BlockSpec(memory_space=pl.ANY),
                pl.BlockSpec((None, None, d, d2), lambda j, pe: (layer, pe[j], 0, 0)),
                pl.BlockSpec((None, None, 1, d2), lambda j, pe: (layer, pe[j], 0, 0)),
                pl.BlockSpec((None, None, d_e, d), lambda j, pe: (layer, pe[j], 0, 0)),
                pl.BlockSpec((None, None, 1, d), lambda j, pe: (layer, pe[j], 0, 0)),
            ],
            out_specs=pl.BlockSpec(memory_space=pl.ANY),
            scratch_shapes=[pltpu.VMEM((blk, half), jnp.uint32)] * 4
            + [pltpu.VMEM((d, d2), MXU_DTYPE), pltpu.VMEM((d_e, d), MXU_DTYPE)]
            + [pltpu.SemaphoreType.DMA(())] * 4,
        ),
        out_shape=jax.ShapeDtypeStruct((n_out_rows, half), jnp.uint32),
        compiler_params=_params(1),
        name="moe_experts",
    )(pair_e, tok_ext, tok_ext, tok_ext, dst_ext, dst_ext, dst_ext, h2p,
      w_gu, b_gu.reshape(n_layers, n_e, 1, d2), w_down, b_down.reshape(n_layers, n_e, 1, d))


def _combine_kernel(x1_ref, ada_ref, gate_ref, lng_ref, lnb_ref, *rest):
    y_refs, o_ref = rest[:TOP_K], rest[TOP_K]
    gates = gate_ref[...]
    y_lo = y_hi = None
    for k in range(TOP_K):
        lo, hi = _unpack_halves(y_refs[k][...])
        g = gates[:, k:k + 1]
        y_lo = lo * g if y_lo is None else y_lo + lo * g
        y_hi = hi * g if y_hi is None else y_hi + hi * g
    y = jnp.concatenate([y_lo, y_hi], axis=1)
    gate2 = ada_ref[5:6, :]
    o_ref[...] = _layer_norm(DEEPNORM_ALPHA * x1_ref[...] + (1.0 + gate2) * y,
                             lng_ref[...], lnb_ref[...])


def _combine(x1, ada_l, gates_tk, ln_g, ln_b, y_rows, seq):
    t, d = x1.shape
    tm = min(MOVE_TILE, seq)
    tiles_per_seq = seq // tm
    return pl.pallas_call(
        _combine_kernel,
        grid=(t // tm,),
        in_specs=[
            pl.BlockSpec((tm, d), lambda i: (i, 0)),
            pl.BlockSpec((None, 6, d), lambda i: (i // tiles_per_seq, 0, 0)),
            pl.BlockSpec((tm, TOP_K), lambda i: (i, 0)),
            pl.BlockSpec((1, d), lambda i: (0, 0)),
            pl.BlockSpec((1, d), lambda i: (0, 0)),
        ] + [
            pl.BlockSpec((tm, d // 2), lambda i, k=k: (k * (t // tm) + i, 0))
            for k in range(TOP_K)
        ],
        out_specs=pl.BlockSpec((tm, d), lambda i: (i, 0)),
        out_shape=jax.ShapeDtypeStruct((t, d), F32),
        compiler_params=_params(1),
        name="moe_combine_ln",
    )(x1, ada_l, gates_tk, ln_g, ln_b, *([y_rows] * TOP_K))


def _split_w_in(w_in):
    d = w_in.shape[0]
    sizes = (ATT_WIDTH, ATT_WIDTH, ATT_WIDTH, N_HEADS, ATT_WIDTH, ATT_WIDTH, ATT_WIDTH,
             LRU_WIDTH, LRU_WIDTH)
    bounds = [sum(sizes[:n]) for n in range(len(sizes) + 1)]
    qa, ka, va, fa, qb, kb, vb, xc, gc = (
        w_in[:, bounds[n]:bounds[n + 1]] for n in range(len(sizes)))
    forget = jnp.concatenate([fa, jnp.zeros((d, LANES - N_HEADS), w_in.dtype)], axis=1)
    w_main = jnp.concatenate([qa, ka, qb, kb, xc, gc, forget], axis=1).astype(MXU_DTYPE)
    w_vt = jnp.concatenate([va, vb], axis=1).T.astype(MXU_DTYPE)
    return w_main, w_vt


def _block_diag(w):
    n, c, dd = w.shape
    eye = jnp.eye(n, dtype=w.dtype)
    return (eye[:, None, :, None] * w[:, :, None, :]).reshape(n * c, n * dd).astype(MXU_DTYPE)


def kernel(x, c, w_ada, b_ada, ln1_g, ln1_b, w_in, b_f, conv_w, conv_b, lru_wa, lru_ba, lru_wx,
           lru_bx, lru_lambda, w_gate, b_gate, w_pa, w_pb, w_pc, w_o, ln2_g, ln2_b, w_router,
           b_router, w_gu, b_gu, w_down, b_down):
    n_batch, seq, d = x.shape
    t = n_batch * seq
    n_layers = w_ada.shape[0]
    n_pairs = (t * TOP_K) // PAIR_ROWS + N_EXPERTS
    n_slots = n_pairs * PAIR_ROWS
    att_tile = min(ATT_TILE, seq)
    vec = lambda a: a.reshape(1, -1)

    ada = _ada(c, w_ada, b_ada).reshape(n_layers, n_batch, 6, d)
    x2d = x.reshape(t, d)
    for l in range(n_layers):
        ada_l = ada[l]
        bf_pad = jnp.concatenate([b_f[l], jnp.zeros((LANES - N_HEADS,), F32)]).reshape(1, LANES)
        w_main, w_vt = _split_w_in(w_in[l])
        qa, ka, qb, kb, xc, gc, vat, vbt, fcol, frow = _inproj(
            x2d, ada_l, w_main, w_vt, bf_pad, seq)
        frow4 = frow.reshape(n_batch, N_HEADS, seq // att_tile, att_tile)
        o_a = _fox(qa, ka, vat, fcol, frow4, seq)
        o_b = _sb(qb, kb, vbt, seq)
        o_c = _lru(xc, gc, conv_w[l], vec(conv_b[l]), _block_diag(lru_wa[l]), vec(lru_ba[l]),
                   _block_diag(lru_wx[l]), vec(lru_bx[l]), vec(lru_lambda[l]), seq)
        x1, h2, logits_t = _merge(
            x2d, ada_l, o_a, o_b, o_c, w_gate[l].astype(MXU_DTYPE), vec(b_gate[l]),
            w_pa[l].astype(MXU_DTYPE), w_pb[l].astype(MXU_DTYPE), w_pc[l].astype(MXU_DTYPE),
            w_o[l].astype(MXU_DTYPE), vec(ln1_g[l]), vec(ln1_b[l]),
            w_router[l].T.astype(MXU_DTYPE), b_router[l].reshape(N_EXPERTS, 1), seq)

        eid, gates, pos, cnt = _route(logits_t)
        counts = cnt[:, 0]
        padded = (counts + PAIR_ROWS - 1) // PAIR_ROWS * PAIR_ROWS
        pad_end = jnp.cumsum(padded)
        pad_start = (pad_end - padded).astype(jnp.int32)
        pair_first = (jnp.arange(n_pairs) * PAIR_ROWS)[:, None]
        pair_e = jnp.minimum(jnp.sum(pad_end[None, :] <= pair_first, axis=1),
                             N_EXPERTS - 1).astype(jnp.int32)
        dest = _dest(pad_start, eid, pos).T.reshape(-1)
        flat = jnp.arange(t * TOP_K, dtype=jnp.int32)
        inv = jnp.full((n_slots,), -1, jnp.int32).at[dest].set(flat, unique_indices=True)
        is_pad = inv < 0
        spare = t * TOP_K + jnp.cumsum(is_pad.astype(jnp.int32)) - 1
        dst = jnp.where(is_pad, spare, (inv % TOP_K) * t + inv // TOP_K)
        tok = jnp.where(is_pad, 0, inv // TOP_K)
        tok_ext = jnp.concatenate([tok, jnp.zeros((EXPERT_BLOCK,), jnp.int32)])
        dst_ext = jnp.concatenate([n_slots + jnp.arange(EXPERT_BLOCK, dtype=jnp.int32), dst])

        y_rows = _experts(l, pair_e, tok_ext, dst_ext, h2, n_slots + EXPERT_BLOCK,
                          w_gu, b_gu, w_down, b_down)
        x2d = _combine(x1, ada_l, gates.T, vec(ln2_g[l]), vec(ln2_b[l]), y_rows, seq)
    return x2d.reshape(n_batch, seq, d)
```

```python
import functools

import jax
import jax.numpy as jnp
from jax import lax
from jax.experimental import pallas as pl
from jax.experimental.pallas import tpu as pltpu
from jax.experimental.pallas import tpu_sc as plsc

D_MODEL = 1024
DEPTH = 2
HEAD_DIM = 64
N_HEADS = 8
ATT_WIDTH = N_HEADS * HEAD_DIM
LRU_WIDTH = D_MODEL
LRU_BLOCKS = 16
CONV_WIDTH = 4
LRU_C = 8.0
N_EXPERTS = 32
TOP_K = 4
SWIGLU_LIMIT = 7.0
SWIGLU_ALPHA = 1.702
LN_EPS = 1e-5
DEEPNORM_ALPHA = (2.0 * DEPTH) ** 0.25
ATT_SCALE = HEAD_DIM ** -0.5
LOG2_E = 1.4426950408889634

LANES = 128
MXU_DTYPE = jnp.bfloat16
F32 = jnp.float32

ROW_TILE = 512
ATT_TILE = 256
ATT_GROUP = 8
LRU_TILE = 256
ROUTE_TILE = 512
EXPERT_BLOCK = 256
MOVE_TILE = 256
SC_CORES = 2
SC_SUBCORES = 16
SC_CHUNK = 64
VMEM_LIMIT = 56 * 1024 * 1024


def _params(n_axes, vmem=VMEM_LIMIT):
    return pltpu.CompilerParams(
        dimension_semantics=("arbitrary",) * n_axes, vmem_limit_bytes=vmem)


def _log_sigmoid(x):
    return jnp.minimum(x, 0.0) - jnp.log1p(jnp.exp(-jnp.abs(x)))


def _softplus(x):
    return jnp.maximum(x, 0.0) + jnp.log1p(jnp.exp(-jnp.abs(x)))


def _layer_norm(v, g, b):
    mu = jnp.mean(v, axis=-1, keepdims=True)
    d = v - mu
    var = jnp.mean(d * d, axis=-1, keepdims=True)
    return d * lax.rsqrt(var + LN_EPS) * g + b


def _pack_halves(x):
    n = x.shape[1] // 2
    bits = lambda v: lax.bitcast_convert_type(v.astype(jnp.bfloat16).astype(F32), jnp.uint32)
    return (bits(x[:, :n]) >> 16) | bits(x[:, n:])


def _unpack_halves(p):
    lo = lax.bitcast_convert_type(p << 16, F32)
    hi = lax.bitcast_convert_type(p & jnp.uint32(0xFFFF0000), F32)
    return lo, hi


def _split3(x):
    hi = x.astype(MXU_DTYPE)
    r1 = x - hi.astype(F32)
    mid = r1.astype(MXU_DTYPE)
    lo = (r1 - mid.astype(F32)).astype(MXU_DTYPE)
    return hi, mid, lo


def _ada_kernel(c_ref, w_ref, b_ref, o_ref):
    c = c_ref[...]
    cond = c * jax.nn.sigmoid(c)
    o_ref[...] = jnp.dot(cond.astype(MXU_DTYPE), w_ref[...].astype(MXU_DTYPE),
                         preferred_element_type=F32) + b_ref[...]


def _ada(c, w_ada, b_ada):
    n_layers, d, n6 = w_ada.shape
    b = c.shape[0]
    tn = n6 // 4
    return pl.pallas_call(
        _ada_kernel,
        grid=(n_layers, n6 // tn),
        in_specs=[
            pl.BlockSpec((b, d), lambda l, j: (0, 0)),
            pl.BlockSpec((None, d, tn), lambda l, j: (l, 0, j)),
            pl.BlockSpec((None, 1, tn), lambda l, j: (l, 0, j)),
        ],
        out_specs=pl.BlockSpec((None, b, tn), lambda l, j: (l, 0, j)),
        out_shape=jax.ShapeDtypeStruct((n_layers, b, n6), F32),
        compiler_params=_params(2),
        name="ada",
    )(c, w_ada, b_ada.reshape(n_layers, 1, n6))


def _nt_dot(a, b):
    return lax.dot_general(a, b, (((1,), (1,)), ((), ())), preferred_element_type=F32)


def _inproj_kernel(x_ref, ada_ref, w_ref, wvt_ref, bf_ref, tri_ref,
                   qa_ref, ka_ref, qb_ref, kb_ref, xc_ref, gc_ref, vat_ref, vbt_ref,
                   fcol_ref, frow_ref, carry_ref, *, tiles_per_seq):
    i = pl.program_id(0)
    tm = x_ref.shape[0]
    shift = ada_ref[0:1, :]
    scale = ada_ref[1:2, :]
    h = (x_ref[...] * (1.0 + scale) + shift).astype(MXU_DTYPE)
    off = 0
    for ref in (qa_ref, ka_ref, qb_ref, kb_ref, xc_ref, gc_ref):
        width = ref.shape[1]
        ref[...] = jnp.dot(h, w_ref[:, off:off + width],
                           preferred_element_type=F32).astype(ref.dtype)
        off += width
    vt = _nt_dot(wvt_ref[...], h).astype(MXU_DTYPE)
    chunk = vat_ref.shape[2]
    for n in range(tm // chunk):
        cols = slice(n * chunk, (n + 1) * chunk)
        vat_ref[n] = vt[0:ATT_WIDTH, cols]
        vbt_ref[n] = vt[ATT_WIDTH:2 * ATT_WIDTH, cols]
    zf = jnp.dot(h, w_ref[:, off:off + LANES], preferred_element_type=F32) + bf_ref[...]
    logf = _log_sigmoid(zf)

    @pl.when(i % tiles_per_seq == 0)
    def _():
        carry_ref[...] = jnp.zeros_like(carry_ref)

    tri = tri_ref[...]
    f_cum = carry_ref[0:1, :]
    for piece in _split3(logf):
        f_cum = f_cum + jnp.dot(tri, piece, preferred_element_type=F32)
    carry_ref[0:1, :] = f_cum[tm - 1:tm, :]
    fcol_ref[...] = f_cum
    frow_ref[...] = f_cum.T[0:N_HEADS, :]


def _inproj(x2d, ada_l, w_main, w_vt, bf_pad, seq):
    t, d = x2d.shape
    tm = min(ROW_TILE, seq)
    chunk = min(ATT_TILE, seq)
    tiles_per_seq = seq // tm
    n_batch = t // seq
    tri = (lax.broadcasted_iota(jnp.int32, (tm, tm), 1)
           <= lax.broadcasted_iota(jnp.int32, (tm, tm), 0)).astype(MXU_DTYPE)
    row = lambda i: (i, 0)
    att = jax.ShapeDtypeStruct((t, ATT_WIDTH), MXU_DTYPE)
    att_t = jax.ShapeDtypeStruct((t // chunk, ATT_WIDTH, chunk), MXU_DTYPE)
    wide = jax.ShapeDtypeStruct((t, LRU_WIDTH), F32)
    att_spec = pl.BlockSpec((tm, ATT_WIDTH), row)
    att_t_spec = pl.BlockSpec((tm // chunk, ATT_WIDTH, chunk), lambda i: (i, 0, 0))
    wide_spec = pl.BlockSpec((tm, LRU_WIDTH), row)
    return pl.pallas_call(
        functools.partial(_inproj_kernel, tiles_per_seq=tiles_per_seq),
        grid=(t // tm,),
        in_specs=[
            pl.BlockSpec((tm, d), row),
            pl.BlockSpec((None, 6, d), lambda i: (i // tiles_per_seq, 0, 0)),
            pl.BlockSpec(w_main.shape, lambda i: (0, 0)),
            pl.BlockSpec(w_vt.shape, lambda i: (0, 0)),
            pl.BlockSpec((1, LANES), lambda i: (0, 0)),
            pl.BlockSpec((tm, tm), lambda i: (0, 0)),
        ],
        out_specs=[att_spec] * 4 + [wide_spec] * 2 + [att_t_spec] * 2 + [
            pl.BlockSpec((tm, LANES), row),
            pl.BlockSpec((None, N_HEADS, tm),
                         lambda i: (i // tiles_per_seq, 0, i % tiles_per_seq)),
        ],
        out_shape=[att] * 4 + [wide] * 2 + [att_t] * 2 + [
            jax.ShapeDtypeStruct((t, LANES), F32),
            jax.ShapeDtypeStruct((n_batch, N_HEADS, seq), F32),
        ],
        scratch_shapes=[pltpu.VMEM((8, LANES), F32)],
        compiler_params=_params(1),
        name="inproj",
    )(x2d, ada_l, w_main, w_vt, bf_pad, tri)


def _head_query(q2, c):
    lane = lax.broadcasted_iota(jnp.int32, (1, LANES), 1)
    in_head = (lane >= c * HEAD_DIM) & (lane < (c + 1) * HEAD_DIM)
    return jnp.where(in_head, q2, jnp.zeros_like(q2)) * ATT_SCALE


def _pair_cols(head):
    return slice((head // 2) * LANES, (head // 2 + 1) * LANES)


def _store_heads(o_ref, heads, outs_t):
    sub = lax.broadcasted_iota(jnp.int32, (LANES, 1), 0)
    for n in range(0, len(heads), 2):
        pair_t = jnp.where(sub < HEAD_DIM, outs_t[n], outs_t[n + 1])
        o_ref[:, _pair_cols(heads[n])] = pair_t.T.astype(o_ref.dtype)


def _fox_kernel(q_ref, k_ref, vt_ref, fcol_ref, frow_ref, o_ref):
    i = pl.program_id(1)
    tq = q_ref.shape[0]
    key = lax.broadcasted_iota(jnp.int32, (tq, tq), 0)
    qry = lax.broadcasted_iota(jnp.int32, (tq, tq), 1)
    causal = key <= qry
    for g in range(N_HEADS // ATT_GROUP):
        heads = list(range(g * ATT_GROUP, (g + 1) * ATT_GROUP))
        qms = [_head_query(q_ref[:, _pair_cols(hd)], hd % 2) for hd in heads]
        fqs = [frow_ref[hd, pl.ds(i, 1), :] for hd in heads]

        def qk(j, heads=heads, qms=qms):
            start = pl.multiple_of(j * tq, tq)
            return tuple(_nt_dot(k_ref[pl.ds(start, tq), _pair_cols(hd)], qms[n])
                         for n, hd in enumerate(heads))

        def block(j, scores, carry, masked, heads=heads, fqs=fqs):
            start = pl.multiple_of(j * tq, tq)
            stats, probs = [], []
            for n, hd in enumerate(heads):
                m, l, _ = carry[n]
                fk = fcol_ref[pl.ds(start, tq), hd:hd + 1]
                s = scores[n] - fk
                if masked:
                    s = jnp.where(causal, s, -jnp.inf)
                m_new = jnp.maximum(m, jnp.max(s, axis=0, keepdims=True) + fqs[n])
                alpha = jnp.exp(m - m_new)
                pm = jnp.exp(s + (fqs[n] - m_new))
                l = alpha * l + jnp.sum(pm, axis=0, keepdims=True)
                stats.append((m_new, l, alpha))
                probs.append(pm.astype(MXU_DTYPE))
            pvs = [jnp.dot(vt_ref[j, _pair_cols(hd), :], probs[n],
                           preferred_element_type=F32) for n, hd in enumerate(heads)]
            return tuple((stats[n][0], stats[n][1], stats[n][2] * carry[n][2] + pvs[n])
                         for n in range(len(heads)))

        init = tuple((jnp.full((1, tq), -jnp.inf, F32), jnp.zeros((1, tq), F32),
                      jnp.zeros((LANES, tq), F32)) for _ in heads)
        carry = lax.fori_loop(
            0, i, lambda j, cr, qk=qk, block=block: block(j, qk(j), cr, False), init)
        carry = block(i, qk(i), carry, True)
        _store_heads(o_ref, heads, [acc / l for _, l, acc in carry])


def _att_specs(t, seq):
    n_batch = t // seq
    tq = min(ATT_TILE, seq)
    nq = seq // tq
    q_spec = pl.BlockSpec((tq, ATT_WIDTH), lambda b, i: (b * nq + i, 0))
    k_spec = pl.BlockSpec((seq, ATT_WIDTH), lambda b, i: (b, 0))
    vt_spec = pl.BlockSpec((nq, ATT_WIDTH, tq), lambda b, i: (b, 0, 0))
    return n_batch, tq, nq, q_spec, k_spec, vt_spec


def _fox(q, k, vt, fcol, frow4, seq):
    t = q.shape[0]
    n_batch, tq, nq, q_spec, k_spec, vt_spec = _att_specs(t, seq)
    return pl.pallas_call(
        _fox_kernel,
        grid=(n_batch, nq),
        in_specs=[
            q_spec, k_spec, vt_spec,
            pl.BlockSpec((seq, LANES), lambda b, i: (b, 0)),
            pl.BlockSpec((None, N_HEADS, nq, tq), lambda b, i: (b, 0, 0, 0)),
        ],
        out_specs=q_spec,
        out_shape=jax.ShapeDtypeStruct((t, ATT_WIDTH), MXU_DTYPE),
        compiler_params=_params(2),
        name="fox_attention",
    )(q, k, vt, fcol, frow4)


def _sb_kernel(q_ref, k_ref, vt_ref, suf_ref, o_ref):
    i = pl.program_id(1)
    tq = q_ref.shape[0]
    key = lax.broadcasted_iota(jnp.int32, (tq, tq), 0)
    qry = lax.broadcasted_iota(jnp.int32, (tq, tq), 1)
    strict = key < qry
    suf = suf_ref[...]
    for g in range(N_HEADS // ATT_GROUP):
        heads = list(range(g * ATT_GROUP, (g + 1) * ATT_GROUP))
        qms = [_head_query(q_ref[:, _pair_cols(hd)], hd % 2) for hd in heads]

        def qk(j, heads=heads, qms=qms):
            start = pl.multiple_of(j * tq, tq)
            return tuple(_nt_dot(k_ref[pl.ds(start, tq), _pair_cols(hd)], qms[n])
                         for n, hd in enumerate(heads))

        def block(j, zs, carry, masked, heads=heads):
            log_betas, splits, laters = [], [], []
            for n in range(len(heads)):
                z = zs[n]
                sp = jnp.maximum(z, 0.0) + jnp.log(1.0 + jnp.exp2(jnp.abs(z) * -LOG2_E))
                log_betas.append(z - sp)
                if masked:
                    sp = jnp.where(strict, sp, 0.0)
                splits.append(sp.astype(MXU_DTYPE))
                laters.append(carry[n][0] + jnp.sum(sp, axis=0, keepdims=True))
            afters = [jnp.dot(suf, splits[n], preferred_element_type=F32)
                      for n in range(len(heads))]
            ws = []
            for n in range(len(heads)):
                w = jnp.exp(log_betas[n] - jnp.maximum(afters[n] + carry[n][0], 0.0))
                if masked:
                    w = jnp.where(strict, w, 0.0)
                ws.append(w.astype(MXU_DTYPE))
            return tuple(
                (laters[n], carry[n][1] + jnp.dot(vt_ref[j, _pair_cols(hd), :], ws[n],
                                                  preferred_element_type=F32))
                for n, hd in enumerate(heads))

        init = tuple((jnp.zeros((1, tq), F32), jnp.zeros((LANES, tq), F32)) for _ in heads)

        def step(s, state, qk=qk, block=block):
            zs, carry = state
            j = i - 1 - s
            nxt = qk(jnp.maximum(j - 1, 0))
            return nxt, block(j, zs, carry, False)

        diag = qk(i)
        first = qk(jnp.maximum(i - 1, 0))
        carry = block(i, diag, init, True)
        _, carry = lax.fori_loop(0, i, step, (first, carry))
        _store_heads(o_ref, heads, [acc for _, acc in carry])


def _sb(q, k, vt, seq):
    t = q.shape[0]
    n_batch, tq, nq, q_spec, k_spec, vt_spec = _att_specs(t, seq)
    suf = (lax.broadcasted_iota(jnp.int32, (tq, tq), 1)
           > lax.broadcasted_iota(jnp.int32, (tq, tq), 0)).astype(MXU_DTYPE)
    return pl.pallas_call(
        _sb_kernel,
        grid=(n_batch, nq),
        in_specs=[q_spec, k_spec, vt_spec, pl.BlockSpec((tq, tq), lambda b, i: (0, 0))],
        out_specs=q_spec,
        out_shape=jax.ShapeDtypeStruct((t, ATT_WIDTH), MXU_DTYPE),
        compiler_params=_params(2),
        name="sb_attention",
    )(q, k, vt, suf)


def _lru_kernel(xc_ref, gc_ref, cw_ref, cb_ref, wa_ref, ba_ref, wx_ref, bx_ref, lam_ref,
                o_ref, ext_ref, a_ref, u_ref, h_ref, state_ref):
    s_idx = pl.program_id(1)
    ts = xc_ref.shape[0]
    halo = 8

    @pl.when(s_idx == 0)
    def _():
        ext_ref[0:halo, :] = jnp.zeros((halo, LRU_WIDTH), F32)
        state_ref[...] = jnp.zeros_like(state_ref)

    ext_ref[halo:halo + ts, :] = xc_ref[...]
    xconv = cb_ref[...]
    for j in range(CONV_WIDTH):
        lo = halo - (CONV_WIDTH - 1) + j
        xconv = xconv + cw_ref[j:j + 1, :] * ext_ref[lo:lo + ts, :]
    ext_ref[0:halo, :] = ext_ref[ts:ts + halo, :]

    xb = xconv.astype(MXU_DTYPE)
    r = jax.nn.sigmoid(jnp.dot(xb, wa_ref[...], preferred_element_type=F32) + ba_ref[...])
    ig = jax.nn.sigmoid(jnp.dot(xb, wx_ref[...], preferred_element_type=F32) + bx_ref[...])
    log_a = (-LRU_C * _softplus(-lam_ref[...])) * r
    a = jnp.exp(log_a)
    a_ref[...] = a
    u_ref[...] = jnp.sqrt(1.0 - a * a) * (ig * xconv)

    def step(t, h):
        h = a_ref[pl.ds(t, 1), :] * h + u_ref[pl.ds(t, 1), :]
        h_ref[pl.ds(t, 1), :] = h
        return h

    state_ref[0:1, :] = lax.fori_loop(0, ts, step, state_ref[0:1, :], unroll=8)
    o_ref[...] = (h_ref[...] * jax.nn.gelu(gc_ref[...])).astype(o_ref.dtype)


def _lru(xc, gc, conv_w, conv_b, wa_bd, ba, wx_bd, bx, lam, seq):
    t, w = xc.shape
    n_batch = t // seq
    ts = min(LRU_TILE, seq)
    ns = seq // ts
    row_spec = pl.BlockSpec((ts, w), lambda b, s: (b * ns + s, 0))
    vec_spec = pl.BlockSpec((1, w), lambda b, s: (0, 0))
    mat_spec = pl.BlockSpec((w, w), lambda b, s: (0, 0))
    return pl.pallas_call(
        _lru_kernel,
        grid=(n_batch, ns),
        in_specs=[row_spec, row_spec,
                  pl.BlockSpec((CONV_WIDTH, w), lambda b, s: (0, 0)), vec_spec,
                  mat_spec, vec_spec, mat_spec, vec_spec, vec_spec],
        out_specs=row_spec,
        out_shape=jax.ShapeDtypeStruct((t, w), MXU_DTYPE),
        scratch_shapes=[pltpu.VMEM((ts + 8, w), F32), pltpu.VMEM((ts, w), F32),
                        pltpu.VMEM((ts, w), F32), pltpu.VMEM((ts, w), F32),
                        pltpu.VMEM((8, w), F32)],
        compiler_params=_params(2),
        name="rg_lru",
    )(xc, gc, conv_w, conv_b, wa_bd, ba, wx_bd, bx, lam)


def _merge_kernel(x_ref, ada_ref, oa_ref, ob_ref, oc_ref, wg_ref, bg_ref, wpa_ref, wpb_ref,
                  wpc_ref, wo_ref, lng_ref, lnb_ref, wr_ref, br_ref,
                  x1_ref, h2_ref, logit_ref):
    d = x_ref.shape[1]
    x = x_ref[...]
    shift1, scale1, gate1 = ada_ref[0:1, :], ada_ref[1:2, :], ada_ref[2:3, :]
    shift2, scale2 = ada_ref[3:4, :], ada_ref[4:5, :]
    h = (x * (1.0 + scale1) + shift1).astype(MXU_DTYPE)
    merged = None
    for n, (o_ref, w_ref) in enumerate(((oa_ref, wpa_ref), (ob_ref, wpb_ref), (oc_ref, wpc_ref))):
        cs = slice(n * d, (n + 1) * d)
        g = jax.nn.sigmoid(jnp.dot(h, wg_ref[:, cs], preferred_element_type=F32) + bg_ref[:, cs])
        term = g * jnp.dot(o_ref[...], w_ref[...], preferred_element_type=F32)
        merged = term if merged is None else merged + term
    y = jnp.dot(merged.astype(MXU_DTYPE), wo_ref[...], preferred_element_type=F32)
    x1 = _layer_norm(DEEPNORM_ALPHA * x + (1.0 + gate1) * y, lng_ref[...], lnb_ref[...])
    x1_ref[...] = x1
    h2 = x1 * (1.0 + scale2) + shift2
    h2_ref[...] = _pack_halves(h2)
    logit_ref[...] = _nt_dot(wr_ref[...], h2.astype(MXU_DTYPE)) + br_ref[...]


def _merge(x2d, ada_l, o_a, o_b, o_c, w_gate, b_gate, w_pa, w_pb, w_pc, w_o, ln_g, ln_b,
           w_router_t, b_router, seq):
    t, d = x2d.shape
    tm = min(ROW_TILE, seq)
    tiles_per_seq = seq // tm
    row = lambda i: (i, 0)
    whole = lambda a: pl.BlockSpec(a.shape, lambda i: (0,) * a.ndim)
    return pl.pallas_call(
        _merge_kernel,
        grid=(t // tm,),
        in_specs=[
            pl.BlockSpec((tm, d), row),
            pl.BlockSpec((None, 6, d), lambda i: (i // tiles_per_seq, 0, 0)),
            pl.BlockSpec((tm, ATT_WIDTH), row), pl.BlockSpec((tm, ATT_WIDTH), row),
            pl.BlockSpec((tm, LRU_WIDTH), row),
            whole(w_gate), whole(b_gate), whole(w_pa), whole(w_pb), whole(w_pc), whole(w_o),
            whole(ln_g), whole(ln_b), whole(w_router_t), whole(b_router),
        ],
        out_specs=[pl.BlockSpec((tm, d), row), pl.BlockSpec((tm, d // 2), row),
                   pl.BlockSpec((N_EXPERTS, tm), lambda i: (0, i))],
        out_shape=[jax.ShapeDtypeStruct((t, d), F32),
                   jax.ShapeDtypeStruct((t, d // 2), jnp.uint32),
                   jax.ShapeDtypeStruct((N_EXPERTS, t), F32)],
        compiler_params=_params(1),
        name="merge_outproj_ln",
    )(x2d, ada_l, o_a, o_b, o_c, w_gate, b_gate, w_pa, w_pb, w_pc, w_o, ln_g, ln_b,
      w_router_t, b_router)


def _route_kernel(logit_ref, tri_ref, eid_ref, gate_ref, pos_ref, cnt_ref, carry_ref):
    i = pl.program_id(0)
    tr = logit_ref.shape[1]

    @pl.when(i == 0)
    def _():
        carry_ref[...] = jnp.zeros_like(carry_ref)

    erow = lax.broadcasted_iota(jnp.int32, (N_EXPERTS, tr), 0)
    cur = logit_ref[...]
    vals, ids = [], []
    for _ in range(TOP_K):
        m = jnp.max(cur, axis=0, keepdims=True)
        idx = jnp.min(jnp.where(cur == m, erow, N_EXPERTS), axis=0, keepdims=True)
        vals.append(m)
        ids.append(idx)
        cur = jnp.where(erow == idx, -jnp.inf, cur)
    exps = [jnp.exp(v - vals[0]) for v in vals]
    denom = exps[0] + exps[1] + exps[2] + exps[3]
    chosen = jnp.zeros((N_EXPERTS, tr), F32)
    for idx in ids:
        chosen = chosen + jnp.where(erow == idx, 1.0, 0.0)
    prefix = (jnp.dot(chosen.astype(MXU_DTYPE), tri_ref[...], preferred_element_type=F32)
              + carry_ref[:, 0:1])
    for k in range(TOP_K):
        eid_ref[k:k + 1, :] = ids[k]
        gate_ref[k:k + 1, :] = exps[k] / denom
        pos_ref[k:k + 1, :] = jnp.sum(jnp.where(erow == ids[k], prefix, 0.0), axis=0,
                                      keepdims=True).astype(jnp.int32)
    carry_ref[...] = carry_ref[...] + jnp.sum(chosen, axis=1, keepdims=True)
    cnt_ref[...] = carry_ref[...].astype(jnp.int32)


def _route(logits_t):
    n_e, t = logits_t.shape
    tr = min(ROUTE_TILE, t)
    tri = (lax.broadcasted_iota(jnp.int32, (tr, tr), 0)
           < lax.broadcasted_iota(jnp.int32, (tr, tr), 1)).astype(MXU_DTYPE)
    tok = lambda i: (0, i)
    return pl.pallas_call(
        _route_kernel,
        grid=(t // tr,),
        in_specs=[pl.BlockSpec((n_e, tr), tok), pl.BlockSpec((tr, tr), lambda i: (0, 0))],
        out_specs=[pl.BlockSpec((TOP_K, tr), tok)] * 3
        + [pl.BlockSpec((n_e, LANES), lambda i: (0, 0))],
        out_shape=[jax.ShapeDtypeStruct((TOP_K, t), jnp.int32),
                   jax.ShapeDtypeStruct((TOP_K, t), F32),
                   jax.ShapeDtypeStruct((TOP_K, t), jnp.int32),
                   jax.ShapeDtypeStruct((n_e, LANES), jnp.int32)],
        scratch_shapes=[pltpu.VMEM((n_e, LANES), F32)],
        compiler_params=_params(1),
        name="route_topk",
    )(logits_t, tri)


def _dest_kernel(start_ref, eid_ref, pos_ref, dest_ref):
    eid = eid_ref[...]
    dest = pos_ref[...]
    for e in range(N_EXPERTS):
        dest = jnp.where(eid == e, dest + start_ref[e], dest)
    dest_ref[...] = dest


def _dest(pad_start, eid, pos):
    k, t = eid.shape
    tr = min(ROUTE_TILE, t)
    spec = pl.BlockSpec((k, tr), lambda i, s: (0, i))
    return pl.pallas_call(
        _dest_kernel,
        grid_spec=pltpu.PrefetchScalarGridSpec(
            num_scalar_prefetch=1, grid=(t // tr,), in_specs=[spec, spec], out_specs=spec),
        out_shape=jax.ShapeDtypeStruct((k, t), jnp.int32),
        compiler_params=_params(1),
        name="route_dest",
    )(pad_start, eid, pos)


def _dispatch_kernel(dest_ref, h_ref, init_ref, buf_ref, sem):
    del init_ref
    tm = h_ref.shape[0]

    def row_copy(r, k):
        return pltpu.make_async_copy(h_ref.at[pl.ds(r, 1), :],
                                     buf_ref.at[pl.ds(dest_ref[r * TOP_K + k], 1), :], sem)

    def issue(r, _):
        for k in range(TOP_K):
            row_copy(r, k).start(priority=k % 2)
        return 0

    lax.fori_loop(0, tm, issue, 0)
    for k in range(TOP_K):
        pltpu.make_async_copy(h_ref, buf_ref.at[pl.ds(0, tm), :], sem).wait()


def _dispatch(dest, h2, n_slots):
    t, d = h2.shape
    tm = min(MOVE_TILE, t)
    init = jnp.zeros((n_slots, d), h2.dtype)
    return pl.pallas_call(
        _dispatch_kernel,
        grid=(t // tm,),
        in_specs=[
            pl.BlockSpec((TOP_K * tm,), lambda i: (i,), memory_space=pltpu.SMEM),
            pl.BlockSpec((tm, d), lambda i: (i, 0)),
            pl.BlockSpec(memory_space=pl.ANY),
        ],
        out_specs=pl.BlockSpec(memory_space=pl.ANY),
        out_shape=jax.ShapeDtypeStruct((n_slots, d), h2.dtype),
        scratch_shapes=[pltpu.SemaphoreType.DMA(())],
        input_output_aliases={2: 0},
        compiler_params=_params(1),
        name="moe_dispatch",
    )(dest, h2, init)


def _expert_kernel(be_ref, nused_ref, x_ref, wgu_ref, bgu_ref, wd_ref, bd_ref, y_ref,
                   wgu_lp, wd_lp):
    i = pl.program_id(0)
    d_e = wd_ref.shape[0]

    @pl.when((i == 0) | (be_ref[i] != be_ref[jnp.maximum(i - 1, 0)]))
    def _():
        wgu_lp[...] = wgu_ref[...].astype(MXU_DTYPE)
        wd_lp[...] = wd_ref[...].astype(MXU_DTYPE)

    @pl.when(i < nused_ref[0])
    def _():
        lo, hi = _unpack_halves(x_ref[...])
        x = jnp.concatenate([lo, hi], axis=1).astype(MXU_DTYPE)
        gu = jnp.dot(x, wgu_lp[...], preferred_element_type=F32) + bgu_ref[...]
        gate = jnp.minimum(gu[:, :d_e], SWIGLU_LIMIT)
        up = jnp.clip(gu[:, d_e:], -SWIGLU_LIMIT, SWIGLU_LIMIT)
        act = (up + 1.0) * (gate * jax.nn.sigmoid(SWIGLU_ALPHA * gate))
        y = jnp.dot(act.astype(MXU_DTYPE), wd_lp[...], preferred_element_type=F32) + bd_ref[...]
        y_ref[...] = _pack_halves(y)

    @pl.when(i >= nused_ref[0])
    def _():
        y_ref[...] = jnp.zeros_like(y_ref)


def _experts(layer, block_e, n_used, x_buf, w_gu, b_gu, w_down, b_down):
    n_slots, half = x_buf.shape
    _, n_e, d, d2 = w_gu.shape
    d_e = w_down.shape[2]
    n_blocks = n_slots // EXPERT_BLOCK
    n_layers = w_gu.shape[0]
    return pl.pallas_call(
        _expert_kernel,
        grid_spec=pltpu.PrefetchScalarGridSpec(
            num_scalar_prefetch=2,
            grid=(n_blocks,),
            in_specs=[
                pl.BlockSpec((EXPERT_BLOCK, half), lambda i, be, nu: (i, 0)),
                pl.BlockSpec((None, None, d, d2), lambda i, be, nu: (layer, be[i], 0, 0)),
                pl.BlockSpec((None, None, 1, d2), lambda i, be, nu: (layer, be[i], 0, 0)),
                pl.BlockSpec((None, None, d_e, d), lambda i, be, nu: (layer, be[i], 0, 0)),
                pl.BlockSpec((None, None, 1, d), lambda i, be, nu: (layer, be[i], 0, 0)),
            ],
            out_specs=pl.BlockSpec((EXPERT_BLOCK, half), lambda i, be, nu: (i, 0)),
            scratch_shapes=[pltpu.VMEM((d, d2), MXU_DTYPE), pltpu.VMEM((d_e, d), MXU_DTYPE)],
        ),
        out_shape=jax.ShapeDtypeStruct((n_slots, half), jnp.uint32),
        compiler_params=_params(1),
        name="moe_experts",
    )(block_e, n_used, x_buf, w_gu, b_gu.reshape(n_layers, n_e, 1, d2), w_down,
      b_down.reshape(n_layers, n_e, 1, d))


def _sc_gather_rows(table, idx):
    n_rows = idx.shape[0]
    width = table.shape[1]
    per_worker = n_rows // (SC_CORES * SC_SUBCORES)
    n_chunks = per_worker // SC_CHUNK
    mesh = plsc.VectorSubcoreMesh(core_axis_name="c", subcore_axis_name="s")

    @functools.partial(
        pl.kernel, mesh=mesh,
        out_type=jax.ShapeDtypeStruct((n_rows, width), table.dtype),
        scratch_types=[pltpu.VMEM((SC_CHUNK,), jnp.int32),
                       pltpu.VMEM((SC_CHUNK, width), table.dtype),
                       pltpu.SemaphoreType.DMA],
    )
    def gather(table_hbm, idx_hbm, out_hbm, idx_v, rows_v, sem):
        wid = lax.axis_index("s") * SC_CORES + lax.axis_index("c")
        base = wid * per_worker

        @pl.loop(0, n_chunks)
        def _(g):
            off = base + g * SC_CHUNK
            pltpu.sync_copy(idx_hbm.at[pl.ds(off, SC_CHUNK)], idx_v)
            pltpu.async_copy(table_hbm.at[idx_v], rows_v, sem).wait()
            pltpu.sync_copy(rows_v, out_hbm.at[pl.ds(off, SC_CHUNK)])

    return gather(table, idx)


def _combine_kernel(x1_ref, ada_ref, gate_ref, lng_ref, lnb_ref, *rest):
    y_refs, o_ref = rest[:TOP_K], rest[TOP_K]
    gates = gate_ref[...]
    y_lo = y_hi = None
    for k in range(TOP_K):
        lo, hi = _unpack_halves(y_refs[k][...])
        g = gates[:, k:k + 1]
        y_lo = lo * g if y_lo is None else y_lo + lo * g
        y_hi = hi * g if y_hi is None else y_hi + hi * g
    y = jnp.concatenate([y_lo, y_hi], axis=1)
    gate2 = ada_ref[5:6, :]
    o_ref[...] = _layer_norm(DEEPNORM_ALPHA * x1_ref[...] + (1.0 + gate2) * y,
                             lng_ref[...], lnb_ref[...])


def _combine(x1, ada_l, gates_tk, ln_g, ln_b, y_rows, seq):
    t, d = x1.shape
    tm = min(MOVE_TILE, seq)
    tiles_per_seq = seq // tm
    return pl.pallas_call(
        _combine_kernel,
        grid=(t // tm,),
        in_specs=[
            pl.BlockSpec((tm, d), lambda i: (i, 0)),
            pl.BlockSpec((None, 6, d), lambda i: (i // tiles_per_seq, 0, 0)),
            pl.BlockSpec((tm, TOP_K), lambda i: (i, 0)),
            pl.BlockSpec((1, d), lambda i: (0, 0)),
            pl.BlockSpec((1, d), lambda i: (0, 0)),
        ] + [
            pl.BlockSpec((tm, d // 2), lambda i, k=k: (k * (t // tm) + i, 0))
            for k in range(TOP_K)
        ],
        out_specs=pl.BlockSpec((tm, d), lambda i: (i, 0)),
        out_shape=jax.ShapeDtypeStruct((t, d), F32),
        compiler_params=_params(1),
        name="moe_combine_ln",
    )(x1, ada_l, gates_tk, ln_g, ln_b, *([y_rows] * TOP_K))


def _split_w_in(w_in):
    d = w_in.shape[0]
    sizes = (ATT_WIDTH, ATT_WIDTH, ATT_WIDTH, N_HEADS, ATT_WIDTH, ATT_WIDTH, ATT_WIDTH,
             LRU_WIDTH, LRU_WIDTH)
    bounds = [sum(sizes[:n]) for n in range(len(sizes) + 1)]
    qa, ka, va, fa, qb, kb, vb, xc, gc = (
        w_in[:, bounds[n]:bounds[n + 1]] for n in range(len(sizes)))
    forget = jnp.concatenate([fa, jnp.zeros((d, LANES - N_HEADS), w_in.dtype)], axis=1)
    w_main = jnp.concatenate([qa, ka, qb, kb, xc, gc, forget], axis=1).astype(MXU_DTYPE)
    w_vt = jnp.concatenate([va, vb], axis=1).T.astype(MXU_DTYPE)
    return w_main, w_vt


def _block_diag(w):
    n, c, dd = w.shape
    eye = jnp.eye(n, dtype=w.dtype)
    return (eye[:, None, :, None] * w[:, :, None, :]).reshape(n * c, n * dd).astype(MXU_DTYPE)


def kernel(x, c, w_ada, b_ada, ln1_g, ln1_b, w_in, b_f, conv_w, conv_b, lru_wa, lru_ba, lru_wx,
           lru_bx, lru_lambda, w_gate, b_gate, w_pa, w_pb, w_pc, w_o, ln2_g, ln2_b, w_router,
           b_router, w_gu, b_gu, w_down, b_down):
    n_batch, seq, d = x.shape
    t = n_batch * seq
    n_layers = w_ada.shape[0]
    n_blocks = (t * TOP_K) // EXPERT_BLOCK + N_EXPERTS
    n_slots = n_blocks * EXPERT_BLOCK
    att_tile = min(ATT_TILE, seq)
    vec = lambda a: a.reshape(1, -1)

    ada = _ada(c, w_ada, b_ada).reshape(n_layers, n_batch, 6, d)
    x2d = x.reshape(t, d)
    for l in range(n_layers):
        ada_l = ada[l]
        bf_pad = jnp.concatenate([b_f[l], jnp.zeros((LANES - N_HEADS,), F32)]).reshape(1, LANES)
        w_main, w_vt = _split_w_in(w_in[l])
        qa, ka, qb, kb, xc, gc, vat, vbt, fcol, frow = _inproj(
            x2d, ada_l, w_main, w_vt, bf_pad, seq)
        frow4 = frow.reshape(n_batch, N_HEADS, seq // att_tile, att_tile)
        o_a = _fox(qa, ka, vat, fcol, frow4, seq)
        o_b = _sb(qb, kb, vbt, seq)
        o_c = _lru(xc, gc, conv_w[l], vec(conv_b[l]), _block_diag(lru_wa[l]), vec(lru_ba[l]),
                   _block_diag(lru_wx[l]), vec(lru_bx[l]), vec(lru_lambda[l]), seq)
        x1, h2, logits_t = _merge(
            x2d, ada_l, o_a, o_b, o_c, w_gate[l].astype(MXU_DTYPE), vec(b_gate[l]),
            w_pa[l].astype(MXU_DTYPE), w_pb[l].astype(MXU_DTYPE), w_pc[l].astype(MXU_DTYPE),
            w_o[l].astype(MXU_DTYPE), vec(ln1_g[l]), vec(ln1_b[l]),
            w_router[l].T.astype(MXU_DTYPE), b_router[l].reshape(N_EXPERTS, 1), seq)

        eid, gates, pos, cnt = _route(logits_t)
        counts = cnt[:, 0]
        padded = (counts + EXPERT_BLOCK - 1) // EXPERT_BLOCK * EXPERT_BLOCK
        pad_end = jnp.cumsum(padded)
        pad_start = (pad_end - padded).astype(jnp.int32)
        block_first = (jnp.arange(n_blocks) * EXPERT_BLOCK)[:, None]
        block_e = jnp.minimum(jnp.sum(pad_end[None, :] <= block_first, axis=1),
                              N_EXPERTS - 1).astype(jnp.int32)
        n_used = (pad_end[-1:] // EXPERT_BLOCK).astype(jnp.int32)
        dest = _dest(pad_start, eid, pos)

        x_buf = _dispatch(dest.T.reshape(-1), h2, n_slots)
        y_buf = _experts(l, block_e, n_used, x_buf, w_gu, b_gu, w_down, b_down)
        y_rows = _sc_gather_rows(y_buf, dest.reshape(-1))
        x2d = _combine(x1, ada_l, gates.T, vec(ln2_g[l]), vec(ln2_b[l]), y_rows, seq)
    return x2d.reshape(n_batch, seq, d)
```

```python
import functools

import jax
import jax.numpy as jnp
from jax import lax
from jax.experimental import pallas as pl
from jax.experimental.pallas import tpu as pltpu
from jax.experimental.pallas import tpu_sc as plsc

D_MODEL = 1024
DEPTH = 2
HEAD_DIM = 64
N_HEADS = 8
ATT_WIDTH = N_HEADS * HEAD_DIM
LRU_WIDTH = D_MODEL
LRU_BLOCKS = 16
CONV_WIDTH = 4
LRU_C = 8.0
N_EXPERTS = 32
TOP_K = 4
SWIGLU_LIMIT = 7.0
SWIGLU_ALPHA = 1.702
LN_EPS = 1e-5
DEEPNORM_ALPHA = (2.0 * DEPTH) ** 0.25
ATT_SCALE = HEAD_DIM ** -0.5
LOG2_E = 1.4426950408889634

LANES = 128
MXU_DTYPE = jnp.bfloat16
F32 = jnp.float32

ROW_TILE = 512
ATT_TILE = 256
ATT_GROUP = 8
LRU_TILE = 256
ROUTE_TILE = 512
EXPERT_BLOCK = 256
MOVE_TILE = 256
SC_CORES = 2
SC_SUBCORES = 16
SC_CHUNK = 64
VMEM_LIMIT = 56 * 1024 * 1024


def _params(n_axes, vmem=VMEM_LIMIT):
    return pltpu.CompilerParams(
        dimension_semantics=("arbitrary",) * n_axes, vmem_limit_bytes=vmem)


def _log_sigmoid(x):
    return jnp.minimum(x, 0.0) - jnp.log1p(jnp.exp(-jnp.abs(x)))


def _softplus(x):
    return jnp.maximum(x, 0.0) + jnp.log1p(jnp.exp(-jnp.abs(x)))


def _layer_norm(v, g, b):
    mu = jnp.mean(v, axis=-1, keepdims=True)
    d = v - mu
    var = jnp.mean(d * d, axis=-1, keepdims=True)
    return d * lax.rsqrt(var + LN_EPS) * g + b


def _pack_halves(x):
    n = x.shape[1] // 2
    bits = lambda v: lax.bitcast_convert_type(v.astype(jnp.bfloat16).astype(F32), jnp.uint32)
    return (bits(x[:, :n]) >> 16) | bits(x[:, n:])


def _unpack_halves(p):
    lo = lax.bitcast_convert_type(p << 16, F32)
    hi = lax.bitcast_convert_type(p & jnp.uint32(0xFFFF0000), F32)
    return lo, hi


def _split3(x):
    hi = x.astype(MXU_DTYPE)
    r1 = x - hi.astype(F32)
    mid = r1.astype(MXU_DTYPE)
    lo = (r1 - mid.astype(F32)).astype(MXU_DTYPE)
    return hi, mid, lo


def _ada_kernel(c_ref, w_ref, b_ref, o_ref):
    c = c_ref[...]
    cond = c * jax.nn.sigmoid(c)
    o_ref[...] = jnp.dot(cond.astype(MXU_DTYPE), w_ref[...].astype(MXU_DTYPE),
                         preferred_element_type=F32) + b_ref[...]


def _ada(c, w_ada, b_ada):
    n_layers, d, n6 = w_ada.shape
    b = c.shape[0]
    tn = n6 // 4
    return pl.pallas_call(
        _ada_kernel,
        grid=(n_layers, n6 // tn),
        in_specs=[
            pl.BlockSpec((b, d), lambda l, j: (0, 0)),
            pl.BlockSpec((None, d, tn), lambda l, j: (l, 0, j)),
            pl.BlockSpec((None, 1, tn), lambda l, j: (l, 0, j)),
        ],
        out_specs=pl.BlockSpec((None, b, tn), lambda l, j: (l, 0, j)),
        out_shape=jax.ShapeDtypeStruct((n_layers, b, n6), F32),
        compiler_params=_params(2),
        name="ada",
    )(c, w_ada, b_ada.reshape(n_layers, 1, n6))


def _nt_dot(a, b):
    return lax.dot_general(a, b, (((1,), (1,)), ((), ())), preferred_element_type=F32)


def _inproj_kernel(x_ref, ada_ref, w_ref, wvt_ref, bf_ref, tri_ref,
                   qa_ref, ka_ref, qb_ref, kb_ref, xc_ref, gc_ref, vat_ref, vbt_ref,
                   fcol_ref, frow_ref, carry_ref, *, tiles_per_seq):
    i = pl.program_id(0)
    tm = x_ref.shape[0]
    shift = ada_ref[0:1, :]
    scale = ada_ref[1:2, :]
    h = (x_ref[...] * (1.0 + scale) + shift).astype(MXU_DTYPE)
    off = 0
    for ref in (qa_ref, ka_ref, qb_ref, kb_ref, xc_ref, gc_ref):
        width = ref.shape[1]
        ref[...] = jnp.dot(h, w_ref[:, off:off + width],
                           preferred_element_type=F32).astype(ref.dtype)
        off += width
    vt = _nt_dot(wvt_ref[...], h).astype(MXU_DTYPE)
    chunk = vat_ref.shape[2]
    for n in range(tm // chunk):
        cols = slice(n * chunk, (n + 1) * chunk)
        vat_ref[n] = vt[0:ATT_WIDTH, cols]
        vbt_ref[n] = vt[ATT_WIDTH:2 * ATT_WIDTH, cols]
    zf = jnp.dot(h, w_ref[:, off:off + LANES], preferred_element_type=F32) + bf_ref[...]
    logf = _log_sigmoid(zf)

    @pl.when(i % tiles_per_seq == 0)
    def _():
        carry_ref[...] = jnp.zeros_like(carry_ref)

    tri = tri_ref[...]
    f_cum = carry_ref[0:1, :]
    for piece in _split3(logf):
        f_cum = f_cum + jnp.dot(tri, piece, preferred_element_type=F32)
    carry_ref[0:1, :] = f_cum[tm - 1:tm, :]
    fcol_ref[...] = f_cum
    frow_ref[...] = f_cum.T[0:N_HEADS, :]


def _inproj(x2d, ada_l, w_main, w_vt, bf_pad, seq):
    t, d = x2d.shape
    tm = min(ROW_TILE, seq)
    chunk = min(ATT_TILE, seq)
    tiles_per_seq = seq // tm
    n_batch = t // seq
    tri = (lax.broadcasted_iota(jnp.int32, (tm, tm), 1)
           <= lax.broadcasted_iota(jnp.int32, (tm, tm), 0)).astype(MXU_DTYPE)
    row = lambda i: (i, 0)
    att = jax.ShapeDtypeStruct((t, ATT_WIDTH), MXU_DTYPE)
    att_t = jax.ShapeDtypeStruct((t // chunk, ATT_WIDTH, chunk), MXU_DTYPE)
    wide = jax.ShapeDtypeStruct((t, LRU_WIDTH), F32)
    att_spec = pl.BlockSpec((tm, ATT_WIDTH), row)
    att_t_spec = pl.BlockSpec((tm // chunk, ATT_WIDTH, chunk), lambda i: (i, 0, 0))
    wide_spec = pl.BlockSpec((tm, LRU_WIDTH), row)
    return pl.pallas_call(
        functools.partial(_inproj_kernel, tiles_per_seq=tiles_per_seq),
        grid=(t // tm,),
        in_specs=[
            pl.BlockSpec((tm, d), row),
            pl.BlockSpec((None, 6, d), lambda i: (i // tiles_per_seq, 0, 0)),
            pl.BlockSpec(w_main.shape, lambda i: (0, 0)),
            pl.BlockSpec(w_vt.shape, lambda i: (0, 0)),
            pl.BlockSpec((1, LANES), lambda i: (0, 0)),
            pl.BlockSpec((tm, tm), lambda i: (0, 0)),
        ],
        out_specs=[att_spec] * 4 + [wide_spec] * 2 + [att_t_spec] * 2 + [
            pl.BlockSpec((tm, LANES), row),
            pl.BlockSpec((None, N_HEADS, tm),
                         lambda i: (i // tiles_per_seq, 0, i % tiles_per_seq)),
        ],
        out_shape=[att] * 4 + [wide] * 2 + [att_t] * 2 + [
            jax.ShapeDtypeStruct((t, LANES), F32),
            jax.ShapeDtypeStruct((n_batch, N_HEADS, seq), F32),
        ],
        scratch_shapes=[pltpu.VMEM((8, LANES), F32)],
        compiler_params=_params(1),
        name="inproj",
    )(x2d, ada_l, w_main, w_vt, bf_pad, tri)


def _head_query(q2, c):
    lane = lax.broadcasted_iota(jnp.int32, (1, LANES), 1)
    in_head = (lane >= c * HEAD_DIM) & (lane < (c + 1) * HEAD_DIM)
    return jnp.where(in_head, q2, jnp.zeros_like(q2)) * ATT_SCALE


def _pair_cols(head):
    return slice((head // 2) * LANES, (head // 2 + 1) * LANES)


def _store_heads(o_ref, heads, outs_t):
    sub = lax.broadcasted_iota(jnp.int32, (LANES, 1), 0)
    for n in range(0, len(heads), 2):
        pair_t = jnp.where(sub < HEAD_DIM, outs_t[n], outs_t[n + 1])
        o_ref[:, _pair_cols(heads[n])] = pair_t.T.astype(o_ref.dtype)


def _fox_kernel(q_ref, k_ref, vt_ref, fcol_ref, frow_ref, o_ref):
    i = pl.program_id(1)
    tq = q_ref.shape[0]
    key = lax.broadcasted_iota(jnp.int32, (tq, tq), 0)
    qry = lax.broadcasted_iota(jnp.int32, (tq, tq), 1)
    causal = key <= qry
    for g in range(N_HEADS // ATT_GROUP):
        heads = list(range(g * ATT_GROUP, (g + 1) * ATT_GROUP))
        qms = [_head_query(q_ref[:, _pair_cols(hd)], hd % 2) for hd in heads]
        fqs = [frow_ref[hd, pl.ds(i, 1), :] for hd in heads]

        def qk(j, heads=heads, qms=qms):
            start = pl.multiple_of(j * tq, tq)
            return tuple(_nt_dot(k_ref[pl.ds(start, tq), _pair_cols(hd)], qms[n])
                         for n, hd in enumerate(heads))

        def block(j, scores, carry, masked, heads=heads, fqs=fqs):
            start = pl.multiple_of(j * tq, tq)
            stats, probs = [], []
            for n, hd in enumerate(heads):
                m, l, _ = carry[n]
                fk = fcol_ref[pl.ds(start, tq), hd:hd + 1]
                s = scores[n] - fk
                if masked:
                    s = jnp.where(causal, s, -jnp.inf)
                m_new = jnp.maximum(m, jnp.max(s, axis=0, keepdims=True) + fqs[n])
                alpha = jnp.exp(m - m_new)
                pm = jnp.exp(s + (fqs[n] - m_new))
                l = alpha * l + jnp.sum(pm, axis=0, keepdims=True)
                stats.append((m_new, l, alpha))
                probs.append(pm.astype(MXU_DTYPE))
            pvs = [jnp.dot(vt_ref[j, _pair_cols(hd), :], probs[n],
                           preferred_element_type=F32) for n, hd in enumerate(heads)]
            return tuple((stats[n][0], stats[n][1], stats[n][2] * carry[n][2] + pvs[n])
                         for n in range(len(heads)))

        init = tuple((jnp.full((1, tq), -jnp.inf, F32), jnp.zeros((1, tq), F32),
                      jnp.zeros((LANES, tq), F32)) for _ in heads)
        carry = lax.fori_loop(
            0, i, lambda j, cr, qk=qk, block=block: block(j, qk(j), cr, False), init)
        carry = block(i, qk(i), carry, True)
        _store_heads(o_ref, heads, [acc / l for _, l, acc in carry])


def _att_specs(t, seq):
    n_batch = t // seq
    tq = min(ATT_TILE, seq)
    nq = seq // tq
    q_spec = pl.BlockSpec((tq, ATT_WIDTH), lambda b, i: (b * nq + i, 0))
    k_spec = pl.BlockSpec((seq, ATT_WIDTH), lambda b, i: (b, 0))
    vt_spec = pl.BlockSpec((nq, ATT_WIDTH, tq), lambda b, i: (b, 0, 0))
    return n_batch, tq, nq, q_spec, k_spec, vt_spec


def _fox(q, k, vt, fcol, frow4, seq):
    t = q.shape[0]
    n_batch, tq, nq, q_spec, k_spec, vt_spec = _att_specs(t, seq)
    return pl.pallas_call(
        _fox_kernel,
        grid=(n_batch, nq),
        in_specs=[
            q_spec, k_spec, vt_spec,
            pl.BlockSpec((seq, LANES), lambda b, i: (b, 0)),
            pl.BlockSpec((None, N_HEADS, nq, tq), lambda b, i: (b, 0, 0, 0)),
        ],
        out_specs=q_spec,
        out_shape=jax.ShapeDtypeStruct((t, ATT_WIDTH), MXU_DTYPE),
        compiler_params=_params(2),
        name="fox_attention",
    )(q, k, vt, fcol, frow4)


def _sb_kernel(q_ref, k_ref, vt_ref, suf_ref, o_ref):
    i = pl.program_id(1)
    tq = q_ref.shape[0]
    key = lax.broadcasted_iota(jnp.int32, (tq, tq), 0)
    qry = lax.broadcasted_iota(jnp.int32, (tq, tq), 1)
    strict = key < qry
    suf = suf_ref[...]
    for g in range(N_HEADS // ATT_GROUP):
        heads = list(range(g * ATT_GROUP, (g + 1) * ATT_GROUP))
        qms = [_head_query(q_ref[:, _pair_cols(hd)], hd % 2) for hd in heads]

        def qk(j, heads=heads, qms=qms):
            start = pl.multiple_of(j * tq, tq)
            return tuple(_nt_dot(k_ref[pl.ds(start, tq), _pair_cols(hd)], qms[n])
                         for n, hd in enumerate(heads))

        def block(j, zs, carry, masked, heads=heads):
            log_betas, splits, laters = [], [], []
            for n in range(len(heads)):
                z = zs[n]
                sp = jnp.maximum(z, 0.0) + jnp.log(1.0 + jnp.exp2(jnp.abs(z) * -LOG2_E))
                log_betas.append(z - sp)
                if masked:
                    sp = jnp.where(strict, sp, 0.0)
                splits.append(sp.astype(MXU_DTYPE))
                laters.append(carry[n][0] + jnp.sum(sp, axis=0, keepdims=True))
            afters = [jnp.dot(suf, splits[n], preferred_element_type=F32)
                      for n in range(len(heads))]
            ws = []
            for n in range(len(heads)):
                w = jnp.exp(log_betas[n] - jnp.maximum(afters[n] + carry[n][0], 0.0))
                if masked:
                    w = jnp.where(strict, w, 0.0)
                ws.append(w.astype(MXU_DTYPE))
            return tuple(
                (laters[n], carry[n][1] + jnp.dot(vt_ref[j, _pair_cols(hd), :], ws[n],
                                                  preferred_element_type=F32))
                for n, hd in enumerate(heads))

        init = tuple((jnp.zeros((1, tq), F32), jnp.zeros((LANES, tq), F32)) for _ in heads)

        def step(s, state, qk=qk, block=block):
            zs, carry = state
            j = i - 1 - s
            nxt = qk(jnp.maximum(j - 1, 0))
            return nxt, block(j, zs, carry, False)

        diag = qk(i)
        first = qk(jnp.maximum(i - 1, 0))
        carry = block(i, diag, init, True)
        _, carry = lax.fori_loop(0, i, step, (first, carry))
        _store_heads(o_ref, heads, [acc for _, acc in carry])


def _sb(q, k, vt, seq):
    t = q.shape[0]
    n_batch, tq, nq, q_spec, k_spec, vt_spec = _att_specs(t, seq)
    suf = (lax.broadcasted_iota(jnp.int32, (tq, tq), 1)
           > lax.broadcasted_iota(jnp.int32, (tq, tq), 0)).astype(MXU_DTYPE)
    return pl.pallas_call(
        _sb_kernel,
        grid=(n_batch, nq),
        in_specs=[q_spec, k_spec, vt_spec, pl.BlockSpec((tq, tq), lambda b, i: (0, 0))],
        out_specs=q_spec,
        out_shape=jax.ShapeDtypeStruct((t, ATT_WIDTH), MXU_DTYPE),
        compiler_params=_params(2),
        name="sb_attention",
    )(q, k, vt, suf)


def _lru_kernel(xc_ref, gc_ref, cw_ref, cb_ref, wa_ref, ba_ref, wx_ref, bx_ref, lam_ref,
                o_ref, ext_ref, a_ref, u_ref, h_ref, state_ref):
    s_idx = pl.program_id(1)
    ts = xc_ref.shape[0]
    halo = 8

    @pl.when(s_idx == 0)
    def _():
        ext_ref[0:halo, :] = jnp.zeros((halo, LRU_WIDTH), F32)
        state_ref[...] = jnp.zeros_like(state_ref)

    ext_ref[halo:halo + ts, :] = xc_ref[...]
    xconv = cb_ref[...]
    for j in range(CONV_WIDTH):
        lo = halo - (CONV_WIDTH - 1) + j
        xconv = xconv + cw_ref[j:j + 1, :] * ext_ref[lo:lo + ts, :]
    ext_ref[0:halo, :] = ext_ref[ts:ts + halo, :]

    xb = xconv.astype(MXU_DTYPE)
    r = jax.nn.sigmoid(jnp.dot(xb, wa_ref[...], preferred_element_type=F32) + ba_ref[...])
    ig = jax.nn.sigmoid(jnp.dot(xb, wx_ref[...], preferred_element_type=F32) + bx_ref[...])
    log_a = (-LRU_C * _softplus(-lam_ref[...])) * r
    a = jnp.exp(log_a)
    a_ref[...] = a
    u_ref[...] = jnp.sqrt(1.0 - a * a) * (ig * xconv)

    def step(t, h):
        h = a_ref[pl.ds(t, 1), :] * h + u_ref[pl.ds(t, 1), :]
        h_ref[pl.ds(t, 1), :] = h
        return h

    state_ref[0:1, :] = lax.fori_loop(0, ts, step, state_ref[0:1, :], unroll=8)
    o_ref[...] = (h_ref[...] * jax.nn.gelu(gc_ref[...])).astype(o_ref.dtype)


def _lru(xc, gc, conv_w, conv_b, wa_bd, ba, wx_bd, bx, lam, seq):
    t, w = xc.shape
    n_batch = t // seq
    ts = min(LRU_TILE, seq)
    ns = seq // ts
    row_spec = pl.BlockSpec((ts, w), lambda b, s: (b * ns + s, 0))
    vec_spec = pl.BlockSpec((1, w), lambda b, s: (0, 0))
    mat_spec = pl.BlockSpec((w, w), lambda b, s: (0, 0))
    return pl.pallas_call(
        _lru_kernel,
        grid=(n_batch, ns),
        in_specs=[row_spec, row_spec,
                  pl.BlockSpec((CONV_WIDTH, w), lambda b, s: (0, 0)), vec_spec,
                  mat_spec, vec_spec, mat_spec, vec_spec, vec_spec],
        out_specs=row_spec,
        out_shape=jax.ShapeDtypeStruct((t, w), MXU_DTYPE),
        scratch_shapes=[pltpu.VMEM((ts + 8, w), F32), pltpu.VMEM((ts, w), F32),
                        pltpu.VMEM((ts, w), F32), pltpu.VMEM((ts, w), F32),
                        pltpu.VMEM((8, w), F32)],
        compiler_params=_params(2),
        name="rg_lru",
    )(xc, gc, conv_w, conv_b, wa_bd, ba, wx_bd, bx, lam)


def _merge_kernel(x_ref, ada_ref, oa_ref, ob_ref, oc_ref, wg_ref, bg_ref, wpa_ref, wpb_ref,
                  wpc_ref, wo_ref, lng_ref, lnb_ref, wr_ref, br_ref,
                  x1_ref, h2_ref, logit_ref):
    d = x_ref.shape[1]
    x = x_ref[...]
    shift1, scale1, gate1 = ada_ref[0:1, :], ada_ref[1:2, :], ada_ref[2:3, :]
    shift2, scale2 = ada_ref[3:4, :], ada_ref[4:5, :]
    h = (x * (1.0 + scale1) + shift1).astype(MXU_DTYPE)
    merged = None
    for n, (o_ref, w_ref) in enumerate(((oa_ref, wpa_ref), (ob_ref, wpb_ref), (oc_ref, wpc_ref))):
        cs = slice(n * d, (n + 1) * d)
        g = jax.nn.sigmoid(jnp.dot(h, wg_ref[:, cs], preferred_element_type=F32) + bg_ref[:, cs])
        term = g * jnp.dot(o_ref[...], w_ref[...], preferred_element_type=F32)
        merged = term if merged is None else merged + term
    y = jnp.dot(merged.astype(MXU_DTYPE), wo_ref[...], preferred_element_type=F32)
    x1 = _layer_norm(DEEPNORM_ALPHA * x + (1.0 + gate1) * y, lng_ref[...], lnb_ref[...])
    x1_ref[...] = x1
    h2 = x1 * (1.0 + scale2) + shift2
    h2_ref[...] = _pack_halves(h2)
    logit_ref[...] = _nt_dot(wr_ref[...], h2.astype(MXU_DTYPE)) + br_ref[...]


def _merge(x2d, ada_l, o_a, o_b, o_c, w_gate, b_gate, w_pa, w_pb, w_pc, w_o, ln_g, ln_b,
           w_router_t, b_router, seq):
    t, d = x2d.shape
    tm = min(ROW_TILE, seq)
    tiles_per_seq = seq // tm
    row = lambda i: (i, 0)
    whole = lambda a: pl.BlockSpec(a.shape, lambda i: (0,) * a.ndim)
    return pl.pallas_call(
        _merge_kernel,
        grid=(t // tm,),
        in_specs=[
            pl.BlockSpec((tm, d), row),
            pl.BlockSpec((None, 6, d), lambda i: (i // tiles_per_seq, 0, 0)),
            pl.BlockSpec((tm, ATT_WIDTH), row), pl.BlockSpec((tm, ATT_WIDTH), row),
            pl.BlockSpec((tm, LRU_WIDTH), row),
            whole(w_gate), whole(b_gate), whole(w_pa), whole(w_pb), whole(w_pc), whole(w_o),
            whole(ln_g), whole(ln_b), whole(w_router_t), whole(b_router),
        ],
        out_specs=[pl.BlockSpec((tm, d), row), pl.BlockSpec((tm, d // 2), row),
                   pl.BlockSpec((N_EXPERTS, tm), lambda i: (0, i))],
        out_shape=[jax.ShapeDtypeStruct((t, d), F32),
                   jax.ShapeDtypeStruct((t, d // 2), jnp.uint32),
                   jax.ShapeDtypeStruct((N_EXPERTS, t), F32)],
        compiler_params=_params(1),
        name="merge_outproj_ln",
    )(x2d, ada_l, o_a, o_b, o_c, w_gate, b_gate, w_pa, w_pb, w_pc, w_o, ln_g, ln_b,
      w_router_t, b_router)


def _route_kernel(logit_ref, tri_ref, eid_ref, gate_ref, pos_ref, cnt_ref, carry_ref):
    i = pl.program_id(0)
    tr = logit_ref.shape[1]

    @pl.when(i == 0)
    def _():
        carry_ref[...] = jnp.zeros_like(carry_ref)

    erow = lax.broadcasted_iota(jnp.int32, (N_EXPERTS, tr), 0)
    cur = logit_ref[...]
    vals, ids = [], []
    for _ in range(TOP_K):
        m = jnp.max(cur, axis=0, keepdims=True)
        idx = jnp.min(jnp.where(cur == m, erow, N_EXPERTS), axis=0, keepdims=True)
        vals.append(m)
        ids.append(idx)
        cur = jnp.where(erow == idx, -jnp.inf, cur)
    exps = [jnp.exp(v - vals[0]) for v in vals]
    denom = exps[0] + exps[1] + exps[2] + exps[3]
    chosen = jnp.zeros((N_EXPERTS, tr), F32)
    for idx in ids:
        chosen = chosen + jnp.where(erow == idx, 1.0, 0.0)
    prefix = (jnp.dot(chosen.astype(MXU_DTYPE), tri_ref[...], preferred_element_type=F32)
              + carry_ref[:, 0:1])
    for k in range(TOP_K):
        eid_ref[k:k + 1, :] = ids[k]
        gate_ref[k:k + 1, :] = exps[k] / denom
        pos_ref[k:k + 1, :] = jnp.sum(jnp.where(erow == ids[k], prefix, 0.0), axis=0,
                                      keepdims=True).astype(jnp.int32)
    carry_ref[...] = carry_ref[...] + jnp.sum(chosen, axis=1, keepdims=True)
    cnt_ref[...] = carry_ref[...].astype(jnp.int32)


def _route(logits_t):
    n_e, t = logits_t.shape
    tr = min(ROUTE_TILE, t)
    tri = (lax.broadcasted_iota(jnp.int32, (tr, tr), 0)
           < lax.broadcasted_iota(jnp.int32, (tr, tr), 1)).astype(MXU_DTYPE)
    tok = lambda i: (0, i)
    return pl.pallas_call(
        _route_kernel,
        grid=(t // tr,),
        in_specs=[pl.BlockSpec((n_e, tr), tok), pl.BlockSpec((tr, tr), lambda i: (0, 0))],
        out_specs=[pl.BlockSpec((TOP_K, tr), tok)] * 3
        + [pl.BlockSpec((n_e, LANES), lambda i: (0, 0))],
        out_shape=[jax.ShapeDtypeStruct((TOP_K, t), jnp.int32),
                   jax.ShapeDtypeStruct((TOP_K, t), F32),
                   jax.ShapeDtypeStruct((TOP_K, t), jnp.int32),
                   jax.ShapeDtypeStruct((n_e, LANES), jnp.int32)],
        scratch_shapes=[pltpu.VMEM((n_e, LANES), F32)],
        compiler_params=_params(1),
        name="route_topk",
    )(logits_t, tri)


def _dest_kernel(start_ref, eid_ref, pos_ref, dest_ref):
    eid = eid_ref[...]
    dest = pos_ref[...]
    for e in range(N_EXPERTS):
        dest = jnp.where(eid == e, dest + start_ref[e], dest)
    dest_ref[...] = dest


def _dest(pad_start, eid, pos):
    k, t = eid.shape
    tr = min(ROUTE_TILE, t)
    spec = pl.BlockSpec((k, tr), lambda i, s: (0, i))
    return pl.pallas_call(
        _dest_kernel,
        grid_spec=pltpu.PrefetchScalarGridSpec(
            num_scalar_prefetch=1, grid=(t // tr,), in_specs=[spec, spec], out_specs=spec),
        out_shape=jax.ShapeDtypeStruct((k, t), jnp.int32),
        compiler_params=_params(1),
        name="route_dest",
    )(pad_start, eid, pos)


def _sc_mesh():
    return plsc.VectorSubcoreMesh(core_axis_name="c", subcore_axis_name="s")


def _sc_worker():
    return lax.axis_index("s") * SC_CORES + lax.axis_index("c")


def _sc_scatter_rows(rows, idx, n_out):
    t, width = rows.shape
    per_worker = t // (SC_CORES * SC_SUBCORES)
    n_chunks = per_worker // SC_CHUNK

    @functools.partial(
        pl.kernel, mesh=_sc_mesh(),
        out_type=jax.ShapeDtypeStruct((n_out, width), rows.dtype),
        scratch_types=[pltpu.VMEM((SC_CHUNK,), jnp.int32),
                       pltpu.VMEM((SC_CHUNK, width), rows.dtype),
                       pltpu.SemaphoreType.DMA],
    )
    def scatter(rows_hbm, idx_hbm, out_hbm, idx_v, rows_v, sem):
        base = _sc_worker() * per_worker

        @pl.loop(0, n_chunks)
        def _(g):
            off = base + g * SC_CHUNK
            pltpu.sync_copy(rows_hbm.at[pl.ds(off, SC_CHUNK)], rows_v)
            for k in range(TOP_K):
                pltpu.sync_copy(idx_hbm.at[pl.ds(k * t + off, SC_CHUNK)], idx_v)
                pltpu.async_copy(rows_v, out_hbm.at[idx_v], sem).wait()

    return scatter(rows, idx)


def _sc_gather_rows(table, idx):
    n_rows = idx.shape[0]
    width = table.shape[1]
    per_worker = n_rows // (SC_CORES * SC_SUBCORES)
    n_chunks = per_worker // SC_CHUNK

    @functools.partial(
        pl.kernel, mesh=_sc_mesh(),
        out_type=jax.ShapeDtypeStruct((n_rows, width), table.dtype),
        scratch_types=[pltpu.VMEM((SC_CHUNK,), jnp.int32),
                       pltpu.VMEM((SC_CHUNK, width), table.dtype),
                       pltpu.SemaphoreType.DMA],
    )
    def gather(table_hbm, idx_hbm, out_hbm, idx_v, rows_v, sem):
        base = _sc_worker() * per_worker

        @pl.loop(0, n_chunks)
        def _(g):
            off = base + g * SC_CHUNK
            pltpu.sync_copy(idx_hbm.at[pl.ds(off, SC_CHUNK)], idx_v)
            pltpu.async_copy(table_hbm.at[idx_v], rows_v, sem).wait()
            pltpu.sync_copy(rows_v, out_hbm.at[pl.ds(off, SC_CHUNK)])

    return gather(table, idx)


def _expert_kernel(be_ref, valid_ref, x_ref, wgu_ref, bgu_ref, wd_ref, bd_ref, y_ref,
                   wgu_lp, wd_lp):
    i = pl.program_id(0)
    d_e = wd_ref.shape[0]
    n_valid = valid_ref[i]

    @pl.when((i == 0) | (be_ref[i] != be_ref[jnp.maximum(i - 1, 0)]))
    def _():
        wgu_lp[...] = wgu_ref[...].astype(MXU_DTYPE)
        wd_lp[...] = wd_ref[...].astype(MXU_DTYPE)

    @pl.when(n_valid > 0)
    def _():
        row = lax.broadcasted_iota(jnp.int32, x_ref.shape, 0)
        lo, hi = _unpack_halves(jnp.where(row < n_valid, x_ref[...], jnp.uint32(0)))
        x = jnp.concatenate([lo, hi], axis=1).astype(MXU_DTYPE)
        gu = jnp.dot(x, wgu_lp[...], preferred_element_type=F32) + bgu_ref[...]
        gate = jnp.minimum(gu[:, :d_e], SWIGLU_LIMIT)
        up = jnp.clip(gu[:, d_e:], -SWIGLU_LIMIT, SWIGLU_LIMIT)
        act = (up + 1.0) * (gate * jax.nn.sigmoid(SWIGLU_ALPHA * gate))
        y = jnp.dot(act.astype(MXU_DTYPE), wd_lp[...], preferred_element_type=F32) + bd_ref[...]
        y_ref[...] = _pack_halves(y)

    @pl.when(n_valid == 0)
    def _():
        y_ref[...] = jnp.zeros_like(y_ref)


def _experts(layer, block_e, valid, x_buf, w_gu, b_gu, w_down, b_down):
    n_slots, half = x_buf.shape
    _, n_e, d, d2 = w_gu.shape
    d_e = w_down.shape[2]
    n_blocks = n_slots // EXPERT_BLOCK
    n_layers = w_gu.shape[0]
    return pl.pallas_call(
        _expert_kernel,
        grid_spec=pltpu.PrefetchScalarGridSpec(
            num_scalar_prefetch=2,
            grid=(n_blocks,),
            in_specs=[
                pl.BlockSpec((EXPERT_BLOCK, half), lambda i, be, nu: (i, 0)),
                pl.BlockSpec((None, None, d, d2), lambda i, be, nu: (layer, be[i], 0, 0)),
                pl.BlockSpec((None, None, 1, d2), lambda i, be, nu: (layer, be[i], 0, 0)),
                pl.BlockSpec((None, None, d_e, d), lambda i, be, nu: (layer, be[i], 0, 0)),
                pl.BlockSpec((None, None, 1, d), lambda i, be, nu: (layer, be[i], 0, 0)),
            ],
            out_specs=pl.BlockSpec((EXPERT_BLOCK, half), lambda i, be, nu: (i, 0)),
            scratch_shapes=[pltpu.VMEM((d, d2), MXU_DTYPE), pltpu.VMEM((d_e, d), MXU_DTYPE)],
        ),
        out_shape=jax.ShapeDtypeStruct((n_slots, half), jnp.uint32),
        compiler_params=_params(1),
        name="moe_experts",
    )(block_e, valid, x_buf, w_gu, b_gu.reshape(n_layers, n_e, 1, d2), w_down,
      b_down.reshape(n_layers, n_e, 1, d))


def _combine_kernel(x1_ref, ada_ref, gate_ref, lng_ref, lnb_ref, *rest):
    y_refs, o_ref = rest[:TOP_K], rest[TOP_K]
    gates = gate_ref[...]
    y_lo = y_hi = None
    for k in range(TOP_K):
        lo, hi = _unpack_halves(y_refs[k][...])
        g = gates[:, k:k + 1]
        y_lo = lo * g if y_lo is None else y_lo + lo * g
        y_hi = hi * g if y_hi is None else y_hi + hi * g
    y = jnp.concatenate([y_lo, y_hi], axis=1)
    gate2 = ada_ref[5:6, :]
    o_ref[...] = _layer_norm(DEEPNORM_ALPHA * x1_ref[...] + (1.0 + gate2) * y,
                             lng_ref[...], lnb_ref[...])


def _combine(x1, ada_l, gates_tk, ln_g, ln_b, y_rows, seq):
    t, d = x1.shape
    tm = min(MOVE_TILE, seq)
    tiles_per_seq = seq // tm
    return pl.pallas_call(
        _combine_kernel,
        grid=(t // tm,),
        in_specs=[
            pl.BlockSpec((tm, d), lambda i: (i, 0)),
            pl.BlockSpec((None, 6, d), lambda i: (i // tiles_per_seq, 0, 0)),
            pl.BlockSpec((tm, TOP_K), lambda i: (i, 0)),
            pl.BlockSpec((1, d), lambda i: (0, 0)),
            pl.BlockSpec((1, d), lambda i: (0, 0)),
        ] + [
            pl.BlockSpec((tm, d // 2), lambda i, k=k: (k * (t // tm) + i, 0))
            for k in range(TOP_K)
        ],
        out_specs=pl.BlockSpec((tm, d), lambda i: (i, 0)),
        out_shape=jax.ShapeDtypeStruct((t, d), F32),
        compiler_params=_params(1),
        name="moe_combine_ln",
    )(x1, ada_l, gates_tk, ln_g, ln_b, *([y_rows] * TOP_K))


def _split_w_in(w_in):
    d = w_in.shape[0]
    sizes = (ATT_WIDTH, ATT_WIDTH, ATT_WIDTH, N_HEADS, ATT_WIDTH, ATT_WIDTH, ATT_WIDTH,
             LRU_WIDTH, LRU_WIDTH)
    bounds = [sum(sizes[:n]) for n in range(len(sizes) + 1)]
    qa, ka, va, fa, qb, kb, vb, xc, gc = (
        w_in[:, bounds[n]:bounds[n + 1]] for n in range(len(sizes)))
    forget = jnp.concatenate([fa, jnp.zeros((d, LANES - N_HEADS), w_in.dtype)], axis=1)
    w_main = jnp.concatenate([qa, ka, qb, kb, xc, gc, forget], axis=1).astype(MXU_DTYPE)
    w_vt = jnp.concatenate([va, vb], axis=1).T.astype(MXU_DTYPE)
    return w_main, w_vt


def _block_diag(w):
    n, c, dd = w.shape
    eye = jnp.eye(n, dtype=w.dtype)
    return (eye[:, None, :, None] * w[:, :, None, :]).reshape(n * c, n * dd).astype(MXU_DTYPE)


def kernel(x, c, w_ada, b_ada, ln1_g, ln1_b, w_in, b_f, conv_w, conv_b, lru_wa, lru_ba, lru_wx,
           lru_bx, lru_lambda, w_gate, b_gate, w_pa, w_pb, w_pc, w_o, ln2_g, ln2_b, w_router,
           b_router, w_gu, b_gu, w_down, b_down):
    n_batch, seq, d = x.shape
    t = n_batch * seq
    n_layers = w_ada.shape[0]
    n_blocks = (t * TOP_K) // EXPERT_BLOCK + N_EXPERTS
    n_slots = n_blocks * EXPERT_BLOCK
    att_tile = min(ATT_TILE, seq)
    vec = lambda a: a.reshape(1, -1)

    ada = _ada(c, w_ada, b_ada).reshape(n_layers, n_batch, 6, d)
    x2d = x.reshape(t, d)
    for l in range(n_layers):
        ada_l = ada[l]
        bf_pad = jnp.concatenate([b_f[l], jnp.zeros((LANES - N_HEADS,), F32)]).reshape(1, LANES)
        w_main, w_vt = _split_w_in(w_in[l])
        qa, ka, qb, kb, xc, gc, vat, vbt, fcol, frow = _inproj(
            x2d, ada_l, w_main, w_vt, bf_pad, seq)
        frow4 = frow.reshape(n_batch, N_HEADS, seq // att_tile, att_tile)
        o_a = _fox(qa, ka, vat, fcol, frow4, seq)
        o_b = _sb(qb, kb, vbt, seq)
        o_c = _lru(xc, gc, conv_w[l], vec(conv_b[l]), _block_diag(lru_wa[l]), vec(lru_ba[l]),
                   _block_diag(lru_wx[l]), vec(lru_bx[l]), vec(lru_lambda[l]), seq)
        x1, h2, logits_t = _merge(
            x2d, ada_l, o_a, o_b, o_c, w_gate[l].astype(MXU_DTYPE), vec(b_gate[l]),
            w_pa[l].astype(MXU_DTYPE), w_pb[l].astype(MXU_DTYPE), w_pc[l].astype(MXU_DTYPE),
            w_o[l].astype(MXU_DTYPE), vec(ln1_g[l]), vec(ln1_b[l]),
            w_router[l].T.astype(MXU_DTYPE), b_router[l].reshape(N_EXPERTS, 1), seq)

        eid, gates, pos, cnt = _route(logits_t)
        counts = cnt[:, 0]
        padded = (counts + EXPERT_BLOCK - 1) // EXPERT_BLOCK * EXPERT_BLOCK
        pad_end = jnp.cumsum(padded)
        pad_start = (pad_end - padded).astype(jnp.int32)
        block_first = (jnp.arange(n_blocks) * EXPERT_BLOCK)[:, None]
        block_e = jnp.minimum(jnp.sum(pad_end[None, :] <= block_first, axis=1),
                              N_EXPERTS - 1).astype(jnp.int32)
        valid = jnp.clip(counts[block_e] + pad_start[block_e] - block_first[:, 0],
                         0, EXPERT_BLOCK).astype(jnp.int32)
        dest = _dest(pad_start, eid, pos).reshape(-1)

        x_buf = _sc_scatter_rows(h2, dest, n_slots)
        y_buf = _experts(l, block_e, valid, x_buf, w_gu, b_gu, w_down, b_down)
        y_rows = _sc_gather_rows(y_buf, dest)
        x2d = _combine(x1, ada_l, gates.T, vec(ln2_g[l]), vec(ln2_b[l]), y_rows, seq)
    return x2d.reshape(n_batch, seq, d)
```

```python
import functools

import jax
import jax.numpy as jnp
from jax import lax
from jax.experimental import pallas as pl
from jax.experimental.pallas import tpu as pltpu
from jax.experimental.pallas import tpu_sc as plsc

D_MODEL = 1024
DEPTH = 2
HEAD_DIM = 64
N_HEADS = 8
ATT_WIDTH = N_HEADS * HEAD_DIM
LRU_WIDTH = D_MODEL
LRU_BLOCKS = 16
CONV_WIDTH = 4
LRU_C = 8.0
N_EXPERTS = 32
TOP_K = 4
SWIGLU_LIMIT = 7.0
SWIGLU_ALPHA = 1.702
LN_EPS = 1e-5
DEEPNORM_ALPHA = (2.0 * DEPTH) ** 0.25
ATT_SCALE = HEAD_DIM ** -0.5
LOG2_E = 1.4426950408889634

LANES = 128
MXU_DTYPE = jnp.bfloat16
F32 = jnp.float32

ROW_TILE = 512
ATT_TILE = 256
ATT_GROUP = 8
LRU_TILE = 256
ROUTE_TILE = 512
EXPERT_BLOCK = 256
MOVE_TILE = 256
SC_CORES = 2
SC_SUBCORES = 16
SC_CHUNK = 64
VMEM_LIMIT = 56 * 1024 * 1024


def _params(n_axes, vmem=VMEM_LIMIT):
    return pltpu.CompilerParams(
        dimension_semantics=("arbitrary",) * n_axes, vmem_limit_bytes=vmem)


def _log_sigmoid(x):
    return jnp.minimum(x, 0.0) - jnp.log1p(jnp.exp(-jnp.abs(x)))


def _softplus(x):
    return jnp.maximum(x, 0.0) + jnp.log1p(jnp.exp(-jnp.abs(x)))


def _layer_norm(v, g, b):
    mu = jnp.mean(v, axis=-1, keepdims=True)
    d = v - mu
    var = jnp.mean(d * d, axis=-1, keepdims=True)
    return d * lax.rsqrt(var + LN_EPS) * g + b


def _pack_halves(x):
    n = x.shape[1] // 2
    bits = lambda v: lax.bitcast_convert_type(v.astype(jnp.bfloat16).astype(F32), jnp.uint32)
    return (bits(x[:, :n]) >> 16) | bits(x[:, n:])


def _unpack_halves(p):
    lo = lax.bitcast_convert_type(p << 16, F32)
    hi = lax.bitcast_convert_type(p & jnp.uint32(0xFFFF0000), F32)
    return lo, hi


def _split3(x):
    hi = x.astype(MXU_DTYPE)
    r1 = x - hi.astype(F32)
    mid = r1.astype(MXU_DTYPE)
    lo = (r1 - mid.astype(F32)).astype(MXU_DTYPE)
    return hi, mid, lo


def _ada_kernel(c_ref, w_ref, b_ref, o_ref):
    c = c_ref[...]
    cond = c * jax.nn.sigmoid(c)
    o_ref[...] = jnp.dot(cond.astype(MXU_DTYPE), w_ref[...].astype(MXU_DTYPE),
                         preferred_element_type=F32) + b_ref[...]


def _ada(c, w_ada, b_ada):
    n_layers, d, n6 = w_ada.shape
    b = c.shape[0]
    tn = n6 // 4
    return pl.pallas_call(
        _ada_kernel,
        grid=(n_layers, n6 // tn),
        in_specs=[
            pl.BlockSpec((b, d), lambda l, j: (0, 0)),
            pl.BlockSpec((None, d, tn), lambda l, j: (l, 0, j)),
            pl.BlockSpec((None, 1, tn), lambda l, j: (l, 0, j)),
        ],
        out_specs=pl.BlockSpec((None, b, tn), lambda l, j: (l, 0, j)),
        out_shape=jax.ShapeDtypeStruct((n_layers, b, n6), F32),
        compiler_params=_params(2),
        name="ada",
    )(c, w_ada, b_ada.reshape(n_layers, 1, n6))


def _nt_dot(a, b):
    return lax.dot_general(a, b, (((1,), (1,)), ((), ())), preferred_element_type=F32)


def _inproj_kernel(x_ref, ada_ref, w_ref, wvt_ref, bf_ref, tri_ref,
                   qa_ref, ka_ref, qb_ref, kb_ref, xc_ref, gc_ref, vat_ref, vbt_ref,
                   fcol_ref, frow_ref, carry_ref, *, tiles_per_seq):
    i = pl.program_id(0)
    tm = x_ref.shape[0]
    shift = ada_ref[0:1, :]
    scale = ada_ref[1:2, :]
    h = (x_ref[...] * (1.0 + scale) + shift).astype(MXU_DTYPE)
    off = 0
    for ref in (qa_ref, ka_ref, qb_ref, kb_ref, xc_ref, gc_ref):
        width = ref.shape[1]
        ref[...] = jnp.dot(h, w_ref[:, off:off + width],
                           preferred_element_type=F32).astype(ref.dtype)
        off += width
    vt = _nt_dot(wvt_ref[...], h).astype(MXU_DTYPE)
    chunk = vat_ref.shape[2]
    for n in range(tm // chunk):
        cols = slice(n * chunk, (n + 1) * chunk)
        vat_ref[n] = vt[0:ATT_WIDTH, cols]
        vbt_ref[n] = vt[ATT_WIDTH:2 * ATT_WIDTH, cols]
    zf = jnp.dot(h, w_ref[:, off:off + LANES], preferred_element_type=F32) + bf_ref[...]
    logf = _log_sigmoid(zf)

    @pl.when(i % tiles_per_seq == 0)
    def _():
        carry_ref[...] = jnp.zeros_like(carry_ref)

    tri = tri_ref[...]
    f_cum = carry_ref[0:1, :]
    for piece in _split3(logf):
        f_cum = f_cum + jnp.dot(tri, piece, preferred_element_type=F32)
    carry_ref[0:1, :] = f_cum[tm - 1:tm, :]
    fcol_ref[...] = f_cum
    frow_ref[...] = f_cum.T[0:N_HEADS, :]


def _inproj(x2d, ada_l, w_main, w_vt, bf_pad, seq):
    t, d = x2d.shape
    tm = min(ROW_TILE, seq)
    chunk = min(ATT_TILE, seq)
    tiles_per_seq = seq // tm
    n_batch = t // seq
    tri = (lax.broadcasted_iota(jnp.int32, (tm, tm), 1)
           <= lax.broadcasted_iota(jnp.int32, (tm, tm), 0)).astype(MXU_DTYPE)
    row = lambda i: (i, 0)
    att = jax.ShapeDtypeStruct((t, ATT_WIDTH), MXU_DTYPE)
    att_t = jax.ShapeDtypeStruct((t // chunk, ATT_WIDTH, chunk), MXU_DTYPE)
    wide = jax.ShapeDtypeStruct((t, LRU_WIDTH), F32)
    att_spec = pl.BlockSpec((tm, ATT_WIDTH), row)
    att_t_spec = pl.BlockSpec((tm // chunk, ATT_WIDTH, chunk), lambda i: (i, 0, 0))
    wide_spec = pl.BlockSpec((tm, LRU_WIDTH), row)
    return pl.pallas_call(
        functools.partial(_inproj_kernel, tiles_per_seq=tiles_per_seq),
        grid=(t // tm,),
        in_specs=[
            pl.BlockSpec((tm, d), row),
            pl.BlockSpec((None, 6, d), lambda i: (i // tiles_per_seq, 0, 0)),
            pl.BlockSpec(w_main.shape, lambda i: (0, 0)),
            pl.BlockSpec(w_vt.shape, lambda i: (0, 0)),
            pl.BlockSpec((1, LANES), lambda i: (0, 0)),
            pl.BlockSpec((tm, tm), lambda i: (0, 0)),
        ],
        out_specs=[att_spec] * 4 + [wide_spec] * 2 + [att_t_spec] * 2 + [
            pl.BlockSpec((tm, LANES), row),
            pl.BlockSpec((None, N_HEADS, tm),
                         lambda i: (i // tiles_per_seq, 0, i % tiles_per_seq)),
        ],
        out_shape=[att] * 4 + [wide] * 2 + [att_t] * 2 + [
            jax.ShapeDtypeStruct((t, LANES), F32),
            jax.ShapeDtypeStruct((n_batch, N_HEADS, seq), F32),
        ],
        scratch_shapes=[pltpu.VMEM((8, LANES), F32)],
        compiler_params=_params(1),
        name="inproj",
    )(x2d, ada_l, w_main, w_vt, bf_pad, tri)


def _head_query(q2, c):
    lane = lax.broadcasted_iota(jnp.int32, (1, LANES), 1)
    in_head = (lane >= c * HEAD_DIM) & (lane < (c + 1) * HEAD_DIM)
    return jnp.where(in_head, q2, jnp.zeros_like(q2)) * ATT_SCALE


def _pair_cols(head):
    return slice((head // 2) * LANES, (head // 2 + 1) * LANES)


def _store_heads(o_ref, heads, outs_t):
    sub = lax.broadcasted_iota(jnp.int32, (LANES, 1), 0)
    for n in range(0, len(heads), 2):
        pair_t = jnp.where(sub < HEAD_DIM, outs_t[n], outs_t[n + 1])
        o_ref[:, _pair_cols(heads[n])] = pair_t.T.astype(o_ref.dtype)


def _fox_kernel(q_ref, k_ref, vt_ref, fcol_ref, frow_ref, o_ref):
    i = pl.program_id(1)
    tq = q_ref.shape[0]
    key = lax.broadcasted_iota(jnp.int32, (tq, tq), 0)
    qry = lax.broadcasted_iota(jnp.int32, (tq, tq), 1)
    causal = key <= qry
    for g in range(N_HEADS // ATT_GROUP):
        heads = list(range(g * ATT_GROUP, (g + 1) * ATT_GROUP))
        qms = [_head_query(q_ref[:, _pair_cols(hd)], hd % 2) for hd in heads]
        fqs = [frow_ref[hd, pl.ds(i, 1), :] for hd in heads]

        def qk(j, heads=heads, qms=qms):
            start = pl.multiple_of(j * tq, tq)
            return tuple(_nt_dot(k_ref[pl.ds(start, tq), _pair_cols(hd)], qms[n])
                         for n, hd in enumerate(heads))

        def block(j, scores, carry, masked, heads=heads, fqs=fqs):
            start = pl.multiple_of(j * tq, tq)
            stats, probs = [], []
            for n, hd in enumerate(heads):
                m, l, _ = carry[n]
                fk = fcol_ref[pl.ds(start, tq), hd:hd + 1]
                s = scores[n] - fk
                if masked:
                    s = jnp.where(causal, s, -jnp.inf)
                m_new = jnp.maximum(m, jnp.max(s, axis=0, keepdims=True) + fqs[n])
                alpha = jnp.exp(m - m_new)
                pm = jnp.exp(s + (fqs[n] - m_new))
                l = alpha * l + jnp.sum(pm, axis=0, keepdims=True)
                stats.append((m_new, l, alpha))
                probs.append(pm.astype(MXU_DTYPE))
            pvs = [jnp.dot(vt_ref[j, _pair_cols(hd), :], probs[n],
                           preferred_element_type=F32) for n, hd in enumerate(heads)]
            return tuple((stats[n][0], stats[n][1], stats[n][2] * carry[n][2] + pvs[n])
                         for n in range(len(heads)))

        init = tuple((jnp.full((1, tq), -jnp.inf, F32), jnp.zeros((1, tq), F32),
                      jnp.zeros((LANES, tq), F32)) for _ in heads)
        carry = lax.fori_loop(
            0, i, lambda j, cr, qk=qk, block=block: block(j, qk(j), cr, False), init)
        carry = block(i, qk(i), carry, True)
        _store_heads(o_ref, heads, [acc / l for _, l, acc in carry])


def _att_specs(t, seq):
    n_batch = t // seq
    tq = min(ATT_TILE, seq)
    nq = seq // tq
    q_spec = pl.BlockSpec((tq, ATT_WIDTH), lambda b, i: (b * nq + i, 0))
    k_spec = pl.BlockSpec((seq, ATT_WIDTH), lambda b, i: (b, 0))
    vt_spec = pl.BlockSpec((nq, ATT_WIDTH, tq), lambda b, i: (b, 0, 0))
    return n_batch, tq, nq, q_spec, k_spec, vt_spec


def _fox(q, k, vt, fcol, frow4, seq):
    t = q.shape[0]
    n_batch, tq, nq, q_spec, k_spec, vt_spec = _att_specs(t, seq)
    return pl.pallas_call(
        _fox_kernel,
        grid=(n_batch, nq),
        in_specs=[
            q_spec, k_spec, vt_spec,
            pl.BlockSpec((seq, LANES), lambda b, i: (b, 0)),
            pl.BlockSpec((None, N_HEADS, nq, tq), lambda b, i: (b, 0, 0, 0)),
        ],
        out_specs=q_spec,
        out_shape=jax.ShapeDtypeStruct((t, ATT_WIDTH), MXU_DTYPE),
        compiler_params=_params(2),
        name="fox_attention",
    )(q, k, vt, fcol, frow4)


def _sb_kernel(q_ref, k_ref, vt_ref, suf_ref, o_ref):
    i = pl.program_id(1)
    tq = q_ref.shape[0]
    key = lax.broadcasted_iota(jnp.int32, (tq, tq), 0)
    qry = lax.broadcasted_iota(jnp.int32, (tq, tq), 1)
    strict = key < qry
    suf = suf_ref[...]
    for g in range(N_HEADS // ATT_GROUP):
        heads = list(range(g * ATT_GROUP, (g + 1) * ATT_GROUP))
        qms = [_head_query(q_ref[:, _pair_cols(hd)], hd % 2) for hd in heads]

        def qk(j, heads=heads, qms=qms):
            start = pl.multiple_of(j * tq, tq)
            return tuple(_nt_dot(k_ref[pl.ds(start, tq), _pair_cols(hd)], qms[n])
                         for n, hd in enumerate(heads))

        def block(j, zs, carry, masked, heads=heads):
            log_betas, splits, laters = [], [], []
            for n in range(len(heads)):
                z = zs[n]
                sp = jnp.maximum(z, 0.0) + jnp.log(1.0 + jnp.exp2(jnp.abs(z) * -LOG2_E))
                log_betas.append(z - sp)
                if masked:
                    sp = jnp.where(strict, sp, 0.0)
                splits.append(sp.astype(MXU_DTYPE))
                laters.append(carry[n][0] + jnp.sum(sp, axis=0, keepdims=True))
            afters = [jnp.dot(suf, splits[n], preferred_element_type=F32)
                      for n in range(len(heads))]
            ws = []
            for n in range(len(heads)):
                w = jnp.exp(log_betas[n] - jnp.maximum(afters[n] + carry[n][0], 0.0))
                if masked:
                    w = jnp.where(strict, w, 0.0)
                ws.append(w.astype(MXU_DTYPE))
            return tuple(
                (laters[n], carry[n][1] + jnp.dot(vt_ref[j, _pair_cols(hd), :], ws[n],
                                                  preferred_element_type=F32))
                for n, hd in enumerate(heads))

        init = tuple((jnp.zeros((1, tq), F32), jnp.zeros((LANES, tq), F32)) for _ in heads)

        carry = block(i, qk(i), init, True)
        carry = lax.fori_loop(
            0, i, lambda s, cr, qk=qk, block=block: block(i - 1 - s, qk(i - 1 - s), cr, False),
            carry)
        _store_heads(o_ref, heads, [acc for _, acc in carry])


def _sb(q, k, vt, seq):
    t = q.shape[0]
    n_batch, tq, nq, q_spec, k_spec, vt_spec = _att_specs(t, seq)
    suf = (lax.broadcasted_iota(jnp.int32, (tq, tq), 1)
           > lax.broadcasted_iota(jnp.int32, (tq, tq), 0)).astype(MXU_DTYPE)
    return pl.pallas_call(
        _sb_kernel,
        grid=(n_batch, nq),
        in_specs=[q_spec, k_spec, vt_spec, pl.BlockSpec((tq, tq), lambda b, i: (0, 0))],
        out_specs=q_spec,
        out_shape=jax.ShapeDtypeStruct((t, ATT_WIDTH), MXU_DTYPE),
        compiler_params=_params(2),
        name="sb_attention",
    )(q, k, vt, suf)


def _lru_kernel(xc_ref, gc_ref, cw_ref, cb_ref, wa_ref, ba_ref, wx_ref, bx_ref, lam_ref,
                o_ref, ext_ref, a_ref, u_ref, h_ref, state_ref):
    s_idx = pl.program_id(1)
    ts = xc_ref.shape[0]
    halo = 8

    @pl.when(s_idx == 0)
    def _():
        ext_ref[0:halo, :] = jnp.zeros((halo, LRU_WIDTH), F32)
        state_ref[...] = jnp.zeros_like(state_ref)

    ext_ref[halo:halo + ts, :] = xc_ref[...]
    xconv = cb_ref[...]
    for j in range(CONV_WIDTH):
        lo = halo - (CONV_WIDTH - 1) + j
        xconv = xconv + cw_ref[j:j + 1, :] * ext_ref[lo:lo + ts, :]
    ext_ref[0:halo, :] = ext_ref[ts:ts + halo, :]

    xb = xconv.astype(MXU_DTYPE)
    r = jax.nn.sigmoid(jnp.dot(xb, wa_ref[...], preferred_element_type=F32) + ba_ref[...])
    ig = jax.nn.sigmoid(jnp.dot(xb, wx_ref[...], preferred_element_type=F32) + bx_ref[...])
    log_a = (-LRU_C * _softplus(-lam_ref[...])) * r
    a = jnp.exp(log_a)
    a_ref[...] = a
    u_ref[...] = jnp.sqrt(1.0 - a * a) * (ig * xconv)

    def step(t, h):
        h = a_ref[pl.ds(t, 1), :] * h + u_ref[pl.ds(t, 1), :]
        h_ref[pl.ds(t, 1), :] = h
        return h

    state_ref[0:1, :] = lax.fori_loop(0, ts, step, state_ref[0:1, :], unroll=8)
    o_ref[...] = (h_ref[...] * jax.nn.gelu(gc_ref[...])).astype(o_ref.dtype)


def _lru(xc, gc, conv_w, conv_b, wa_bd, ba, wx_bd, bx, lam, seq):
    t, w = xc.shape
    n_batch = t // seq
    ts = min(LRU_TILE, seq)
    ns = seq // ts
    row_spec = pl.BlockSpec((ts, w), lambda b, s: (b * ns + s, 0))
    vec_spec = pl.BlockSpec((1, w), lambda b, s: (0, 0))
    mat_spec = pl.BlockSpec((w, w), lambda b, s: (0, 0))
    return pl.pallas_call(
        _lru_kernel,
        grid=(n_batch, ns),
        in_specs=[row_spec, row_spec,
                  pl.BlockSpec((CONV_WIDTH, w), lambda b, s: (0, 0)), vec_spec,
                  mat_spec, vec_spec, mat_spec, vec_spec, vec_spec],
        out_specs=row_spec,
        out_shape=jax.ShapeDtypeStruct((t, w), MXU_DTYPE),
        scratch_shapes=[pltpu.VMEM((ts + 8, w), F32), pltpu.VMEM((ts, w), F32),
                        pltpu.VMEM((ts, w), F32), pltpu.VMEM((ts, w), F32),
                        pltpu.VMEM((8, w), F32)],
        compiler_params=_params(2),
        name="rg_lru",
    )(xc, gc, conv_w, conv_b, wa_bd, ba, wx_bd, bx, lam)


def _merge_kernel(x_ref, ada_ref, oa_ref, ob_ref, oc_ref, wg_ref, bg_ref, wpa_ref, wpb_ref,
                  wpc_ref, wo_ref, lng_ref, lnb_ref, wr_ref, br_ref,
                  x1_ref, h2_ref, logit_ref):
    d = x_ref.shape[1]
    x = x_ref[...]
    shift1, scale1, gate1 = ada_ref[0:1, :], ada_ref[1:2, :], ada_ref[2:3, :]
    shift2, scale2 = ada_ref[3:4, :], ada_ref[4:5, :]
    h = (x * (1.0 + scale1) + shift1).astype(MXU_DTYPE)
    merged = None
    for n, (o_ref, w_ref) in enumerate(((oa_ref, wpa_ref), (ob_ref, wpb_ref), (oc_ref, wpc_ref))):
        cs = slice(n * d, (n + 1) * d)
        g = jax.nn.sigmoid(jnp.dot(h, wg_ref[:, cs], preferred_element_type=F32) + bg_ref[:, cs])
        term = g * jnp.dot(o_ref[...], w_ref[...], preferred_element_type=F32)
        merged = term if merged is None else merged + term
    y = jnp.dot(merged.astype(MXU_DTYPE), wo_ref[...], preferred_element_type=F32)
    x1 = _layer_norm(DEEPNORM_ALPHA * x + (1.0 + gate1) * y, lng_ref[...], lnb_ref[...])
    x1_ref[...] = x1
    h2 = x1 * (1.0 + scale2) + shift2
    h2_ref[...] = _pack_halves(h2)
    logit_ref[...] = _nt_dot(wr_ref[...], h2.astype(MXU_DTYPE)) + br_ref[...]


def _merge(x2d, ada_l, o_a, o_b, o_c, w_gate, b_gate, w_pa, w_pb, w_pc, w_o, ln_g, ln_b,
           w_router_t, b_router, seq):
    t, d = x2d.shape
    tm = min(ROW_TILE, seq)
    tiles_per_seq = seq // tm
    row = lambda i: (i, 0)
    whole = lambda a: pl.BlockSpec(a.shape, lambda i: (0,) * a.ndim)
    return pl.pallas_call(
        _merge_kernel,
        grid=(t // tm,),
        in_specs=[
            pl.BlockSpec((tm, d), row),
            pl.BlockSpec((None, 6, d), lambda i: (i // tiles_per_seq, 0, 0)),
            pl.BlockSpec((tm, ATT_WIDTH), row), pl.BlockSpec((tm, ATT_WIDTH), row),
            pl.BlockSpec((tm, LRU_WIDTH), row),
            whole(w_gate), whole(b_gate), whole(w_pa), whole(w_pb), whole(w_pc), whole(w_o),
            whole(ln_g), whole(ln_b), whole(w_router_t), whole(b_router),
        ],
        out_specs=[pl.BlockSpec((tm, d), row), pl.BlockSpec((tm, d // 2), row),
                   pl.BlockSpec((N_EXPERTS, tm), lambda i: (0, i))],
        out_shape=[jax.ShapeDtypeStruct((t, d), F32),
                   jax.ShapeDtypeStruct((t, d // 2), jnp.uint32),
                   jax.ShapeDtypeStruct((N_EXPERTS, t), F32)],
        compiler_params=_params(1),
        name="merge_outproj_ln",
    )(x2d, ada_l, o_a, o_b, o_c, w_gate, b_gate, w_pa, w_pb, w_pc, w_o, ln_g, ln_b,
      w_router_t, b_router)


def _route_kernel(logit_ref, tri_ref, eid_ref, gate_ref, pos_ref, cnt_ref, carry_ref):
    i = pl.program_id(0)
    tr = logit_ref.shape[1]

    @pl.when(i == 0)
    def _():
        carry_ref[...] = jnp.zeros_like(carry_ref)

    erow = lax.broadcasted_iota(jnp.int32, (N_EXPERTS, tr), 0)
    cur = logit_ref[...]
    vals, ids = [], []
    for _ in range(TOP_K):
        m = jnp.max(cur, axis=0, keepdims=True)
        idx = jnp.min(jnp.where(cur == m, erow, N_EXPERTS), axis=0, keepdims=True)
        vals.append(m)
        ids.append(idx)
        cur = jnp.where(erow == idx, -jnp.inf, cur)
    exps = [jnp.exp(v - vals[0]) for v in vals]
    denom = exps[0] + exps[1] + exps[2] + exps[3]
    chosen = jnp.zeros((N_EXPERTS, tr), F32)
    for idx in ids:
        chosen = chosen + jnp.where(erow == idx, 1.0, 0.0)
    prefix = (jnp.dot(chosen.astype(MXU_DTYPE), tri_ref[...], preferred_element_type=F32)
              + carry_ref[:, 0:1])
    for k in range(TOP_K):
        eid_ref[k:k + 1, :] = ids[k]
        gate_ref[k:k + 1, :] = exps[k] / denom
        pos_ref[k:k + 1, :] = jnp.sum(jnp.where(erow == ids[k], prefix, 0.0), axis=0,
                                      keepdims=True).astype(jnp.int32)
    carry_ref[...] = carry_ref[...] + jnp.sum(chosen, axis=1, keepdims=True)
    cnt_ref[...] = carry_ref[...].astype(jnp.int32)


def _route(logits_t):
    n_e, t = logits_t.shape
    tr = min(ROUTE_TILE, t)
    tri = (lax.broadcasted_iota(jnp.int32, (tr, tr), 0)
           < lax.broadcasted_iota(jnp.int32, (tr, tr), 1)).astype(MXU_DTYPE)
    tok = lambda i: (0, i)
    return pl.pallas_call(
        _route_kernel,
        grid=(t // tr,),
        in_specs=[pl.BlockSpec((n_e, tr), tok), pl.BlockSpec((tr, tr), lambda i: (0, 0))],
        out_specs=[pl.BlockSpec((TOP_K, tr), tok)] * 3
        + [pl.BlockSpec((n_e, LANES), lambda i: (0, 0))],
        out_shape=[jax.ShapeDtypeStruct((TOP_K, t), jnp.int32),
                   jax.ShapeDtypeStruct((TOP_K, t), F32),
                   jax.ShapeDtypeStruct((TOP_K, t), jnp.int32),
                   jax.ShapeDtypeStruct((n_e, LANES), jnp.int32)],
        scratch_shapes=[pltpu.VMEM((n_e, LANES), F32)],
        compiler_params=_params(1),
        name="route_topk",
    )(logits_t, tri)


def _dest_kernel(start_ref, eid_ref, pos_ref, dest_ref):
    eid = eid_ref[...]
    dest = pos_ref[...]
    for e in range(N_EXPERTS):
        dest = jnp.where(eid == e, dest + start_ref[e], dest)
    dest_ref[...] = dest


def _dest(pad_start, eid, pos):
    k, t = eid.shape
    tr = min(ROUTE_TILE, t)
    spec = pl.BlockSpec((k, tr), lambda i, s: (0, i))
    return pl.pallas_call(
        _dest_kernel,
        grid_spec=pltpu.PrefetchScalarGridSpec(
            num_scalar_prefetch=1, grid=(t // tr,), in_specs=[spec, spec], out_specs=spec),
        out_shape=jax.ShapeDtypeStruct((k, t), jnp.int32),
        compiler_params=_params(1),
        name="route_dest",
    )(pad_start, eid, pos)


def _sc_mesh():
    return plsc.VectorSubcoreMesh(core_axis_name="c", subcore_axis_name="s")


def _sc_worker():
    return lax.axis_index("s") * SC_CORES + lax.axis_index("c")


def _sc_scatter_rows(rows, idx, n_out):
    t, width = rows.shape
    per_worker = t // (SC_CORES * SC_SUBCORES)
    n_chunks = per_worker // SC_CHUNK

    @functools.partial(
        pl.kernel, mesh=_sc_mesh(),
        out_type=jax.ShapeDtypeStruct((n_out, width), rows.dtype),
        scratch_types=[pltpu.VMEM((SC_CHUNK,), jnp.int32),
                       pltpu.VMEM((SC_CHUNK, width), rows.dtype),
                       pltpu.SemaphoreType.DMA],
    )
    def scatter(rows_hbm, idx_hbm, out_hbm, idx_v, rows_v, sem):
        base = _sc_worker() * per_worker

        @pl.loop(0, n_chunks)
        def _(g):
            off = base + g * SC_CHUNK
            pltpu.sync_copy(rows_hbm.at[pl.ds(off, SC_CHUNK)], rows_v)
            for k in range(TOP_K):
                pltpu.sync_copy(idx_hbm.at[pl.ds(k * t + off, SC_CHUNK)], idx_v)
                pltpu.async_copy(rows_v, out_hbm.at[idx_v], sem).wait()

    return scatter(rows, idx)


def _sc_gather_rows(table, idx):
    n_rows = idx.shape[0]
    width = table.shape[1]
    per_worker = n_rows // (SC_CORES * SC_SUBCORES)
    n_chunks = per_worker // SC_CHUNK

    @functools.partial(
        pl.kernel, mesh=_sc_mesh(),
        out_type=jax.ShapeDtypeStruct((n_rows, width), table.dtype),
        scratch_types=[pltpu.VMEM((SC_CHUNK,), jnp.int32),
                       pltpu.VMEM((SC_CHUNK, width), table.dtype),
                       pltpu.SemaphoreType.DMA],
    )
    def gather(table_hbm, idx_hbm, out_hbm, idx_v, rows_v, sem):
        base = _sc_worker() * per_worker

        @pl.loop(0, n_chunks)
        def _(g):
            off = base + g * SC_CHUNK
            pltpu.sync_copy(idx_hbm.at[pl.ds(off, SC_CHUNK)], idx_v)
            pltpu.async_copy(table_hbm.at[idx_v], rows_v, sem).wait()
            pltpu.sync_copy(rows_v, out_hbm.at[pl.ds(off, SC_CHUNK)])

    return gather(table, idx)


def _expert_kernel(be_ref, valid_ref, x_ref, wgu_ref, bgu_ref, wd_ref, bd_ref, y_ref,
                   wgu_lp, wd_lp):
    i = pl.program_id(0)
    d_e = wd_ref.shape[0]
    n_valid = valid_ref[i]

    @pl.when((i == 0) | (be_ref[i] != be_ref[jnp.maximum(i - 1, 0)]))
    def _():
        wgu_lp[...] = wgu_ref[...].astype(MXU_DTYPE)
        wd_lp[...] = wd_ref[...].astype(MXU_DTYPE)

    @pl.when(n_valid > 0)
    def _():
        row = lax.broadcasted_iota(jnp.int32, x_ref.shape, 0)
        lo, hi = _unpack_halves(jnp.where(row < n_valid, x_ref[...], jnp.uint32(0)))
        x = jnp.concatenate([lo, hi], axis=1).astype(MXU_DTYPE)
        gu = jnp.dot(x, wgu_lp[...], preferred_element_type=F32) + bgu_ref[...]
        gate = jnp.minimum(gu[:, :d_e], SWIGLU_LIMIT)
        up = jnp.clip(gu[:, d_e:], -SWIGLU_LIMIT, SWIGLU_LIMIT)
        act = (up + 1.0) * (gate * jax.nn.sigmoid(SWIGLU_ALPHA * gate))
        y = jnp.dot(act.astype(MXU_DTYPE), wd_lp[...], preferred_element_type=F32) + bd_ref[...]
        y_ref[...] = _pack_halves(y)

    @pl.when(n_valid == 0)
    def _():
        y_ref[...] = jnp.zeros_like(y_ref)


def _experts(layer, block_e, valid, x_buf, w_gu, b_gu, w_down, b_down):
    n_slots, half = x_buf.shape
    _, n_e, d, d2 = w_gu.shape
    d_e = w_down.shape[2]
    n_blocks = n_slots // EXPERT_BLOCK
    n_layers = w_gu.shape[0]
    return pl.pallas_call(
        _expert_kernel,
        grid_spec=pltpu.PrefetchScalarGridSpec(
            num_scalar_prefetch=2,
            grid=(n_blocks,),
            in_specs=[
                pl.BlockSpec((EXPERT_BLOCK, half), lambda i, be, nu: (i, 0)),
                pl.BlockSpec((None, None, d, d2), lambda i, be, nu: (layer, be[i], 0, 0)),
                pl.BlockSpec((None, None, 1, d2), lambda i, be, nu: (layer, be[i], 0, 0)),
                pl.BlockSpec((None, None, d_e, d), lambda i, be, nu: (layer, be[i], 0, 0)),
                pl.BlockSpec((None, None, 1, d), lambda i, be, nu: (layer, be[i], 0, 0)),
            ],
            out_specs=pl.BlockSpec((EXPERT_BLOCK, half), lambda i, be, nu: (i, 0)),
            scratch_shapes=[pltpu.VMEM((d, d2), MXU_DTYPE), pltpu.VMEM((d_e, d), MXU_DTYPE)],
        ),
        out_shape=jax.ShapeDtypeStruct((n_slots, half), jnp.uint32),
        compiler_params=_params(1),
        name="moe_experts",
    )(block_e, valid, x_buf, w_gu, b_gu.reshape(n_layers, n_e, 1, d2), w_down,
      b_down.reshape(n_layers, n_e, 1, d))


def _combine_kernel(x1_ref, ada_ref, gate_ref, lng_ref, lnb_ref, *rest):
    y_refs, o_ref = rest[:TOP_K], rest[TOP_K]
    gates = gate_ref[...]
    y_lo = y_hi = None
    for k in range(TOP_K):
        lo, hi = _unpack_halves(y_refs[k][...])
        g = gates[:, k:k + 1]
        y_lo = lo * g if y_lo is None else y_lo + lo * g
        y_hi = hi * g if y_hi is None else y_hi + hi * g
    y = jnp.concatenate([y_lo, y_hi], axis=1)
    gate2 = ada_ref[5:6, :]
    o_ref[...] = _layer_norm(DEEPNORM_ALPHA * x1_ref[...] + (1.0 + gate2) * y,
                             lng_ref[...], lnb_ref[...])


def _combine(x1, ada_l, gates_tk, ln_g, ln_b, y_rows, seq):
    t, d = x1.shape
    tm = min(MOVE_TILE, seq)
    tiles_per_seq = seq // tm
    return pl.pallas_call(
        _combine_kernel,
        grid=(t // tm,),
        in_specs=[
            pl.BlockSpec((tm, d), lambda i: (i, 0)),
            pl.BlockSpec((None, 6, d), lambda i: (i // tiles_per_seq, 0, 0)),
            pl.BlockSpec((tm, TOP_K), lambda i: (i, 0)),
            pl.BlockSpec((1, d), lambda i: (0, 0)),
            pl.BlockSpec((1, d), lambda i: (0, 0)),
        ] + [
            pl.BlockSpec((tm, d // 2), lambda i, k=k: (k * (t // tm) + i, 0))
            for k in range(TOP_K)
        ],
        out_specs=pl.BlockSpec((tm, d), lambda i: (i, 0)),
        out_shape=jax.ShapeDtypeStruct((t, d), F32),
        compiler_params=_params(1),
        name="moe_combine_ln",
    )(x1, ada_l, gates_tk, ln_g, ln_b, *([y_rows] * TOP_K))


def _split_w_in(w_in):
    d = w_in.shape[0]
    sizes = (ATT_WIDTH, ATT_WIDTH, ATT_WIDTH, N_HEADS, ATT_WIDTH, ATT_WIDTH, ATT_WIDTH,
             LRU_WIDTH, LRU_WIDTH)
    bounds = [sum(sizes[:n]) for n in range(len(sizes) + 1)]
    qa, ka, va, fa, qb, kb, vb, xc, gc = (
        w_in[:, bounds[n]:bounds[n + 1]] for n in range(len(sizes)))
    forget = jnp.concatenate([fa, jnp.zeros((d, LANES - N_HEADS), w_in.dtype)], axis=1)
    w_main = jnp.concatenate([qa, ka, qb, kb, xc, gc, forget], axis=1).astype(MXU_DTYPE)
    w_vt = jnp.concatenate([va, vb], axis=1).T.astype(MXU_DTYPE)
    return w_main, w_vt


def _block_diag(w):
    n, c, dd = w.shape
    eye = jnp.eye(n, dtype=w.dtype)
    return (eye[:, None, :, None] * w[:, :, None, :]).reshape(n * c, n * dd).astype(MXU_DTYPE)


def kernel(x, c, w_ada, b_ada, ln1_g, ln1_b, w_in, b_f, conv_w, conv_b, lru_wa, lru_ba, lru_wx,
           lru_bx, lru_lambda, w_gate, b_gate, w_pa, w_pb, w_pc, w_o, ln2_g, ln2_b, w_router,
           b_router, w_gu, b_gu, w_down, b_down):
    n_batch, seq, d = x.shape
    t = n_batch * seq
    n_layers = w_ada.shape[0]
    n_blocks = (t * TOP_K) // EXPERT_BLOCK + N_EXPERTS
    n_slots = n_blocks * EXPERT_BLOCK
    att_tile = min(ATT_TILE, seq)
    vec = lambda a: a.reshape(1, -1)

    ada = _ada(c, w_ada, b_ada).reshape(n_layers, n_batch, 6, d)
    x2d = x.reshape(t, d)
    for l in range(n_layers):
        ada_l = ada[l]
        bf_pad = jnp.concatenate([b_f[l], jnp.zeros((LANES - N_HEADS,), F32)]).reshape(1, LANES)
        w_main, w_vt = _split_w_in(w_in[l])
        qa, ka, qb, kb, xc, gc, vat, vbt, fcol, frow = _inproj(
            x2d, ada_l, w_main, w_vt, bf_pad, seq)
        frow4 = frow.reshape(n_batch, N_HEADS, seq // att_tile, att_tile)
        o_a = _fox(qa, ka, vat, fcol, frow4, seq)
        o_b = _sb(qb, kb, vbt, seq)
        o_c = _lru(xc, gc, conv_w[l], vec(conv_b[l]), _block_diag(lru_wa[l]), vec(lru_ba[l]),
                   _block_diag(lru_wx[l]), vec(lru_bx[l]), vec(lru_lambda[l]), seq)
        x1, h2, logits_t = _merge(
            x2d, ada_l, o_a, o_b, o_c, w_gate[l].astype(MXU_DTYPE), vec(b_gate[l]),
            w_pa[l].astype(MXU_DTYPE), w_pb[l].astype(MXU_DTYPE), w_pc[l].astype(MXU_DTYPE),
            w_o[l].astype(MXU_DTYPE), vec(ln1_g[l]), vec(ln1_b[l]),
            w_router[l].T.astype(MXU_DTYPE), b_router[l].reshape(N_EXPERTS, 1), seq)

        eid, gates, pos, cnt = _route(logits_t)
        counts = cnt[:, 0]
        padded = (counts + EXPERT_BLOCK - 1) // EXPERT_BLOCK * EXPERT_BLOCK
        pad_end = jnp.cumsum(padded)
        pad_start = (pad_end - padded).astype(jnp.int32)
        block_first = (jnp.arange(n_blocks) * EXPERT_BLOCK)[:, None]
        block_e = jnp.minimum(jnp.sum(pad_end[None, :] <= block_first, axis=1),
                              N_EXPERTS - 1).astype(jnp.int32)
        valid = jnp.clip(counts[block_e] + pad_start[block_e] - block_first[:, 0],
                         0, EXPERT_BLOCK).astype(jnp.int32)
        dest = _dest(pad_start, eid, pos).reshape(-1)

        x_buf = _sc_scatter_rows(h2, dest, n_slots)
        y_buf = _experts(l, block_e, valid, x_buf, w_gu, b_gu, w_down, b_down)
        y_rows = _sc_gather_rows(y_buf, dest)
        x2d = _combine(x1, ada_l, gates.T, vec(ln2_g[l]), vec(ln2_b[l]), y_rows, seq)
    return x2d.reshape(n_batch, seq, d)
```

```python
import functools

import jax
import jax.numpy as jnp
from jax import lax
from jax.experimental import pallas as pl
from jax.experimental.pallas import tpu as pltpu
from jax.experimental.pallas import tpu_sc as plsc

D_MODEL = 1024
DEPTH = 2
HEAD_DIM = 64
N_HEADS = 8
ATT_WIDTH = N_HEADS * HEAD_DIM
LRU_WIDTH = D_MODEL
LRU_BLOCKS = 16
CONV_WIDTH = 4
LRU_C = 8.0
N_EXPERTS = 32
TOP_K = 4
SWIGLU_LIMIT = 7.0
SWIGLU_ALPHA = 1.702
LN_EPS = 1e-5
DEEPNORM_ALPHA = (2.0 * DEPTH) ** 0.25
ATT_SCALE = HEAD_DIM ** -0.5
LOG2_E = 1.4426950408889634

LANES = 128
MXU_DTYPE = jnp.bfloat16
F32 = jnp.float32

ROW_TILE = 512
ATT_TILE = 256
ATT_GROUP = 8
LRU_TILE = 256
ROUTE_TILE = 512
EXPERT_BLOCK = 256
MOVE_TILE = 256
COMBINE_PARTS = 2
SC_CORES = 2
SC_SUBCORES = 16
SC_CHUNK = 64
VMEM_LIMIT = 56 * 1024 * 1024


def _params(n_axes, vmem=VMEM_LIMIT):
    return pltpu.CompilerParams(
        dimension_semantics=("arbitrary",) * n_axes, vmem_limit_bytes=vmem)


def _log_sigmoid(x):
    return jnp.minimum(x, 0.0) - jnp.log1p(jnp.exp(-jnp.abs(x)))


def _softplus(x):
    return jnp.maximum(x, 0.0) + jnp.log1p(jnp.exp(-jnp.abs(x)))


def _layer_norm(v, g, b):
    mu = jnp.mean(v, axis=-1, keepdims=True)
    d = v - mu
    var = jnp.mean(d * d, axis=-1, keepdims=True)
    return d * lax.rsqrt(var + LN_EPS) * g + b


def _pack_halves(x):
    n = x.shape[1] // 2
    bits = lambda v: lax.bitcast_convert_type(v.astype(jnp.bfloat16).astype(F32), jnp.uint32)
    return (bits(x[:, :n]) >> 16) | bits(x[:, n:])


def _unpack_halves(p):
    lo = lax.bitcast_convert_type(p << 16, F32)
    hi = lax.bitcast_convert_type(p & jnp.uint32(0xFFFF0000), F32)
    return lo, hi


def _split3(x):
    hi = x.astype(MXU_DTYPE)
    r1 = x - hi.astype(F32)
    mid = r1.astype(MXU_DTYPE)
    lo = (r1 - mid.astype(F32)).astype(MXU_DTYPE)
    return hi, mid, lo


def _ada_kernel(c_ref, w_ref, b_ref, o_ref):
    c = c_ref[...]
    cond = c * jax.nn.sigmoid(c)
    o_ref[...] = jnp.dot(cond.astype(MXU_DTYPE), w_ref[...].astype(MXU_DTYPE),
                         preferred_element_type=F32) + b_ref[...]


def _ada(c, w_ada, b_ada):
    n_layers, d, n6 = w_ada.shape
    b = c.shape[0]
    tn = n6 // 4
    return pl.pallas_call(
        _ada_kernel,
        grid=(n_layers, n6 // tn),
        in_specs=[
            pl.BlockSpec((b, d), lambda l, j: (0, 0)),
            pl.BlockSpec((None, d, tn), lambda l, j: (l, 0, j)),
            pl.BlockSpec((None, 1, tn), lambda l, j: (l, 0, j)),
        ],
        out_specs=pl.BlockSpec((None, b, tn), lambda l, j: (l, 0, j)),
        out_shape=jax.ShapeDtypeStruct((n_layers, b, n6), F32),
        compiler_params=_params(2),
        name="ada",
    )(c, w_ada, b_ada.reshape(n_layers, 1, n6))


def _nt_dot(a, b):
    return lax.dot_general(a, b, (((1,), (1,)), ((), ())), preferred_element_type=F32)


def _inproj_kernel(x_ref, ada_ref, w_ref, wvt_ref, bf_ref, tri_ref,
                   qa_ref, ka_ref, qb_ref, kb_ref, xc_ref, gc_ref, vat_ref, vbt_ref,
                   fcol_ref, frow_ref, carry_ref, *, tiles_per_seq):
    i = pl.program_id(0)
    tm = x_ref.shape[0]
    shift = ada_ref[0:1, :]
    scale = ada_ref[1:2, :]
    h = (x_ref[...] * (1.0 + scale) + shift).astype(MXU_DTYPE)
    off = 0
    for ref in (qa_ref, ka_ref, qb_ref, kb_ref, xc_ref, gc_ref):
        width = ref.shape[1]
        ref[...] = jnp.dot(h, w_ref[:, off:off + width],
                           preferred_element_type=F32).astype(ref.dtype)
        off += width
    vt = _nt_dot(wvt_ref[...], h).astype(MXU_DTYPE)
    chunk = vat_ref.shape[2]
    for n in range(tm // chunk):
        cols = slice(n * chunk, (n + 1) * chunk)
        vat_ref[n] = vt[0:ATT_WIDTH, cols]
        vbt_ref[n] = vt[ATT_WIDTH:2 * ATT_WIDTH, cols]
    zf = jnp.dot(h, w_ref[:, off:off + LANES], preferred_element_type=F32) + bf_ref[...]
    logf = _log_sigmoid(zf)

    @pl.when(i % tiles_per_seq == 0)
    def _():
        carry_ref[...] = jnp.zeros_like(carry_ref)

    tri = tri_ref[...]
    f_cum = carry_ref[0:1, :]
    for piece in _split3(logf):
        f_cum = f_cum + jnp.dot(tri, piece, preferred_element_type=F32)
    carry_ref[0:1, :] = f_cum[tm - 1:tm, :]
    fcol_ref[...] = f_cum
    frow_ref[...] = f_cum.T[0:N_HEADS, :]


def _inproj(x2d, ada_l, w_main, w_vt, bf_pad, seq):
    t, d = x2d.shape
    tm = min(ROW_TILE, seq)
    chunk = min(ATT_TILE, seq)
    tiles_per_seq = seq // tm
    n_batch = t // seq
    tri = (lax.broadcasted_iota(jnp.int32, (tm, tm), 1)
           <= lax.broadcasted_iota(jnp.int32, (tm, tm), 0)).astype(MXU_DTYPE)
    row = lambda i: (i, 0)
    att = jax.ShapeDtypeStruct((t, ATT_WIDTH), MXU_DTYPE)
    att_t = jax.ShapeDtypeStruct((t // chunk, ATT_WIDTH, chunk), MXU_DTYPE)
    wide = jax.ShapeDtypeStruct((t, LRU_WIDTH), F32)
    att_spec = pl.BlockSpec((tm, ATT_WIDTH), row)
    att_t_spec = pl.BlockSpec((tm // chunk, ATT_WIDTH, chunk), lambda i: (i, 0, 0))
    wide_spec = pl.BlockSpec((tm, LRU_WIDTH), row)
    return pl.pallas_call(
        functools.partial(_inproj_kernel, tiles_per_seq=tiles_per_seq),
        grid=(t // tm,),
        in_specs=[
            pl.BlockSpec((tm, d), row),
            pl.BlockSpec((None, 6, d), lambda i: (i // tiles_per_seq, 0, 0)),
            pl.BlockSpec(w_main.shape, lambda i: (0, 0)),
            pl.BlockSpec(w_vt.shape, lambda i: (0, 0)),
            pl.BlockSpec((1, LANES), lambda i: (0, 0)),
            pl.BlockSpec((tm, tm), lambda i: (0, 0)),
        ],
        out_specs=[att_spec] * 4 + [wide_spec] * 2 + [att_t_spec] * 2 + [
            pl.BlockSpec((tm, LANES), row),
            pl.BlockSpec((None, N_HEADS, tm),
                         lambda i: (i // tiles_per_seq, 0, i % tiles_per_seq)),
        ],
        out_shape=[att] * 4 + [wide] * 2 + [att_t] * 2 + [
            jax.ShapeDtypeStruct((t, LANES), F32),
            jax.ShapeDtypeStruct((n_batch, N_HEADS, seq), F32),
        ],
        scratch_shapes=[pltpu.VMEM((8, LANES), F32)],
        compiler_params=_params(1),
        name="inproj",
    )(x2d, ada_l, w_main, w_vt, bf_pad, tri)


def _head_query(q2, c):
    lane = lax.broadcasted_iota(jnp.int32, (1, LANES), 1)
    in_head = (lane >= c * HEAD_DIM) & (lane < (c + 1) * HEAD_DIM)
    return jnp.where(in_head, q2, jnp.zeros_like(q2)) * ATT_SCALE


def _pair_cols(head):
    return slice((head // 2) * LANES, (head // 2 + 1) * LANES)


def _store_heads(o_ref, heads, outs_t):
    sub = lax.broadcasted_iota(jnp.int32, (LANES, 1), 0)
    for n in range(0, len(heads), 2):
        pair_t = jnp.where(sub < HEAD_DIM, outs_t[n], outs_t[n + 1])
        o_ref[:, _pair_cols(heads[n])] = pair_t.T.astype(o_ref.dtype)


def _fox_kernel(q_ref, k_ref, vt_ref, fcol_ref, frow_ref, o_ref):
    i = pl.program_id(1)
    tq = q_ref.shape[0]
    key = lax.broadcasted_iota(jnp.int32, (tq, tq), 0)
    qry = lax.broadcasted_iota(jnp.int32, (tq, tq), 1)
    causal = key <= qry
    for g in range(N_HEADS // ATT_GROUP):
        heads = list(range(g * ATT_GROUP, (g + 1) * ATT_GROUP))
        qms = [_head_query(q_ref[:, _pair_cols(hd)], hd % 2) for hd in heads]
        fqs = [frow_ref[hd, pl.ds(i, 1), :] for hd in heads]

        def qk(j, heads=heads, qms=qms):
            start = pl.multiple_of(j * tq, tq)
            return tuple(_nt_dot(k_ref[pl.ds(start, tq), _pair_cols(hd)], qms[n])
                         for n, hd in enumerate(heads))

        def block(j, scores, carry, masked, heads=heads, fqs=fqs):
            start = pl.multiple_of(j * tq, tq)
            stats, probs = [], []
            for n, hd in enumerate(heads):
                m, l, _ = carry[n]
                fk = fcol_ref[pl.ds(start, tq), hd:hd + 1]
                s = scores[n] - fk
                if masked:
                    s = jnp.where(causal, s, -jnp.inf)
                m_new = jnp.maximum(m, jnp.max(s, axis=0, keepdims=True) + fqs[n])
                alpha = jnp.exp(m - m_new)
                pm = jnp.exp(s + (fqs[n] - m_new))
                l = alpha * l + jnp.sum(pm, axis=0, keepdims=True)
                stats.append((m_new, l, alpha))
                probs.append(pm.astype(MXU_DTYPE))
            pvs = [jnp.dot(vt_ref[j, _pair_cols(hd), :], probs[n],
                           preferred_element_type=F32) for n, hd in enumerate(heads)]
            return tuple((stats[n][0], stats[n][1], stats[n][2] * carry[n][2] + pvs[n])
                         for n in range(len(heads)))

        init = tuple((jnp.full((1, tq), -jnp.inf, F32), jnp.zeros((1, tq), F32),
                      jnp.zeros((LANES, tq), F32)) for _ in heads)
        carry = lax.fori_loop(
            0, i, lambda j, cr, qk=qk, block=block: block(j, qk(j), cr, False), init)
        carry = block(i, qk(i), carry, True)
        _store_heads(o_ref, heads, [acc / l for _, l, acc in carry])


def _att_specs(t, seq):
    n_batch = t // seq
    tq = min(ATT_TILE, seq)
    nq = seq // tq
    q_spec = pl.BlockSpec((tq, ATT_WIDTH), lambda b, i: (b * nq + i, 0))
    k_spec = pl.BlockSpec((seq, ATT_WIDTH), lambda b, i: (b, 0))
    vt_spec = pl.BlockSpec((nq, ATT_WIDTH, tq), lambda b, i: (b, 0, 0))
    return n_batch, tq, nq, q_spec, k_spec, vt_spec


def _fox(q, k, vt, fcol, frow4, seq):
    t = q.shape[0]
    n_batch, tq, nq, q_spec, k_spec, vt_spec = _att_specs(t, seq)
    return pl.pallas_call(
        _fox_kernel,
        grid=(n_batch, nq),
        in_specs=[
            q_spec, k_spec, vt_spec,
            pl.BlockSpec((seq, LANES), lambda b, i: (b, 0)),
            pl.BlockSpec((None, N_HEADS, nq, tq), lambda b, i: (b, 0, 0, 0)),
        ],
        out_specs=q_spec,
        out_shape=jax.ShapeDtypeStruct((t, ATT_WIDTH), MXU_DTYPE),
        compiler_params=_params(2),
        name="fox_attention",
    )(q, k, vt, fcol, frow4)


def _sb_kernel(q_ref, k_ref, vt_ref, suf_ref, o_ref):
    i = pl.program_id(1)
    tq = q_ref.shape[0]
    key = lax.broadcasted_iota(jnp.int32, (tq, tq), 0)
    qry = lax.broadcasted_iota(jnp.int32, (tq, tq), 1)
    strict = key < qry
    suf = suf_ref[...]
    for g in range(N_HEADS // ATT_GROUP):
        heads = list(range(g * ATT_GROUP, (g + 1) * ATT_GROUP))
        qms = [_head_query(q_ref[:, _pair_cols(hd)], hd % 2) for hd in heads]

        def qk(j, heads=heads, qms=qms):
            start = pl.multiple_of(j * tq, tq)
            return tuple(_nt_dot(k_ref[pl.ds(start, tq), _pair_cols(hd)], qms[n])
                         for n, hd in enumerate(heads))

        def block(j, zs, carry, masked, heads=heads):
            log_betas, splits, laters = [], [], []
            for n in range(len(heads)):
                z = zs[n]
                sp = jnp.maximum(z, 0.0) + jnp.log(1.0 + jnp.exp2(jnp.abs(z) * -LOG2_E))
                log_betas.append(z - sp)
                if masked:
                    sp = jnp.where(strict, sp, 0.0)
                splits.append(sp.astype(MXU_DTYPE))
                laters.append(carry[n][0] + jnp.sum(sp, axis=0, keepdims=True))
            afters = [jnp.dot(suf, splits[n], preferred_element_type=F32)
                      for n in range(len(heads))]
            ws = []
            for n in range(len(heads)):
                w = jnp.exp(log_betas[n] - jnp.maximum(afters[n] + carry[n][0], 0.0))
                if masked:
                    w = jnp.where(strict, w, 0.0)
                ws.append(w.astype(MXU_DTYPE))
            return tuple(
                (laters[n], carry[n][1] + jnp.dot(vt_ref[j, _pair_cols(hd), :], ws[n],
                                                  preferred_element_type=F32))
                for n, hd in enumerate(heads))

        init = tuple((jnp.zeros((1, tq), F32), jnp.zeros((LANES, tq), F32)) for _ in heads)

        carry = block(i, qk(i), init, True)
        carry = lax.fori_loop(
            0, i, lambda s, cr, qk=qk, block=block: block(i - 1 - s, qk(i - 1 - s), cr, False),
            carry)
        _store_heads(o_ref, heads, [acc for _, acc in carry])


def _sb(q, k, vt, seq):
    t = q.shape[0]
    n_batch, tq, nq, q_spec, k_spec, vt_spec = _att_specs(t, seq)
    suf = (lax.broadcasted_iota(jnp.int32, (tq, tq), 1)
           > lax.broadcasted_iota(jnp.int32, (tq, tq), 0)).astype(MXU_DTYPE)
    return pl.pallas_call(
        _sb_kernel,
        grid=(n_batch, nq),
        in_specs=[q_spec, k_spec, vt_spec, pl.BlockSpec((tq, tq), lambda b, i: (0, 0))],
        out_specs=q_spec,
        out_shape=jax.ShapeDtypeStruct((t, ATT_WIDTH), MXU_DTYPE),
        compiler_params=_params(2),
        name="sb_attention",
    )(q, k, vt, suf)


def _lru_kernel(xc_ref, gc_ref, cw_ref, cb_ref, wa_ref, ba_ref, wx_ref, bx_ref, lam_ref,
                o_ref, ext_ref, a_ref, u_ref, h_ref, state_ref):
    s_idx = pl.program_id(1)
    ts = xc_ref.shape[0]
    halo = 8

    @pl.when(s_idx == 0)
    def _():
        ext_ref[0:halo, :] = jnp.zeros((halo, LRU_WIDTH), F32)
        state_ref[...] = jnp.zeros_like(state_ref)

    ext_ref[halo:halo + ts, :] = xc_ref[...]
    xconv = cb_ref[...]
    for j in range(CONV_WIDTH):
        lo = halo - (CONV_WIDTH - 1) + j
        xconv = xconv + cw_ref[j:j + 1, :] * ext_ref[lo:lo + ts, :]
    ext_ref[0:halo, :] = ext_ref[ts:ts + halo, :]

    xb = xconv.astype(MXU_DTYPE)
    r = jax.nn.sigmoid(jnp.dot(xb, wa_ref[...], preferred_element_type=F32) + ba_ref[...])
    ig = jax.nn.sigmoid(jnp.dot(xb, wx_ref[...], preferred_element_type=F32) + bx_ref[...])
    log_a = (-LRU_C * _softplus(-lam_ref[...])) * r
    a = jnp.exp(log_a)
    a_ref[...] = a
    u_ref[...] = jnp.sqrt(1.0 - a * a) * (ig * xconv)

    def step(t, h):
        h = a_ref[pl.ds(t, 1), :] * h + u_ref[pl.ds(t, 1), :]
        h_ref[pl.ds(t, 1), :] = h
        return h

    state_ref[0:1, :] = lax.fori_loop(0, ts, step, state_ref[0:1, :], unroll=8)
    o_ref[...] = (h_ref[...] * jax.nn.gelu(gc_ref[...])).astype(o_ref.dtype)


def _lru(xc, gc, conv_w, conv_b, wa_bd, ba, wx_bd, bx, lam, seq):
    t, w = xc.shape
    n_batch = t // seq
    ts = min(LRU_TILE, seq)
    ns = seq // ts
    row_spec = pl.BlockSpec((ts, w), lambda b, s: (b * ns + s, 0))
    vec_spec = pl.BlockSpec((1, w), lambda b, s: (0, 0))
    mat_spec = pl.BlockSpec((w, w), lambda b, s: (0, 0))
    return pl.pallas_call(
        _lru_kernel,
        grid=(n_batch, ns),
        in_specs=[row_spec, row_spec,
                  pl.BlockSpec((CONV_WIDTH, w), lambda b, s: (0, 0)), vec_spec,
                  mat_spec, vec_spec, mat_spec, vec_spec, vec_spec],
        out_specs=row_spec,
        out_shape=jax.ShapeDtypeStruct((t, w), MXU_DTYPE),
        scratch_shapes=[pltpu.VMEM((ts + 8, w), F32), pltpu.VMEM((ts, w), F32),
                        pltpu.VMEM((ts, w), F32), pltpu.VMEM((ts, w), F32),
                        pltpu.VMEM((8, w), F32)],
        compiler_params=_params(2),
        name="rg_lru",
    )(xc, gc, conv_w, conv_b, wa_bd, ba, wx_bd, bx, lam)


def _merge_kernel(x_ref, ada_ref, oa_ref, ob_ref, oc_ref, wg_ref, bg_ref, wpa_ref, wpb_ref,
                  wpc_ref, wo_ref, lng_ref, lnb_ref, wr_ref, br_ref,
                  x1_ref, h2_ref, logit_ref):
    d = x_ref.shape[1]
    x = x_ref[...]
    shift1, scale1, gate1 = ada_ref[0:1, :], ada_ref[1:2, :], ada_ref[2:3, :]
    shift2, scale2 = ada_ref[3:4, :], ada_ref[4:5, :]
    h = (x * (1.0 + scale1) + shift1).astype(MXU_DTYPE)
    merged = None
    for n, (o_ref, w_ref) in enumerate(((oa_ref, wpa_ref), (ob_ref, wpb_ref), (oc_ref, wpc_ref))):
        cs = slice(n * d, (n + 1) * d)
        g = jax.nn.sigmoid(jnp.dot(h, wg_ref[:, cs], preferred_element_type=F32) + bg_ref[:, cs])
        term = g * jnp.dot(o_ref[...], w_ref[...], preferred_element_type=F32)
        merged = term if merged is None else merged + term
    y = jnp.dot(merged.astype(MXU_DTYPE), wo_ref[...], preferred_element_type=F32)
    x1 = _layer_norm(DEEPNORM_ALPHA * x + (1.0 + gate1) * y, lng_ref[...], lnb_ref[...])
    x1_ref[...] = x1
    h2 = x1 * (1.0 + scale2) + shift2
    h2_ref[...] = _pack_halves(h2)
    logit_ref[...] = _nt_dot(wr_ref[...], h2.astype(MXU_DTYPE)) + br_ref[...]


def _merge(x2d, ada_l, o_a, o_b, o_c, w_gate, b_gate, w_pa, w_pb, w_pc, w_o, ln_g, ln_b,
           w_router_t, b_router, seq):
    t, d = x2d.shape
    tm = min(ROW_TILE, seq)
    tiles_per_seq = seq // tm
    row = lambda i: (i, 0)
    whole = lambda a: pl.BlockSpec(a.shape, lambda i: (0,) * a.ndim)
    return pl.pallas_call(
        _merge_kernel,
        grid=(t // tm,),
        in_specs=[
            pl.BlockSpec((tm, d), row),
            pl.BlockSpec((None, 6, d), lambda i: (i // tiles_per_seq, 0, 0)),
            pl.BlockSpec((tm, ATT_WIDTH), row), pl.BlockSpec((tm, ATT_WIDTH), row),
            pl.BlockSpec((tm, LRU_WIDTH), row),
            whole(w_gate), whole(b_gate), whole(w_pa), whole(w_pb), whole(w_pc), whole(w_o),
            whole(ln_g), whole(ln_b), whole(w_router_t), whole(b_router),
        ],
        out_specs=[pl.BlockSpec((tm, d), row), pl.BlockSpec((tm, d // 2), row),
                   pl.BlockSpec((N_EXPERTS, tm), lambda i: (0, i))],
        out_shape=[jax.ShapeDtypeStruct((t, d), F32),
                   jax.ShapeDtypeStruct((t, d // 2), jnp.uint32),
                   jax.ShapeDtypeStruct((N_EXPERTS, t), F32)],
        compiler_params=_params(1),
        name="merge_outproj_ln",
    )(x2d, ada_l, o_a, o_b, o_c, w_gate, b_gate, w_pa, w_pb, w_pc, w_o, ln_g, ln_b,
      w_router_t, b_router)


def _route_kernel(logit_ref, tri_ref, eid_ref, gate_ref, pos_ref, cnt_ref, carry_ref):
    i = pl.program_id(0)
    tr = logit_ref.shape[1]

    @pl.when(i == 0)
    def _():
        carry_ref[...] = jnp.zeros_like(carry_ref)

    erow = lax.broadcasted_iota(jnp.int32, (N_EXPERTS, tr), 0)
    cur = logit_ref[...]
    vals, ids = [], []
    for _ in range(TOP_K):
        m = jnp.max(cur, axis=0, keepdims=True)
        idx = jnp.min(jnp.where(cur == m, erow, N_EXPERTS), axis=0, keepdims=True)
        vals.append(m)
        ids.append(idx)
        cur = jnp.where(erow == idx, -jnp.inf, cur)
    exps = [jnp.exp(v - vals[0]) for v in vals]
    denom = exps[0] + exps[1] + exps[2] + exps[3]
    chosen = jnp.zeros((N_EXPERTS, tr), F32)
    for idx in ids:
        chosen = chosen + jnp.where(erow == idx, 1.0, 0.0)
    prefix = (jnp.dot(chosen.astype(MXU_DTYPE), tri_ref[...], preferred_element_type=F32)
              + carry_ref[:, 0:1])
    for k in range(TOP_K):
        eid_ref[k:k + 1, :] = ids[k]
        gate_ref[k:k + 1, :] = exps[k] / denom
        pos_ref[k:k + 1, :] = jnp.sum(jnp.where(erow == ids[k], prefix, 0.0), axis=0,
                                      keepdims=True).astype(jnp.int32)
    carry_ref[...] = carry_ref[...] + jnp.sum(chosen, axis=1, keepdims=True)
    cnt_ref[...] = carry_ref[...].astype(jnp.int32)


def _route(logits_t):
    n_e, t = logits_t.shape
    tr = min(ROUTE_TILE, t)
    tri = (lax.broadcasted_iota(jnp.int32, (tr, tr), 0)
           < lax.broadcasted_iota(jnp.int32, (tr, tr), 1)).astype(MXU_DTYPE)
    tok = lambda i: (0, i)
    return pl.pallas_call(
        _route_kernel,
        grid=(t // tr,),
        in_specs=[pl.BlockSpec((n_e, tr), tok), pl.BlockSpec((tr, tr), lambda i: (0, 0))],
        out_specs=[pl.BlockSpec((TOP_K, tr), tok)] * 3
        + [pl.BlockSpec((n_e, LANES), lambda i: (0, 0))],
        out_shape=[jax.ShapeDtypeStruct((TOP_K, t), jnp.int32),
                   jax.ShapeDtypeStruct((TOP_K, t), F32),
                   jax.ShapeDtypeStruct((TOP_K, t), jnp.int32),
                   jax.ShapeDtypeStruct((n_e, LANES), jnp.int32)],
        scratch_shapes=[pltpu.VMEM((n_e, LANES), F32)],
        compiler_params=_params(1),
        name="route_topk",
    )(logits_t, tri)


def _dest_kernel(start_ref, eid_ref, pos_ref, dest_ref):
    eid = eid_ref[...]
    dest = pos_ref[...]
    for e in range(N_EXPERTS):
        dest = jnp.where(eid == e, dest + start_ref[e], dest)
    dest_ref[...] = dest


def _dest(pad_start, eid, pos):
    k, t = eid.shape
    tr = min(ROUTE_TILE, t)
    spec = pl.BlockSpec((k, tr), lambda i, s: (0, i))
    return pl.pallas_call(
        _dest_kernel,
        grid_spec=pltpu.PrefetchScalarGridSpec(
            num_scalar_prefetch=1, grid=(t // tr,), in_specs=[spec, spec], out_specs=spec),
        out_shape=jax.ShapeDtypeStruct((k, t), jnp.int32),
        compiler_params=_params(1),
        name="route_dest",
    )(pad_start, eid, pos)


def _sc_mesh():
    return plsc.VectorSubcoreMesh(core_axis_name="c", subcore_axis_name="s")


def _sc_worker():
    return lax.axis_index("s") * SC_CORES + lax.axis_index("c")


def _sc_scatter_rows(rows, idx, n_out):
    t, width = rows.shape
    per_worker = t // (SC_CORES * SC_SUBCORES)
    n_chunks = per_worker // SC_CHUNK

    @functools.partial(
        pl.kernel, mesh=_sc_mesh(),
        out_type=jax.ShapeDtypeStruct((n_out, width), rows.dtype),
        scratch_types=[pltpu.VMEM((SC_CHUNK,), jnp.int32),
                       pltpu.VMEM((SC_CHUNK, width), rows.dtype),
                       pltpu.SemaphoreType.DMA],
    )
    def scatter(rows_hbm, idx_hbm, out_hbm, idx_v, rows_v, sem):
        base = _sc_worker() * per_worker

        @pl.loop(0, n_chunks)
        def _(g):
            off = base + g * SC_CHUNK
            pltpu.sync_copy(rows_hbm.at[pl.ds(off, SC_CHUNK)], rows_v)
            for k in range(TOP_K):
                pltpu.sync_copy(idx_hbm.at[pl.ds(k * t + off, SC_CHUNK)], idx_v)
                pltpu.async_copy(rows_v, out_hbm.at[idx_v], sem).wait()

    return scatter(rows, idx)


def _sc_gather_rows(table, idx):
    n_rows = idx.shape[0]
    width = table.shape[1]
    per_worker = n_rows // (SC_CORES * SC_SUBCORES)
    n_chunks = per_worker // SC_CHUNK

    @functools.partial(
        pl.kernel, mesh=_sc_mesh(),
        out_type=jax.ShapeDtypeStruct((n_rows, width), table.dtype),
        scratch_types=[pltpu.VMEM((SC_CHUNK,), jnp.int32),
                       pltpu.VMEM((SC_CHUNK, width), table.dtype),
                       pltpu.SemaphoreType.DMA],
    )
    def gather(table_hbm, idx_hbm, out_hbm, idx_v, rows_v, sem):
        base = _sc_worker() * per_worker

        @pl.loop(0, n_chunks)
        def _(g):
            off = base + g * SC_CHUNK
            pltpu.sync_copy(idx_hbm.at[pl.ds(off, SC_CHUNK)], idx_v)
            pltpu.async_copy(table_hbm.at[idx_v], rows_v, sem).wait()
            pltpu.sync_copy(rows_v, out_hbm.at[pl.ds(off, SC_CHUNK)])

    return gather(table, idx)


def _expert_kernel(be_ref, valid_ref, x_ref, wgu_ref, bgu_ref, wd_ref, bd_ref, y_ref,
                   wgu_lp, wd_lp):
    i = pl.program_id(0)
    d_e = wd_ref.shape[0]
    n_valid = valid_ref[i]

    @pl.when((i == 0) | (be_ref[i] != be_ref[jnp.maximum(i - 1, 0)]))
    def _():
        wgu_lp[...] = wgu_ref[...].astype(MXU_DTYPE)
        wd_lp[...] = wd_ref[...].astype(MXU_DTYPE)

    @pl.when(n_valid > 0)
    def _():
        row = lax.broadcasted_iota(jnp.int32, x_ref.shape, 0)
        lo, hi = _unpack_halves(jnp.where(row < n_valid, x_ref[...], jnp.uint32(0)))
        x = jnp.concatenate([lo, hi], axis=1).astype(MXU_DTYPE)
        gu = jnp.dot(x, wgu_lp[...], preferred_element_type=F32) + bgu_ref[...]
        gate = jnp.minimum(gu[:, :d_e], SWIGLU_LIMIT)
        up = jnp.clip(gu[:, d_e:], -SWIGLU_LIMIT, SWIGLU_LIMIT)
        act = (up + 1.0) * (gate * jax.nn.sigmoid(SWIGLU_ALPHA * gate))
        y = jnp.dot(act.astype(MXU_DTYPE), wd_lp[...], preferred_element_type=F32) + bd_ref[...]
        y_ref[...] = _pack_halves(y)

    @pl.when(n_valid == 0)
    def _():
        y_ref[...] = jnp.zeros_like(y_ref)


def _experts(layer, block_e, valid, x_buf, w_gu, b_gu, w_down, b_down):
    n_slots, half = x_buf.shape
    _, n_e, d, d2 = w_gu.shape
    d_e = w_down.shape[2]
    n_blocks = n_slots // EXPERT_BLOCK
    n_layers = w_gu.shape[0]
    return pl.pallas_call(
        _expert_kernel,
        grid_spec=pltpu.PrefetchScalarGridSpec(
            num_scalar_prefetch=2,
            grid=(n_blocks,),
            in_specs=[
                pl.BlockSpec((EXPERT_BLOCK, half), lambda i, be, nu: (i, 0)),
                pl.BlockSpec((None, None, d, d2), lambda i, be, nu: (layer, be[i], 0, 0)),
                pl.BlockSpec((None, None, 1, d2), lambda i, be, nu: (layer, be[i], 0, 0)),
                pl.BlockSpec((None, None, d_e, d), lambda i, be, nu: (layer, be[i], 0, 0)),
                pl.BlockSpec((None, None, 1, d), lambda i, be, nu: (layer, be[i], 0, 0)),
            ],
            out_specs=pl.BlockSpec((EXPERT_BLOCK, half), lambda i, be, nu: (i, 0)),
            scratch_shapes=[pltpu.VMEM((d, d2), MXU_DTYPE), pltpu.VMEM((d_e, d), MXU_DTYPE)],
        ),
        out_shape=jax.ShapeDtypeStruct((n_slots, half), jnp.uint32),
        compiler_params=_params(1),
        name="moe_experts",
    )(block_e, valid, x_buf, w_gu, b_gu.reshape(n_layers, n_e, 1, d2), w_down,
      b_down.reshape(n_layers, n_e, 1, d))


def _combine_kernel(x1_ref, ada_ref, gate_ref, lng_ref, lnb_ref, *rest):
    y_refs, o_ref = rest[:TOP_K], rest[-1]
    gates = gate_ref[...]
    y_lo = y_hi = None
    for k in range(TOP_K):
        lo, hi = _unpack_halves(y_refs[k][...])
        g = gates[:, k:k + 1]
        y_lo = lo * g if y_lo is None else y_lo + lo * g
        y_hi = hi * g if y_hi is None else y_hi + hi * g
    y = jnp.concatenate([y_lo, y_hi], axis=1)
    gate2 = ada_ref[5:6, :]
    o_ref[...] = _layer_norm(DEEPNORM_ALPHA * x1_ref[...] + (1.0 + gate2) * y,
                             lng_ref[...], lnb_ref[...])


def _combine(x1, ada_l, gates_tk, ln_g, ln_b, y_rows, seq, part, n_parts, earlier):
    t, d = x1.shape
    tm = min(MOVE_TILE, seq)
    tiles_per_seq = seq // tm
    steps = t // tm // n_parts
    first = part * steps
    tok = lambda i: (first + i, 0)
    in_specs = [
        pl.BlockSpec((tm, d), tok),
        pl.BlockSpec((None, 6, d), lambda i: ((first + i) // tiles_per_seq, 0, 0)),
        pl.BlockSpec((tm, TOP_K), tok),
        pl.BlockSpec((1, d), lambda i: (0, 0)),
        pl.BlockSpec((1, d), lambda i: (0, 0)),
    ] + [
        pl.BlockSpec((tm, d // 2), lambda i, k=k: (k * steps + i, 0)) for k in range(TOP_K)
    ]
    args = [x1, ada_l, gates_tk, ln_g, ln_b] + [y_rows] * TOP_K
    aliases = {}
    if earlier is not None:
        in_specs.append(pl.BlockSpec(memory_space=pl.ANY))
        aliases = {len(args): 0}
        args.append(earlier)
    return pl.pallas_call(
        _combine_kernel,
        grid=(steps,),
        in_specs=in_specs,
        out_specs=pl.BlockSpec((tm, d), tok),
        out_shape=jax.ShapeDtypeStruct((t, d), F32),
        input_output_aliases=aliases,
        compiler_params=_params(1),
        name="moe_combine_ln",
    )(*args)


def _split_w_in(w_in):
    d = w_in.shape[0]
    sizes = (ATT_WIDTH, ATT_WIDTH, ATT_WIDTH, N_HEADS, ATT_WIDTH, ATT_WIDTH, ATT_WIDTH,
             LRU_WIDTH, LRU_WIDTH)
    bounds = [sum(sizes[:n]) for n in range(len(sizes) + 1)]
    qa, ka, va, fa, qb, kb, vb, xc, gc = (
        w_in[:, bounds[n]:bounds[n + 1]] for n in range(len(sizes)))
    forget = jnp.concatenate([fa, jnp.zeros((d, LANES - N_HEADS), w_in.dtype)], axis=1)
    w_main = jnp.concatenate([qa, ka, qb, kb, xc, gc, forget], axis=1).astype(MXU_DTYPE)
    w_vt = jnp.concatenate([va, vb], axis=1).T.astype(MXU_DTYPE)
    return w_main, w_vt


def _block_diag(w):
    n, c, dd = w.shape
    eye = jnp.eye(n, dtype=w.dtype)
    return (eye[:, None, :, None] * w[:, :, None, :]).reshape(n * c, n * dd).astype(MXU_DTYPE)


def kernel(x, c, w_ada, b_ada, ln1_g, ln1_b, w_in, b_f, conv_w, conv_b, lru_wa, lru_ba, lru_wx,
           lru_bx, lru_lambda, w_gate, b_gate, w_pa, w_pb, w_pc, w_o, ln2_g, ln2_b, w_router,
           b_router, w_gu, b_gu, w_down, b_down):
    n_batch, seq, d = x.shape
    t = n_batch * seq
    n_layers = w_ada.shape[0]
    n_blocks = (t * TOP_K) // EXPERT_BLOCK + N_EXPERTS
    n_slots = n_blocks * EXPERT_BLOCK
    att_tile = min(ATT_TILE, seq)
    vec = lambda a: a.reshape(1, -1)

    ada = _ada(c, w_ada, b_ada).reshape(n_layers, n_batch, 6, d)
    x2d = x.reshape(t, d)
    for l in range(n_layers):
        ada_l = ada[l]
        bf_pad = jnp.concatenate([b_f[l], jnp.zeros((LANES - N_HEADS,), F32)]).reshape(1, LANES)
        w_main, w_vt = _split_w_in(w_in[l])
        qa, ka, qb, kb, xc, gc, vat, vbt, fcol, frow = _inproj(
            x2d, ada_l, w_main, w_vt, bf_pad, seq)
        frow4 = frow.reshape(n_batch, N_HEADS, seq // att_tile, att_tile)
        o_a = _fox(qa, ka, vat, fcol, frow4, seq)
        o_b = _sb(qb, kb, vbt, seq)
        o_c = _lru(xc, gc, conv_w[l], vec(conv_b[l]), _block_diag(lru_wa[l]), vec(lru_ba[l]),
                   _block_diag(lru_wx[l]), vec(lru_bx[l]), vec(lru_lambda[l]), seq)
        x1, h2, logits_t = _merge(
            x2d, ada_l, o_a, o_b, o_c, w_gate[l].astype(MXU_DTYPE), vec(b_gate[l]),
            w_pa[l].astype(MXU_DTYPE), w_pb[l].astype(MXU_DTYPE), w_pc[l].astype(MXU_DTYPE),
            w_o[l].astype(MXU_DTYPE), vec(ln1_g[l]), vec(ln1_b[l]),
            w_router[l].T.astype(MXU_DTYPE), b_router[l].reshape(N_EXPERTS, 1), seq)

        eid, gates, pos, cnt = _route(logits_t)
        counts = cnt[:, 0]
        padded = (counts + EXPERT_BLOCK - 1) // EXPERT_BLOCK * EXPERT_BLOCK
        pad_end = jnp.cumsum(padded)
        pad_start = (pad_end - padded).astype(jnp.int32)
        block_first = (jnp.arange(n_blocks) * EXPERT_BLOCK)[:, None]
        block_e = jnp.minimum(jnp.sum(pad_end[None, :] <= block_first, axis=1),
                              N_EXPERTS - 1).astype(jnp.int32)
        valid = jnp.clip(counts[block_e] + pad_start[block_e] - block_first[:, 0],
                         0, EXPERT_BLOCK).astype(jnp.int32)
        dest = _dest(pad_start, eid, pos)

        x_buf = _sc_scatter_rows(h2, dest.reshape(-1), n_slots)
        y_buf = _experts(l, block_e, valid, x_buf, w_gu, b_gu, w_down, b_down)
        t_part = t // COMBINE_PARTS
        x2d = None
        for part in range(COMBINE_PARTS):
            idx = dest[:, part * t_part:(part + 1) * t_part].reshape(-1)
            y_rows = _sc_gather_rows(y_buf, idx)
            x2d = _combine(x1, ada_l, gates.T, vec(ln2_g[l]), vec(ln2_b[l]), y_rows, seq,
                           part, COMBINE_PARTS, x2d)
    return x2d.reshape(n_batch, seq, d)
```

```python
import functools

import jax
import jax.numpy as jnp
from jax import lax
from jax.experimental import pallas as pl
from jax.experimental.pallas import tpu as pltpu
from jax.experimental.pallas import tpu_sc as plsc

D_MODEL = 1024
DEPTH = 2
HEAD_DIM = 64
N_HEADS = 8
ATT_WIDTH = N_HEADS * HEAD_DIM
LRU_WIDTH = D_MODEL
LRU_BLOCKS = 16
CONV_WIDTH = 4
LRU_C = 8.0
N_EXPERTS = 32
TOP_K = 4
SWIGLU_LIMIT = 7.0
SWIGLU_ALPHA = 1.702
LN_EPS = 1e-5
DEEPNORM_ALPHA = (2.0 * DEPTH) ** 0.25
ATT_SCALE = HEAD_DIM ** -0.5
LOG2_E = 1.4426950408889634

LANES = 128
MXU_DTYPE = jnp.bfloat16
F32 = jnp.float32

ROW_TILE = 512
ATT_TILE = 256
ATT_GROUP = 8
LRU_TILE = 256
ROUTE_TILE = 512
EXPERT_BLOCK = 256
MOVE_TILE = 256
COMBINE_PARTS = 4
SC_CORES = 2
SC_SUBCORES = 16
SC_CHUNK = 64
VMEM_LIMIT = 56 * 1024 * 1024


def _params(n_axes, vmem=VMEM_LIMIT):
    return pltpu.CompilerParams(
        dimension_semantics=("arbitrary",) * n_axes, vmem_limit_bytes=vmem)


def _log_sigmoid(x):
    return jnp.minimum(x, 0.0) - jnp.log1p(jnp.exp(-jnp.abs(x)))


def _softplus(x):
    return jnp.maximum(x, 0.0) + jnp.log1p(jnp.exp(-jnp.abs(x)))


def _layer_norm(v, g, b):
    mu = jnp.mean(v, axis=-1, keepdims=True)
    d = v - mu
    var = jnp.mean(d * d, axis=-1, keepdims=True)
    return d * lax.rsqrt(var + LN_EPS) * g + b


def _pack_halves(x):
    n = x.shape[1] // 2
    bits = lambda v: lax.bitcast_convert_type(v.astype(jnp.bfloat16).astype(F32), jnp.uint32)
    return (bits(x[:, :n]) >> 16) | bits(x[:, n:])


def _unpack_halves(p):
    lo = lax.bitcast_convert_type(p << 16, F32)
    hi = lax.bitcast_convert_type(p & jnp.uint32(0xFFFF0000), F32)
    return lo, hi


def _split3(x):
    hi = x.astype(MXU_DTYPE)
    r1 = x - hi.astype(F32)
    mid = r1.astype(MXU_DTYPE)
    lo = (r1 - mid.astype(F32)).astype(MXU_DTYPE)
    return hi, mid, lo


def _ada_kernel(c_ref, w_ref, b_ref, o_ref):
    c = c_ref[...]
    cond = c * jax.nn.sigmoid(c)
    o_ref[...] = jnp.dot(cond.astype(MXU_DTYPE), w_ref[...].astype(MXU_DTYPE),
                         preferred_element_type=F32) + b_ref[...]


def _ada(c, w_ada, b_ada):
    n_layers, d, n6 = w_ada.shape
    b = c.shape[0]
    tn = n6 // 4
    return pl.pallas_call(
        _ada_kernel,
        grid=(n_layers, n6 // tn),
        in_specs=[
            pl.BlockSpec((b, d), lambda l, j: (0, 0)),
            pl.BlockSpec((None, d, tn), lambda l, j: (l, 0, j)),
            pl.BlockSpec((None, 1, tn), lambda l, j: (l, 0, j)),
        ],
        out_specs=pl.BlockSpec((None, b, tn), lambda l, j: (l, 0, j)),
        out_shape=jax.ShapeDtypeStruct((n_layers, b, n6), F32),
        compiler_params=_params(2),
        name="ada",
    )(c, w_ada, b_ada.reshape(n_layers, 1, n6))


def _nt_dot(a, b):
    return lax.dot_general(a, b, (((1,), (1,)), ((), ())), preferred_element_type=F32)


def _inproj_kernel(x_ref, ada_ref, w_ref, wvt_ref, bf_ref, tri_ref,
                   qa_ref, ka_ref, qb_ref, kb_ref, xc_ref, gc_ref, vat_ref, vbt_ref,
                   fcol_ref, frow_ref, carry_ref, *, tiles_per_seq):
    i = pl.program_id(0)
    tm = x_ref.shape[0]
    shift = ada_ref[0:1, :]
    scale = ada_ref[1:2, :]
    h = (x_ref[...] * (1.0 + scale) + shift).astype(MXU_DTYPE)
    off = 0
    for ref in (qa_ref, ka_ref, qb_ref, kb_ref, xc_ref, gc_ref):
        width = ref.shape[1]
        ref[...] = jnp.dot(h, w_ref[:, off:off + width],
                           preferred_element_type=F32).astype(ref.dtype)
        off += width
    vt = _nt_dot(wvt_ref[...], h).astype(MXU_DTYPE)
    chunk = vat_ref.shape[2]
    for n in range(tm // chunk):
        cols = slice(n * chunk, (n + 1) * chunk)
        vat_ref[n] = vt[0:ATT_WIDTH, cols]
        vbt_ref[n] = vt[ATT_WIDTH:2 * ATT_WIDTH, cols]
    zf = jnp.dot(h, w_ref[:, off:off + LANES], preferred_element_type=F32) + bf_ref[...]
    logf = _log_sigmoid(zf)

    @pl.when(i % tiles_per_seq == 0)
    def _():
        carry_ref[...] = jnp.zeros_like(carry_ref)

    tri = tri_ref[...]
    f_cum = carry_ref[0:1, :]
    for piece in _split3(logf):
        f_cum = f_cum + jnp.dot(tri, piece, preferred_element_type=F32)
    carry_ref[0:1, :] = f_cum[tm - 1:tm, :]
    fcol_ref[...] = f_cum
    frow_ref[...] = f_cum.T[0:N_HEADS, :]


def _inproj(x2d, ada_l, w_main, w_vt, bf_pad, seq):
    t, d = x2d.shape
    tm = min(ROW_TILE, seq)
    chunk = min(ATT_TILE, seq)
    tiles_per_seq = seq // tm
    n_batch = t // seq
    tri = (lax.broadcasted_iota(jnp.int32, (tm, tm), 1)
           <= lax.broadcasted_iota(jnp.int32, (tm, tm), 0)).astype(MXU_DTYPE)
    row = lambda i: (i, 0)
    att = jax.ShapeDtypeStruct((t, ATT_WIDTH), MXU_DTYPE)
    att_t = jax.ShapeDtypeStruct((t // chunk, ATT_WIDTH, chunk), MXU_DTYPE)
    wide = jax.ShapeDtypeStruct((t, LRU_WIDTH), F32)
    att_spec = pl.BlockSpec((tm, ATT_WIDTH), row)
    att_t_spec = pl.BlockSpec((tm // chunk, ATT_WIDTH, chunk), lambda i: (i, 0, 0))
    wide_spec = pl.BlockSpec((tm, LRU_WIDTH), row)
    return pl.pallas_call(
        functools.partial(_inproj_kernel, tiles_per_seq=tiles_per_seq),
        grid=(t // tm,),
        in_specs=[
            pl.BlockSpec((tm, d), row),
            pl.BlockSpec((None, 6, d), lambda i: (i // tiles_per_seq, 0, 0)),
            pl.BlockSpec(w_main.shape, lambda i: (0, 0)),
            pl.BlockSpec(w_vt.shape, lambda i: (0, 0)),
            pl.BlockSpec((1, LANES), lambda i: (0, 0)),
            pl.BlockSpec((tm, tm), lambda i: (0, 0)),
        ],
        out_specs=[att_spec] * 4 + [wide_spec] * 2 + [att_t_spec] * 2 + [
            pl.BlockSpec((tm, LANES), row),
            pl.BlockSpec((None, N_HEADS, tm),
                         lambda i: (i // tiles_per_seq, 0, i % tiles_per_seq)),
        ],
        out_shape=[att] * 4 + [wide] * 2 + [att_t] * 2 + [
            jax.ShapeDtypeStruct((t, LANES), F32),
            jax.ShapeDtypeStruct((n_batch, N_HEADS, seq), F32),
        ],
        scratch_shapes=[pltpu.VMEM((8, LANES), F32)],
        compiler_params=_params(1),
        name="inproj",
    )(x2d, ada_l, w_main, w_vt, bf_pad, tri)


def _head_query(q2, c):
    lane = lax.broadcasted_iota(jnp.int32, (1, LANES), 1)
    in_head = (lane >= c * HEAD_DIM) & (lane < (c + 1) * HEAD_DIM)
    return jnp.where(in_head, q2, jnp.zeros_like(q2)) * ATT_SCALE


def _pair_cols(head):
    return slice((head // 2) * LANES, (head // 2 + 1) * LANES)


def _store_heads(o_ref, heads, outs_t):
    sub = lax.broadcasted_iota(jnp.int32, (LANES, 1), 0)
    for n in range(0, len(heads), 2):
        pair_t = jnp.where(sub < HEAD_DIM, outs_t[n], outs_t[n + 1])
        o_ref[:, _pair_cols(heads[n])] = pair_t.T.astype(o_ref.dtype)


def _fox_kernel(q_ref, k_ref, vt_ref, fcol_ref, frow_ref, o_ref):
    i = pl.program_id(1)
    tq = q_ref.shape[0]
    key = lax.broadcasted_iota(jnp.int32, (tq, tq), 0)
    qry = lax.broadcasted_iota(jnp.int32, (tq, tq), 1)
    causal = key <= qry
    for g in range(N_HEADS // ATT_GROUP):
        heads = list(range(g * ATT_GROUP, (g + 1) * ATT_GROUP))
        qms = [_head_query(q_ref[:, _pair_cols(hd)], hd % 2) for hd in heads]
        fqs = [frow_ref[hd, pl.ds(i, 1), :] for hd in heads]

        def qk(j, heads=heads, qms=qms):
            start = pl.multiple_of(j * tq, tq)
            return tuple(_nt_dot(k_ref[pl.ds(start, tq), _pair_cols(hd)], qms[n])
                         for n, hd in enumerate(heads))

        def block(j, scores, carry, masked, heads=heads, fqs=fqs):
            start = pl.multiple_of(j * tq, tq)
            stats, probs = [], []
            for n, hd in enumerate(heads):
                m, l, _ = carry[n]
                fk = fcol_ref[pl.ds(start, tq), hd:hd + 1]
                s = scores[n] - fk
                if masked:
                    s = jnp.where(causal, s, -jnp.inf)
                m_new = jnp.maximum(m, jnp.max(s, axis=0, keepdims=True) + fqs[n])
                alpha = jnp.exp(m - m_new)
                pm = jnp.exp(s + (fqs[n] - m_new))
                l = alpha * l + jnp.sum(pm, axis=0, keepdims=True)
                stats.append((m_new, l, alpha))
                probs.append(pm.astype(MXU_DTYPE))
            pvs = [jnp.dot(vt_ref[j, _pair_cols(hd), :], probs[n],
                           preferred_element_type=F32) for n, hd in enumerate(heads)]
            return tuple((stats[n][0], stats[n][1], stats[n][2] * carry[n][2] + pvs[n])
                         for n in range(len(heads)))

        init = tuple((jnp.full((1, tq), -jnp.inf, F32), jnp.zeros((1, tq), F32),
                      jnp.zeros((LANES, tq), F32)) for _ in heads)
        carry = lax.fori_loop(
            0, i, lambda j, cr, qk=qk, block=block: block(j, qk(j), cr, False), init)
        carry = block(i, qk(i), carry, True)
        _store_heads(o_ref, heads, [acc / l for _, l, acc in carry])


def _att_specs(t, seq):
    n_batch = t // seq
    tq = min(ATT_TILE, seq)
    nq = seq // tq
    q_spec = pl.BlockSpec((tq, ATT_WIDTH), lambda b, i: (b * nq + i, 0))
    k_spec = pl.BlockSpec((seq, ATT_WIDTH), lambda b, i: (b, 0))
    vt_spec = pl.BlockSpec((nq, ATT_WIDTH, tq), lambda b, i: (b, 0, 0))
    return n_batch, tq, nq, q_spec, k_spec, vt_spec


def _fox(q, k, vt, fcol, frow4, seq):
    t = q.shape[0]
    n_batch, tq, nq, q_spec, k_spec, vt_spec = _att_specs(t, seq)
    return pl.pallas_call(
        _fox_kernel,
        grid=(n_batch, nq),
        in_specs=[
            q_spec, k_spec, vt_spec,
            pl.BlockSpec((seq, LANES), lambda b, i: (b, 0)),
            pl.BlockSpec((None, N_HEADS, nq, tq), lambda b, i: (b, 0, 0, 0)),
        ],
        out_specs=q_spec,
        out_shape=jax.ShapeDtypeStruct((t, ATT_WIDTH), MXU_DTYPE),
        compiler_params=_params(2),
        name="fox_attention",
    )(q, k, vt, fcol, frow4)


def _sb_kernel(q_ref, k_ref, vt_ref, suf_ref, o_ref):
    i = pl.program_id(1)
    tq = q_ref.shape[0]
    key = lax.broadcasted_iota(jnp.int32, (tq, tq), 0)
    qry = lax.broadcasted_iota(jnp.int32, (tq, tq), 1)
    strict = key < qry
    suf = suf_ref[...]
    for g in range(N_HEADS // ATT_GROUP):
        heads = list(range(g * ATT_GROUP, (g + 1) * ATT_GROUP))
        qms = [_head_query(q_ref[:, _pair_cols(hd)], hd % 2) for hd in heads]

        def qk(j, heads=heads, qms=qms):
            start = pl.multiple_of(j * tq, tq)
            return tuple(_nt_dot(k_ref[pl.ds(start, tq), _pair_cols(hd)], qms[n])
                         for n, hd in enumerate(heads))

        def block(j, zs, carry, masked, heads=heads):
            log_betas, splits, laters = [], [], []
            for n in range(len(heads)):
                z = zs[n]
                sp = jnp.maximum(z, 0.0) + jnp.log(1.0 + jnp.exp2(jnp.abs(z) * -LOG2_E))
                log_betas.append(z - sp)
                if masked:
                    sp = jnp.where(strict, sp, 0.0)
                splits.append(sp.astype(MXU_DTYPE))
                laters.append(carry[n][0] + jnp.sum(sp, axis=0, keepdims=True))
            afters = [jnp.dot(suf, splits[n], preferred_element_type=F32)
                      for n in range(len(heads))]
            ws = []
            for n in range(len(heads)):
                w = jnp.exp(log_betas[n] - jnp.maximum(afters[n] + carry[n][0], 0.0))
                if masked:
                    w = jnp.where(strict, w, 0.0)
                ws.append(w.astype(MXU_DTYPE))
            return tuple(
                (laters[n], carry[n][1] + jnp.dot(vt_ref[j, _pair_cols(hd), :], ws[n],
                                                  preferred_element_type=F32))
                for n, hd in enumerate(heads))

        init = tuple((jnp.zeros((1, tq), F32), jnp.zeros((LANES, tq), F32)) for _ in heads)

        carry = block(i, qk(i), init, True)
        carry = lax.fori_loop(
            0, i, lambda s, cr, qk=qk, block=block: block(i - 1 - s, qk(i - 1 - s), cr, False),
            carry)
        _store_heads(o_ref, heads, [acc for _, acc in carry])


def _sb(q, k, vt, seq):
    t = q.shape[0]
    n_batch, tq, nq, q_spec, k_spec, vt_spec = _att_specs(t, seq)
    suf = (lax.broadcasted_iota(jnp.int32, (tq, tq), 1)
           > lax.broadcasted_iota(jnp.int32, (tq, tq), 0)).astype(MXU_DTYPE)
    return pl.pallas_call(
        _sb_kernel,
        grid=(n_batch, nq),
        in_specs=[q_spec, k_spec, vt_spec, pl.BlockSpec((tq, tq), lambda b, i: (0, 0))],
        out_specs=q_spec,
        out_shape=jax.ShapeDtypeStruct((t, ATT_WIDTH), MXU_DTYPE),
        compiler_params=_params(2),
        name="sb_attention",
    )(q, k, vt, suf)


def _lru_kernel(xc_ref, gc_ref, cw_ref, cb_ref, wa_ref, ba_ref, wx_ref, bx_ref, lam_ref,
                o_ref, ext_ref, a_ref, u_ref, h_ref, state_ref):
    s_idx = pl.program_id(1)
    ts = xc_ref.shape[0]
    halo = 8

    @pl.when(s_idx == 0)
    def _():
        ext_ref[0:halo, :] = jnp.zeros((halo, LRU_WIDTH), F32)
        state_ref[...] = jnp.zeros_like(state_ref)

    ext_ref[halo:halo + ts, :] = xc_ref[...]
    xconv = cb_ref[...]
    for j in range(CONV_WIDTH):
        lo = halo - (CONV_WIDTH - 1) + j
        xconv = xconv + cw_ref[j:j + 1, :] * ext_ref[lo:lo + ts, :]
    ext_ref[0:halo, :] = ext_ref[ts:ts + halo, :]

    xb = xconv.astype(MXU_DTYPE)
    r = jax.nn.sigmoid(jnp.dot(xb, wa_ref[...], preferred_element_type=F32) + ba_ref[...])
    ig = jax.nn.sigmoid(jnp.dot(xb, wx_ref[...], preferred_element_type=F32) + bx_ref[...])
    log_a = (-LRU_C * _softplus(-lam_ref[...])) * r
    a = jnp.exp(log_a)
    a_ref[...] = a
    u_ref[...] = jnp.sqrt(1.0 - a * a) * (ig * xconv)

    def step(t, h):
        h = a_ref[pl.ds(t, 1), :] * h + u_ref[pl.ds(t, 1), :]
        h_ref[pl.ds(t, 1), :] = h
        return h

    state_ref[0:1, :] = lax.fori_loop(0, ts, step, state_ref[0:1, :], unroll=8)
    o_ref[...] = (h_ref[...] * jax.nn.gelu(gc_ref[...])).astype(o_ref.dtype)


def _lru(xc, gc, conv_w, conv_b, wa_bd, ba, wx_bd, bx, lam, seq):
    t, w = xc.shape
    n_batch = t // seq
    ts = min(LRU_TILE, seq)
    ns = seq // ts
    row_spec = pl.BlockSpec((ts, w), lambda b, s: (b * ns + s, 0))
    vec_spec = pl.BlockSpec((1, w), lambda b, s: (0, 0))
    mat_spec = pl.BlockSpec((w, w), lambda b, s: (0, 0))
    return pl.pallas_call(
        _lru_kernel,
        grid=(n_batch, ns),
        in_specs=[row_spec, row_spec,
                  pl.BlockSpec((CONV_WIDTH, w), lambda b, s: (0, 0)), vec_spec,
                  mat_spec, vec_spec, mat_spec, vec_spec, vec_spec],
        out_specs=row_spec,
        out_shape=jax.ShapeDtypeStruct((t, w), MXU_DTYPE),
        scratch_shapes=[pltpu.VMEM((ts + 8, w), F32), pltpu.VMEM((ts, w), F32),
                        pltpu.VMEM((ts, w), F32), pltpu.VMEM((ts, w), F32),
                        pltpu.VMEM((8, w), F32)],
        compiler_params=_params(2),
        name="rg_lru",
    )(xc, gc, conv_w, conv_b, wa_bd, ba, wx_bd, bx, lam)


def _merge_kernel(x_ref, ada_ref, oa_ref, ob_ref, oc_ref, wg_ref, bg_ref, wpa_ref, wpb_ref,
                  wpc_ref, wo_ref, lng_ref, lnb_ref, wr_ref, br_ref,
                  x1_ref, h2_ref, logit_ref):
    d = x_ref.shape[1]
    x = x_ref[...]
    shift1, scale1, gate1 = ada_ref[0:1, :], ada_ref[1:2, :], ada_ref[2:3, :]
    shift2, scale2 = ada_ref[3:4, :], ada_ref[4:5, :]
    h = (x * (1.0 + scale1) + shift1).astype(MXU_DTYPE)
    merged = None
    for n, (o_ref, w_ref) in enumerate(((oa_ref, wpa_ref), (ob_ref, wpb_ref), (oc_ref, wpc_ref))):
        cs = slice(n * d, (n + 1) * d)
        g = jax.nn.sigmoid(jnp.dot(h, wg_ref[:, cs], preferred_element_type=F32) + bg_ref[:, cs])
        term = g * jnp.dot(o_ref[...], w_ref[...], preferred_element_type=F32)
        merged = term if merged is None else merged + term
    y = jnp.dot(merged.astype(MXU_DTYPE), wo_ref[...], preferred_element_type=F32)
    x1 = _layer_norm(DEEPNORM_ALPHA * x + (1.0 + gate1) * y, lng_ref[...], lnb_ref[...])
    x1_ref[...] = x1
    h2 = x1 * (1.0 + scale2) + shift2
    h2_ref[...] = _pack_halves(h2)
    logit_ref[...] = _nt_dot(wr_ref[...], h2.astype(MXU_DTYPE)) + br_ref[...]


def _merge(x2d, ada_l, o_a, o_b, o_c, w_gate, b_gate, w_pa, w_pb, w_pc, w_o, ln_g, ln_b,
           w_router_t, b_router, seq):
    t, d = x2d.shape
    tm = min(ROW_TILE, seq)
    tiles_per_seq = seq // tm
    row = lambda i: (i, 0)
    whole = lambda a: pl.BlockSpec(a.shape, lambda i: (0,) * a.ndim)
    return pl.pallas_call(
        _merge_kernel,
        grid=(t // tm,),
        in_specs=[
            pl.BlockSpec((tm, d), row),
            pl.BlockSpec((None, 6, d), lambda i: (i // tiles_per_seq, 0, 0)),
            pl.BlockSpec((tm, ATT_WIDTH), row), pl.BlockSpec((tm, ATT_WIDTH), row),
            pl.BlockSpec((tm, LRU_WIDTH), row),
            whole(w_gate), whole(b_gate), whole(w_pa), whole(w_pb), whole(w_pc), whole(w_o),
            whole(ln_g), whole(ln_b), whole(w_router_t), whole(b_router),
        ],
        out_specs=[pl.BlockSpec((tm, d), row), pl.BlockSpec((tm, d // 2), row),
                   pl.BlockSpec((N_EXPERTS, tm), lambda i: (0, i))],
        out_shape=[jax.ShapeDtypeStruct((t, d), F32),
                   jax.ShapeDtypeStruct((t, d // 2), jnp.uint32),
                   jax.ShapeDtypeStruct((N_EXPERTS, t), F32)],
        compiler_params=_params(1),
        name="merge_outproj_ln",
    )(x2d, ada_l, o_a, o_b, o_c, w_gate, b_gate, w_pa, w_pb, w_pc, w_o, ln_g, ln_b,
      w_router_t, b_router)


def _route_kernel(logit_ref, tri_ref, eid_ref, gate_ref, pos_ref, cnt_ref, carry_ref):
    i = pl.program_id(0)
    tr = logit_ref.shape[1]

    @pl.when(i == 0)
    def _():
        carry_ref[...] = jnp.zeros_like(carry_ref)

    erow = lax.broadcasted_iota(jnp.int32, (N_EXPERTS, tr), 0)
    cur = logit_ref[...]
    vals, ids = [], []
    for _ in range(TOP_K):
        m = jnp.max(cur, axis=0, keepdims=True)
        idx = jnp.min(jnp.where(cur == m, erow, N_EXPERTS), axis=0, keepdims=True)
        vals.append(m)
        ids.append(idx)
        cur = jnp.where(erow == idx, -jnp.inf, cur)
    exps = [jnp.exp(v - vals[0]) for v in vals]
    denom = exps[0] + exps[1] + exps[2] + exps[3]
    chosen = jnp.zeros((N_EXPERTS, tr), F32)
    for idx in ids:
        chosen = chosen + jnp.where(erow == idx, 1.0, 0.0)
    prefix = (jnp.dot(chosen.astype(MXU_DTYPE), tri_ref[...], preferred_element_type=F32)
              + carry_ref[:, 0:1])
    for k in range(TOP_K):
        eid_ref[k:k + 1, :] = ids[k]
        gate_ref[k:k + 1, :] = exps[k] / denom
        pos_ref[k:k + 1, :] = jnp.sum(jnp.where(erow == ids[k], prefix, 0.0), axis=0,
                                      keepdims=True).astype(jnp.int32)
    carry_ref[...] = carry_ref[...] + jnp.sum(chosen, axis=1, keepdims=True)
    cnt_ref[...] = carry_ref[...].astype(jnp.int32)


def _route(logits_t):
    n_e, t = logits_t.shape
    tr = min(ROUTE_TILE, t)
    tri = (lax.broadcasted_iota(jnp.int32, (tr, tr), 0)
           < lax.broadcasted_iota(jnp.int32, (tr, tr), 1)).astype(MXU_DTYPE)
    tok = lambda i: (0, i)
    return pl.pallas_call(
        _route_kernel,
        grid=(t // tr,),
        in_specs=[pl.BlockSpec((n_e, tr), tok), pl.BlockSpec((tr, tr), lambda i: (0, 0))],
        out_specs=[pl.BlockSpec((TOP_K, tr), tok)] * 3
        + [pl.BlockSpec((n_e, LANES), lambda i: (0, 0))],
        out_shape=[jax.ShapeDtypeStruct((TOP_K, t), jnp.int32),
                   jax.ShapeDtypeStruct((TOP_K, t), F32),
                   jax.ShapeDtypeStruct((TOP_K, t), jnp.int32),
                   jax.ShapeDtypeStruct((n_e, LANES), jnp.int32)],
        scratch_shapes=[pltpu.VMEM((n_e, LANES), F32)],
        compiler_params=_params(1),
        name="route_topk",
    )(logits_t, tri)


def _dest_kernel(start_ref, eid_ref, pos_ref, dest_ref):
    eid = eid_ref[...]
    dest = pos_ref[...]
    for e in range(N_EXPERTS):
        dest = jnp.where(eid == e, dest + start_ref[e], dest)
    dest_ref[...] = dest


def _dest(pad_start, eid, pos):
    k, t = eid.shape
    tr = min(ROUTE_TILE, t)
    spec = pl.BlockSpec((k, tr), lambda i, s: (0, i))
    return pl.pallas_call(
        _dest_kernel,
        grid_spec=pltpu.PrefetchScalarGridSpec(
            num_scalar_prefetch=1, grid=(t // tr,), in_specs=[spec, spec], out_specs=spec),
        out_shape=jax.ShapeDtypeStruct((k, t), jnp.int32),
        compiler_params=_params(1),
        name="route_dest",
    )(pad_start, eid, pos)


def _sc_mesh():
    return plsc.VectorSubcoreMesh(core_axis_name="c", subcore_axis_name="s")


def _sc_worker():
    return lax.axis_index("s") * SC_CORES + lax.axis_index("c")


def _sc_scatter_rows(rows, idx, n_out):
    t, width = rows.shape
    per_worker = t // (SC_CORES * SC_SUBCORES)
    n_chunks = per_worker // SC_CHUNK

    @functools.partial(
        pl.kernel, mesh=_sc_mesh(),
        out_type=jax.ShapeDtypeStruct((n_out, width), rows.dtype),
        scratch_types=[pltpu.VMEM((SC_CHUNK,), jnp.int32),
                       pltpu.VMEM((SC_CHUNK, width), rows.dtype),
                       pltpu.SemaphoreType.DMA],
    )
    def scatter(rows_hbm, idx_hbm, out_hbm, idx_v, rows_v, sem):
        base = _sc_worker() * per_worker

        @pl.loop(0, n_chunks)
        def _(g):
            off = base + g * SC_CHUNK
            pltpu.sync_copy(rows_hbm.at[pl.ds(off, SC_CHUNK)], rows_v)
            for k in range(TOP_K):
                pltpu.sync_copy(idx_hbm.at[pl.ds(k * t + off, SC_CHUNK)], idx_v)
                pltpu.async_copy(rows_v, out_hbm.at[idx_v], sem).wait()

    return scatter(rows, idx)


def _sc_gather_rows(table, idx):
    n_rows = idx.shape[0]
    width = table.shape[1]
    per_worker = n_rows // (SC_CORES * SC_SUBCORES)
    n_chunks = per_worker // SC_CHUNK

    @functools.partial(
        pl.kernel, mesh=_sc_mesh(),
        out_type=jax.ShapeDtypeStruct((n_rows, width), table.dtype),
        scratch_types=[pltpu.VMEM((SC_CHUNK,), jnp.int32),
                       pltpu.VMEM((SC_CHUNK, width), table.dtype),
                       pltpu.SemaphoreType.DMA],
    )
    def gather(table_hbm, idx_hbm, out_hbm, idx_v, rows_v, sem):
        base = _sc_worker() * per_worker

        @pl.loop(0, n_chunks)
        def _(g):
            off = base + g * SC_CHUNK
            pltpu.sync_copy(idx_hbm.at[pl.ds(off, SC_CHUNK)], idx_v)
            pltpu.async_copy(table_hbm.at[idx_v], rows_v, sem).wait()
            pltpu.sync_copy(rows_v, out_hbm.at[pl.ds(off, SC_CHUNK)])

    return gather(table, idx)


def _expert_kernel(be_ref, valid_ref, x_ref, wgu_ref, bgu_ref, wd_ref, bd_ref, y_ref,
                   wgu_lp, wd_lp):
    i = pl.program_id(0)
    d_e = wd_ref.shape[0]
    n_valid = valid_ref[i]

    @pl.when((i == 0) | (be_ref[i] != be_ref[jnp.maximum(i - 1, 0)]))
    def _():
        wgu_lp[...] = wgu_ref[...].astype(MXU_DTYPE)
        wd_lp[...] = wd_ref[...].astype(MXU_DTYPE)

    @pl.when(n_valid > 0)
    def _():
        row = lax.broadcasted_iota(jnp.int32, x_ref.shape, 0)
        lo, hi = _unpack_halves(jnp.where(row < n_valid, x_ref[...], jnp.uint32(0)))
        x = jnp.concatenate([lo, hi], axis=1).astype(MXU_DTYPE)
        gu = jnp.dot(x, wgu_lp[...], preferred_element_type=F32) + bgu_ref[...]
        gate = jnp.minimum(gu[:, :d_e], SWIGLU_LIMIT)
        up = jnp.clip(gu[:, d_e:], -SWIGLU_LIMIT, SWIGLU_LIMIT)
        act = (up + 1.0) * (gate * jax.nn.sigmoid(SWIGLU_ALPHA * gate))
        y = jnp.dot(act.astype(MXU_DTYPE), wd_lp[...], preferred_element_type=F32) + bd_ref[...]
        y_ref[...] = _pack_halves(y)

    @pl.when(n_valid == 0)
    def _():
        y_ref[...] = jnp.zeros_like(y_ref)


def _experts(layer, block_e, valid, x_buf, w_gu, b_gu, w_down, b_down):
    n_slots, half = x_buf.shape
    _, n_e, d, d2 = w_gu.shape
    d_e = w_down.shape[2]
    n_blocks = n_slots // EXPERT_BLOCK
    n_layers = w_gu.shape[0]
    return pl.pallas_call(
        _expert_kernel,
        grid_spec=pltpu.PrefetchScalarGridSpec(
            num_scalar_prefetch=2,
            grid=(n_blocks,),
            in_specs=[
                pl.BlockSpec((EXPERT_BLOCK, half), lambda i, be, nu: (i, 0)),
                pl.BlockSpec((None, None, d, d2), lambda i, be, nu: (layer, be[i], 0, 0)),
                pl.BlockSpec((None, None, 1, d2), lambda i, be, nu: (layer, be[i], 0, 0)),
                pl.BlockSpec((None, None, d_e, d), lambda i, be, nu: (layer, be[i], 0, 0)),
                pl.BlockSpec((None, None, 1, d), lambda i, be, nu: (layer, be[i], 0, 0)),
            ],
            out_specs=pl.BlockSpec((EXPERT_BLOCK, half), lambda i, be, nu: (i, 0)),
            scratch_shapes=[pltpu.VMEM((d, d2), MXU_DTYPE), pltpu.VMEM((d_e, d), MXU_DTYPE)],
        ),
        out_shape=jax.ShapeDtypeStruct((n_slots, half), jnp.uint32),
        compiler_params=_params(1),
        name="moe_experts",
    )(block_e, valid, x_buf, w_gu, b_gu.reshape(n_layers, n_e, 1, d2), w_down,
      b_down.reshape(n_layers, n_e, 1, d))


def _combine_kernel(x1_ref, ada_ref, gate_ref, lng_ref, lnb_ref, *rest):
    y_refs, o_ref = rest[:TOP_K], rest[-1]
    gates = gate_ref[...]
    y_lo = y_hi = None
    for k in range(TOP_K):
        lo, hi = _unpack_halves(y_refs[k][...])
        g = gates[:, k:k + 1]
        y_lo = lo * g if y_lo is None else y_lo + lo * g
        y_hi = hi * g if y_hi is None else y_hi + hi * g
    y = jnp.concatenate([y_lo, y_hi], axis=1)
    gate2 = ada_ref[5:6, :]
    o_ref[...] = _layer_norm(DEEPNORM_ALPHA * x1_ref[...] + (1.0 + gate2) * y,
                             lng_ref[...], lnb_ref[...])


def _combine(x1, ada_l, gates_tk, ln_g, ln_b, y_rows, seq, part, n_parts, earlier):
    t, d = x1.shape
    tm = min(MOVE_TILE, seq)
    tiles_per_seq = seq // tm
    steps = t // tm // n_parts
    first = part * steps
    tok = lambda i: (first + i, 0)
    in_specs = [
        pl.BlockSpec((tm, d), tok),
        pl.BlockSpec((None, 6, d), lambda i: ((first + i) // tiles_per_seq, 0, 0)),
        pl.BlockSpec((tm, TOP_K), tok),
        pl.BlockSpec((1, d), lambda i: (0, 0)),
        pl.BlockSpec((1, d), lambda i: (0, 0)),
    ] + [
        pl.BlockSpec((tm, d // 2), lambda i, k=k: (k * steps + i, 0)) for k in range(TOP_K)
    ]
    args = [x1, ada_l, gates_tk, ln_g, ln_b] + [y_rows] * TOP_K
    aliases = {}
    if earlier is not None:
        in_specs.append(pl.BlockSpec(memory_space=pl.ANY))
        aliases = {len(args): 0}
        args.append(earlier)
    return pl.pallas_call(
        _combine_kernel,
        grid=(steps,),
        in_specs=in_specs,
        out_specs=pl.BlockSpec((tm, d), tok),
        out_shape=jax.ShapeDtypeStruct((t, d), F32),
        input_output_aliases=aliases,
        compiler_params=_params(1),
        name="moe_combine_ln",
    )(*args)


def _split_w_in(w_in):
    d = w_in.shape[0]
    sizes = (ATT_WIDTH, ATT_WIDTH, ATT_WIDTH, N_HEADS, ATT_WIDTH, ATT_WIDTH, ATT_WIDTH,
             LRU_WIDTH, LRU_WIDTH)
    bounds = [sum(sizes[:n]) for n in range(len(sizes) + 1)]
    qa, ka, va, fa, qb, kb, vb, xc, gc = (
        w_in[:, bounds[n]:bounds[n + 1]] for n in range(len(sizes)))
    forget = jnp.concatenate([fa, jnp.zeros((d, LANES - N_HEADS), w_in.dtype)], axis=1)
    w_main = jnp.concatenate([qa, ka, qb, kb, xc, gc, forget], axis=1).astype(MXU_DTYPE)
    w_vt = jnp.concatenate([va, vb], axis=1).T.astype(MXU_DTYPE)
    return w_main, w_vt


def _block_diag(w):
    n, c, dd = w.shape
    eye = jnp.eye(n, dtype=w.dtype)
    return (eye[:, None, :, None] * w[:, :, None, :]).reshape(n * c, n * dd).astype(MXU_DTYPE)


def kernel(x, c, w_ada, b_ada, ln1_g, ln1_b, w_in, b_f, conv_w, conv_b, lru_wa, lru_ba, lru_wx,
           lru_bx, lru_lambda, w_gate, b_gate, w_pa, w_pb, w_pc, w_o, ln2_g, ln2_b, w_router,
           b_router, w_gu, b_gu, w_down, b_down):
    n_batch, seq, d = x.shape
    t = n_batch * seq
    n_layers = w_ada.shape[0]
    n_blocks = (t * TOP_K) // EXPERT_BLOCK + N_EXPERTS
    n_slots = n_blocks * EXPERT_BLOCK
    att_tile = min(ATT_TILE, seq)
    vec = lambda a: a.reshape(1, -1)

    ada = _ada(c, w_ada, b_ada).reshape(n_layers, n_batch, 6, d)
    x2d = x.reshape(t, d)
    for l in range(n_layers):
        ada_l = ada[l]
        bf_pad = jnp.concatenate([b_f[l], jnp.zeros((LANES - N_HEADS,), F32)]).reshape(1, LANES)
        w_main, w_vt = _split_w_in(w_in[l])
        qa, ka, qb, kb, xc, gc, vat, vbt, fcol, frow = _inproj(
            x2d, ada_l, w_main, w_vt, bf_pad, seq)
        frow4 = frow.reshape(n_batch, N_HEADS, seq // att_tile, att_tile)
        o_a = _fox(qa, ka, vat, fcol, frow4, seq)
        o_b = _sb(qb, kb, vbt, seq)
        o_c = _lru(xc, gc, conv_w[l], vec(conv_b[l]), _block_diag(lru_wa[l]), vec(lru_ba[l]),
                   _block_diag(lru_wx[l]), vec(lru_bx[l]), vec(lru_lambda[l]), seq)
        x1, h2, logits_t = _merge(
            x2d, ada_l, o_a, o_b, o_c, w_gate[l].astype(MXU_DTYPE), vec(b_gate[l]),
            w_pa[l].astype(MXU_DTYPE), w_pb[l].astype(MXU_DTYPE), w_pc[l].astype(MXU_DTYPE),
            w_o[l].astype(MXU_DTYPE), vec(ln1_g[l]), vec(ln1_b[l]),
            w_router[l].T.astype(MXU_DTYPE), b_router[l].reshape(N_EXPERTS, 1), seq)

        eid, gates, pos, cnt = _route(logits_t)
        counts = cnt[:, 0]
        padded = (counts + EXPERT_BLOCK - 1) // EXPERT_BLOCK * EXPERT_BLOCK
        pad_end = jnp.cumsum(padded)
        pad_start = (pad_end - padded).astype(jnp.int32)
        block_first = (jnp.arange(n_blocks) * EXPERT_BLOCK)[:, None]
        block_e = jnp.minimum(jnp.sum(pad_end[None, :] <= block_first, axis=1),
                              N_EXPERTS - 1).astype(jnp.int32)
        valid = jnp.clip(counts[block_e] + pad_start[block_e] - block_first[:, 0],
                         0, EXPERT_BLOCK).astype(jnp.int32)
        dest = _dest(pad_start, eid, pos)

        x_buf = _sc_scatter_rows(h2, dest.reshape(-1), n_slots)
        y_buf = _experts(l, block_e, valid, x_buf, w_gu, b_gu, w_down, b_down)
        t_part = t // COMBINE_PARTS
        x2d = None
        for part in range(COMBINE_PARTS):
            idx = dest[:, part * t_part:(part + 1) * t_part].reshape(-1)
            y_rows = _sc_gather_rows(y_buf, idx)
            x2d = _combine(x1, ada_l, gates.T, vec(ln2_g[l]), vec(ln2_b[l]), y_rows, seq,
                           part, COMBINE_PARTS, x2d)
    return x2d.reshape(n_batch, seq, d)
```

```python
import functools

import jax
import jax.numpy as jnp
from jax import lax
from jax.experimental import pallas as pl
from jax.experimental.pallas import tpu as pltpu
from jax.experimental.pallas import tpu_sc as plsc

D_MODEL = 1024
DEPTH = 2
HEAD_DIM = 64
N_HEADS = 8
ATT_WIDTH = N_HEADS * HEAD_DIM
LRU_WIDTH = D_MODEL
LRU_BLOCKS = 16
CONV_WIDTH = 4
LRU_C = 8.0
N_EXPERTS = 32
TOP_K = 4
SWIGLU_LIMIT = 7.0
SWIGLU_ALPHA = 1.702
LN_EPS = 1e-5
DEEPNORM_ALPHA = (2.0 * DEPTH) ** 0.25
ATT_SCALE = HEAD_DIM ** -0.5
LOG2_E = 1.4426950408889634

LANES = 128
MXU_DTYPE = jnp.bfloat16
F32 = jnp.float32

ROW_TILE = 512
FOX_TILE = 512
SB_TILE = 256
ATT_GROUP = 8
LRU_TILE = 256
ROUTE_TILE = 512
EXPERT_BLOCK = 256
MOVE_TILE = 256
COMBINE_PARTS = 4
SC_CORES = 2
SC_SUBCORES = 16
SC_CHUNK = 64
VMEM_LIMIT = 56 * 1024 * 1024


def _params(n_axes, vmem=VMEM_LIMIT):
    return pltpu.CompilerParams(
        dimension_semantics=("arbitrary",) * n_axes, vmem_limit_bytes=vmem)


def _log_sigmoid(x):
    return jnp.minimum(x, 0.0) - jnp.log1p(jnp.exp(-jnp.abs(x)))


def _softplus(x):
    return jnp.maximum(x, 0.0) + jnp.log1p(jnp.exp(-jnp.abs(x)))


def _layer_norm(v, g, b):
    mu = jnp.mean(v, axis=-1, keepdims=True)
    d = v - mu
    var = jnp.mean(d * d, axis=-1, keepdims=True)
    return d * lax.rsqrt(var + LN_EPS) * g + b


def _pack_halves(x):
    n = x.shape[1] // 2
    bits = lambda v: lax.bitcast_convert_type(v.astype(jnp.bfloat16).astype(F32), jnp.uint32)
    return (bits(x[:, :n]) >> 16) | bits(x[:, n:])


def _unpack_halves(p):
    lo = lax.bitcast_convert_type(p << 16, F32)
    hi = lax.bitcast_convert_type(p & jnp.uint32(0xFFFF0000), F32)
    return lo, hi


def _split3(x):
    hi = x.astype(MXU_DTYPE)
    r1 = x - hi.astype(F32)
    mid = r1.astype(MXU_DTYPE)
    lo = (r1 - mid.astype(F32)).astype(MXU_DTYPE)
    return hi, mid, lo


def _ada_kernel(c_ref, w_ref, b_ref, o_ref):
    c = c_ref[...]
    cond = c * jax.nn.sigmoid(c)
    o_ref[...] = jnp.dot(cond.astype(MXU_DTYPE), w_ref[...].astype(MXU_DTYPE),
                         preferred_element_type=F32) + b_ref[...]


def _ada(c, w_ada, b_ada):
    n_layers, d, n6 = w_ada.shape
    b = c.shape[0]
    tn = n6 // 4
    return pl.pallas_call(
        _ada_kernel,
        grid=(n_layers, n6 // tn),
        in_specs=[
            pl.BlockSpec((b, d), lambda l, j: (0, 0)),
            pl.BlockSpec((None, d, tn), lambda l, j: (l, 0, j)),
            pl.BlockSpec((None, 1, tn), lambda l, j: (l, 0, j)),
        ],
        out_specs=pl.BlockSpec((None, b, tn), lambda l, j: (l, 0, j)),
        out_shape=jax.ShapeDtypeStruct((n_layers, b, n6), F32),
        compiler_params=_params(2),
        name="ada",
    )(c, w_ada, b_ada.reshape(n_layers, 1, n6))


def _nt_dot(a, b):
    return lax.dot_general(a, b, (((1,), (1,)), ((), ())), preferred_element_type=F32)


def _inproj_kernel(x_ref, ada_ref, w_ref, wvt_ref, bf_ref, tri_ref,
                   qa_ref, ka_ref, qb_ref, kb_ref, xc_ref, gc_ref, vat_ref, vbt_ref,
                   fcol_ref, frow_ref, carry_ref, *, tiles_per_seq):
    i = pl.program_id(0)
    tm = x_ref.shape[0]
    shift = ada_ref[0:1, :]
    scale = ada_ref[1:2, :]
    h = (x_ref[...] * (1.0 + scale) + shift).astype(MXU_DTYPE)
    off = 0
    for ref in (qa_ref, ka_ref, qb_ref, kb_ref, xc_ref, gc_ref):
        width = ref.shape[1]
        ref[...] = jnp.dot(h, w_ref[:, off:off + width],
                           preferred_element_type=F32).astype(ref.dtype)
        off += width
    vt = _nt_dot(wvt_ref[...], h).astype(MXU_DTYPE)
    for ref, rows in ((vat_ref, slice(0, ATT_WIDTH)), (vbt_ref, slice(ATT_WIDTH, 2 * ATT_WIDTH))):
        chunk = ref.shape[2]
        for n in range(tm // chunk):
            ref[n] = vt[rows, n * chunk:(n + 1) * chunk]
    zf = jnp.dot(h, w_ref[:, off:off + LANES], preferred_element_type=F32) + bf_ref[...]
    logf = _log_sigmoid(zf)

    @pl.when(i % tiles_per_seq == 0)
    def _():
        carry_ref[...] = jnp.zeros_like(carry_ref)

    tri = tri_ref[...]
    f_cum = carry_ref[0:1, :]
    for piece in _split3(logf):
        f_cum = f_cum + jnp.dot(tri, piece, preferred_element_type=F32)
    carry_ref[0:1, :] = f_cum[tm - 1:tm, :]
    fcol_ref[...] = f_cum
    frow_ref[...] = f_cum.T[0:N_HEADS, :]


def _inproj(x2d, ada_l, w_main, w_vt, bf_pad, seq):
    t, d = x2d.shape
    tm = min(ROW_TILE, seq)
    tiles_per_seq = seq // tm
    n_batch = t // seq
    tri = (lax.broadcasted_iota(jnp.int32, (tm, tm), 1)
           <= lax.broadcasted_iota(jnp.int32, (tm, tm), 0)).astype(MXU_DTYPE)
    row = lambda i: (i, 0)
    att = jax.ShapeDtypeStruct((t, ATT_WIDTH), MXU_DTYPE)
    chunks = [min(tile, seq) for tile in (FOX_TILE, SB_TILE)]
    att_t = [jax.ShapeDtypeStruct((t // c, ATT_WIDTH, c), MXU_DTYPE) for c in chunks]
    wide = jax.ShapeDtypeStruct((t, LRU_WIDTH), F32)
    att_spec = pl.BlockSpec((tm, ATT_WIDTH), row)
    att_t_spec = [pl.BlockSpec((tm // c, ATT_WIDTH, c), lambda i: (i, 0, 0)) for c in chunks]
    wide_spec = pl.BlockSpec((tm, LRU_WIDTH), row)
    return pl.pallas_call(
        functools.partial(_inproj_kernel, tiles_per_seq=tiles_per_seq),
        grid=(t // tm,),
        in_specs=[
            pl.BlockSpec((tm, d), row),
            pl.BlockSpec((None, 6, d), lambda i: (i // tiles_per_seq, 0, 0)),
            pl.BlockSpec(w_main.shape, lambda i: (0, 0)),
            pl.BlockSpec(w_vt.shape, lambda i: (0, 0)),
            pl.BlockSpec((1, LANES), lambda i: (0, 0)),
            pl.BlockSpec((tm, tm), lambda i: (0, 0)),
        ],
        out_specs=[att_spec] * 4 + [wide_spec] * 2 + att_t_spec + [
            pl.BlockSpec((tm, LANES), row),
            pl.BlockSpec((None, N_HEADS, tm),
                         lambda i: (i // tiles_per_seq, 0, i % tiles_per_seq)),
        ],
        out_shape=[att] * 4 + [wide] * 2 + att_t + [
            jax.ShapeDtypeStruct((t, LANES), F32),
            jax.ShapeDtypeStruct((n_batch, N_HEADS, seq), F32),
        ],
        scratch_shapes=[pltpu.VMEM((8, LANES), F32)],
        compiler_params=_params(1),
        name="inproj",
    )(x2d, ada_l, w_main, w_vt, bf_pad, tri)


def _head_query(q2, c):
    lane = lax.broadcasted_iota(jnp.int32, (1, LANES), 1)
    in_head = (lane >= c * HEAD_DIM) & (lane < (c + 1) * HEAD_DIM)
    return jnp.where(in_head, q2, jnp.zeros_like(q2)) * ATT_SCALE


def _pair_cols(head):
    return slice((head // 2) * LANES, (head // 2 + 1) * LANES)


def _store_heads(o_ref, heads, outs_t):
    sub = lax.broadcasted_iota(jnp.int32, (LANES, 1), 0)
    for n in range(0, len(heads), 2):
        pair_t = jnp.where(sub < HEAD_DIM, outs_t[n], outs_t[n + 1])
        o_ref[:, _pair_cols(heads[n])] = pair_t.T.astype(o_ref.dtype)


def _fox_kernel(q_ref, k_ref, vt_ref, fcol_ref, frow_ref, o_ref):
    i = pl.program_id(1)
    tq = q_ref.shape[0]
    key = lax.broadcasted_iota(jnp.int32, (tq, tq), 0)
    qry = lax.broadcasted_iota(jnp.int32, (tq, tq), 1)
    causal = key <= qry
    for g in range(N_HEADS // ATT_GROUP):
        heads = list(range(g * ATT_GROUP, (g + 1) * ATT_GROUP))
        qms = [_head_query(q_ref[:, _pair_cols(hd)], hd % 2) for hd in heads]
        fqs = [frow_ref[hd, pl.ds(i, 1), :] for hd in heads]

        def qk(j, heads=heads, qms=qms):
            start = pl.multiple_of(j * tq, tq)
            return tuple(_nt_dot(k_ref[pl.ds(start, tq), _pair_cols(hd)], qms[n])
                         for n, hd in enumerate(heads))

        def block(j, scores, carry, masked, heads=heads, fqs=fqs):
            start = pl.multiple_of(j * tq, tq)
            stats, probs = [], []
            for n, hd in enumerate(heads):
                m, l, _ = carry[n]
                fk = fcol_ref[pl.ds(start, tq), hd:hd + 1]
                s = scores[n] - fk
                if masked:
                    s = jnp.where(causal, s, -jnp.inf)
                m_new = jnp.maximum(m, jnp.max(s, axis=0, keepdims=True) + fqs[n])
                alpha = jnp.exp(m - m_new)
                pm = jnp.exp(s + (fqs[n] - m_new))
                l = alpha * l + jnp.sum(pm, axis=0, keepdims=True)
                stats.append((m_new, l, alpha))
                probs.append(pm.astype(MXU_DTYPE))
            pvs = [jnp.dot(vt_ref[j, _pair_cols(hd), :], probs[n],
                           preferred_element_type=F32) for n, hd in enumerate(heads)]
            return tuple((stats[n][0], stats[n][1], stats[n][2] * carry[n][2] + pvs[n])
                         for n in range(len(heads)))

        init = tuple((jnp.full((1, tq), -jnp.inf, F32), jnp.zeros((1, tq), F32),
                      jnp.zeros((LANES, tq), F32)) for _ in heads)
        carry = lax.fori_loop(
            0, i, lambda j, cr, qk=qk, block=block: block(j, qk(j), cr, False), init)
        carry = block(i, qk(i), carry, True)
        _store_heads(o_ref, heads, [acc / l for _, l, acc in carry])


def _att_specs(t, seq, tile):
    n_batch = t // seq
    tq = min(tile, seq)
    nq = seq // tq
    q_spec = pl.BlockSpec((tq, ATT_WIDTH), lambda b, i: (b * nq + i, 0))
    k_spec = pl.BlockSpec((seq, ATT_WIDTH), lambda b, i: (b, 0))
    vt_spec = pl.BlockSpec((nq, ATT_WIDTH, tq), lambda b, i: (b, 0, 0))
    return n_batch, tq, nq, q_spec, k_spec, vt_spec


def _fox(q, k, vt, fcol, frow4, seq):
    t = q.shape[0]
    n_batch, tq, nq, q_spec, k_spec, vt_spec = _att_specs(t, seq, FOX_TILE)
    return pl.pallas_call(
        _fox_kernel,
        grid=(n_batch, nq),
        in_specs=[
            q_spec, k_spec, vt_spec,
            pl.BlockSpec((seq, LANES), lambda b, i: (b, 0)),
            pl.BlockSpec((None, N_HEADS, nq, tq), lambda b, i: (b, 0, 0, 0)),
        ],
        out_specs=q_spec,
        out_shape=jax.ShapeDtypeStruct((t, ATT_WIDTH), MXU_DTYPE),
        compiler_params=_params(2),
        name="fox_attention",
    )(q, k, vt, fcol, frow4)


def _sb_kernel(q_ref, k_ref, vt_ref, suf_ref, o_ref):
    i = pl.program_id(1)
    tq = q_ref.shape[0]
    key = lax.broadcasted_iota(jnp.int32, (tq, tq), 0)
    qry = lax.broadcasted_iota(jnp.int32, (tq, tq), 1)
    strict = key < qry
    suf = suf_ref[...]
    for g in range(N_HEADS // ATT_GROUP):
        heads = list(range(g * ATT_GROUP, (g + 1) * ATT_GROUP))
        qms = [_head_query(q_ref[:, _pair_cols(hd)], hd % 2) for hd in heads]

        def qk(j, heads=heads, qms=qms):
            start = pl.multiple_of(j * tq, tq)
            return tuple(_nt_dot(k_ref[pl.ds(start, tq), _pair_cols(hd)], qms[n])
                         for n, hd in enumerate(heads))

        def block(j, zs, carry, masked, heads=heads):
            log_betas, splits, laters = [], [], []
            for n in range(len(heads)):
                z = zs[n]
                sp = jnp.maximum(z, 0.0) + jnp.log(1.0 + jnp.exp2(jnp.abs(z) * -LOG2_E))
                log_betas.append(z - sp)
                if masked:
                    sp = jnp.where(strict, sp, 0.0)
                splits.append(sp.astype(MXU_DTYPE))
                laters.append(carry[n][0] + jnp.sum(sp, axis=0, keepdims=True))
            afters = [jnp.dot(suf, splits[n], preferred_element_type=F32)
                      for n in range(len(heads))]
            ws = []
            for n in range(len(heads)):
                w = jnp.exp(log_betas[n] - jnp.maximum(afters[n] + carry[n][0], 0.0))
                if masked:
                    w = jnp.where(strict, w, 0.0)
                ws.append(w.astype(MXU_DTYPE))
            return tuple(
                (laters[n], carry[n][1] + jnp.dot(vt_ref[j, _pair_cols(hd), :], ws[n],
                                                  preferred_element_type=F32))
                for n, hd in enumerate(heads))

        init = tuple((jnp.zeros((1, tq), F32), jnp.zeros((LANES, tq), F32)) for _ in heads)

        carry = block(i, qk(i), init, True)
        carry = lax.fori_loop(
            0, i, lambda s, cr, qk=qk, block=block: block(i - 1 - s, qk(i - 1 - s), cr, False),
            carry)
        _store_heads(o_ref, heads, [acc for _, acc in carry])


def _sb(q, k, vt, seq):
    t = q.shape[0]
    n_batch, tq, nq, q_spec, k_spec, vt_spec = _att_specs(t, seq, SB_TILE)
    suf = (lax.broadcasted_iota(jnp.int32, (tq, tq), 1)
           > lax.broadcasted_iota(jnp.int32, (tq, tq), 0)).astype(MXU_DTYPE)
    return pl.pallas_call(
        _sb_kernel,
        grid=(n_batch, nq),
        in_specs=[q_spec, k_spec, vt_spec, pl.BlockSpec((tq, tq), lambda b, i: (0, 0))],
        out_specs=q_spec,
        out_shape=jax.ShapeDtypeStruct((t, ATT_WIDTH), MXU_DTYPE),
        compiler_params=_params(2),
        name="sb_attention",
    )(q, k, vt, suf)


def _lru_kernel(xc_ref, gc_ref, cw_ref, cb_ref, wa_ref, ba_ref, wx_ref, bx_ref, lam_ref,
                o_ref, ext_ref, a_ref, u_ref, h_ref, state_ref):
    s_idx = pl.program_id(1)
    ts = xc_ref.shape[0]
    halo = 8

    @pl.when(s_idx == 0)
    def _():
        ext_ref[0:halo, :] = jnp.zeros((halo, LRU_WIDTH), F32)
        state_ref[...] = jnp.zeros_like(state_ref)

    ext_ref[halo:halo + ts, :] = xc_ref[...]
    xconv = cb_ref[...]
    for j in range(CONV_WIDTH):
        lo = halo - (CONV_WIDTH - 1) + j
        xconv = xconv + cw_ref[j:j + 1, :] * ext_ref[lo:lo + ts, :]
    ext_ref[0:halo, :] = ext_ref[ts:ts + halo, :]

    xb = xconv.astype(MXU_DTYPE)
    r = jax.nn.sigmoid(jnp.dot(xb, wa_ref[...], preferred_element_type=F32) + ba_ref[...])
    ig = jax.nn.sigmoid(jnp.dot(xb, wx_ref[...], preferred_element_type=F32) + bx_ref[...])
    log_a = (-LRU_C * _softplus(-lam_ref[...])) * r
    a = jnp.exp(log_a)
    a_ref[...] = a
    u_ref[...] = jnp.sqrt(1.0 - a * a) * (ig * xconv)

    def step(t, h):
        h = a_ref[pl.ds(t, 1), :] * h + u_ref[pl.ds(t, 1), :]
        h_ref[pl.ds(t, 1), :] = h
        return h

    state_ref[0:1, :] = lax.fori_loop(0, ts, step, state_ref[0:1, :], unroll=8)
    o_ref[...] = (h_ref[...] * jax.nn.gelu(gc_ref[...])).astype(o_ref.dtype)


def _lru(xc, gc, conv_w, conv_b, wa_bd, ba, wx_bd, bx, lam, seq):
    t, w = xc.shape
    n_batch = t // seq
    ts = min(LRU_TILE, seq)
    ns = seq // ts
    row_spec = pl.BlockSpec((ts, w), lambda b, s: (b * ns + s, 0))
    vec_spec = pl.BlockSpec((1, w), lambda b, s: (0, 0))
    mat_spec = pl.BlockSpec((w, w), lambda b, s: (0, 0))
    return pl.pallas_call(
        _lru_kernel,
        grid=(n_batch, ns),
        in_specs=[row_spec, row_spec,
                  pl.BlockSpec((CONV_WIDTH, w), lambda b, s: (0, 0)), vec_spec,
                  mat_spec, vec_spec, mat_spec, vec_spec, vec_spec],
        out_specs=row_spec,
        out_shape=jax.ShapeDtypeStruct((t, w), MXU_DTYPE),
        scratch_shapes=[pltpu.VMEM((ts + 8, w), F32), pltpu.VMEM((ts, w), F32),
                        pltpu.VMEM((ts, w), F32), pltpu.VMEM((ts, w), F32),
                        pltpu.VMEM((8, w), F32)],
        compiler_params=_params(2),
        name="rg_lru",
    )(xc, gc, conv_w, conv_b, wa_bd, ba, wx_bd, bx, lam)


def _merge_kernel(x_ref, ada_ref, oa_ref, ob_ref, oc_ref, wg_ref, bg_ref, wpa_ref, wpb_ref,
                  wpc_ref, wo_ref, lng_ref, lnb_ref, wr_ref, br_ref,
                  x1_ref, h2_ref, logit_ref):
    d = x_ref.shape[1]
    x = x_ref[...]
    shift1, scale1, gate1 = ada_ref[0:1, :], ada_ref[1:2, :], ada_ref[2:3, :]
    shift2, scale2 = ada_ref[3:4, :], ada_ref[4:5, :]
    h = (x * (1.0 + scale1) + shift1).astype(MXU_DTYPE)
    merged = None
    for n, (o_ref, w_ref) in enumerate(((oa_ref, wpa_ref), (ob_ref, wpb_ref), (oc_ref, wpc_ref))):
        cs = slice(n * d, (n + 1) * d)
        g = jax.nn.sigmoid(jnp.dot(h, wg_ref[:, cs], preferred_element_type=F32) + bg_ref[:, cs])
        term = g * jnp.dot(o_ref[...], w_ref[...], preferred_element_type=F32)
        merged = term if merged is None else merged + term
    y = jnp.dot(merged.astype(MXU_DTYPE), wo_ref[...], preferred_element_type=F32)
    x1 = _layer_norm(DEEPNORM_ALPHA * x + (1.0 + gate1) * y, lng_ref[...], lnb_ref[...])
    x1_ref[...] = x1
    h2 = x1 * (1.0 + scale2) + shift2
    h2_ref[...] = _pack_halves(h2)
    logit_ref[...] = _nt_dot(wr_ref[...], h2.astype(MXU_DTYPE)) + br_ref[...]


def _merge(x2d, ada_l, o_a, o_b, o_c, w_gate, b_gate, w_pa, w_pb, w_pc, w_o, ln_g, ln_b,
           w_router_t, b_router, seq):
    t, d = x2d.shape
    tm = min(ROW_TILE, seq)
    tiles_per_seq = seq // tm
    row = lambda i: (i, 0)
    whole = lambda a: pl.BlockSpec(a.shape, lambda i: (0,) * a.ndim)
    return pl.pallas_call(
        _merge_kernel,
        grid=(t // tm,),
        in_specs=[
            pl.BlockSpec((tm, d), row),
            pl.BlockSpec((None, 6, d), lambda i: (i // tiles_per_seq, 0, 0)),
            pl.BlockSpec((tm, ATT_WIDTH), row), pl.BlockSpec((tm, ATT_WIDTH), row),
            pl.BlockSpec((tm, LRU_WIDTH), row),
            whole(w_gate), whole(b_gate), whole(w_pa), whole(w_pb), whole(w_pc), whole(w_o),
            whole(ln_g), whole(ln_b), whole(w_router_t), whole(b_router),
        ],
        out_specs=[pl.BlockSpec((tm, d), row), pl.BlockSpec((tm, d // 2), row),
                   pl.BlockSpec((N_EXPERTS, tm), lambda i: (0, i))],
        out_shape=[jax.ShapeDtypeStruct((t, d), F32),
                   jax.ShapeDtypeStruct((t, d // 2), jnp.uint32),
                   jax.ShapeDtypeStruct((N_EXPERTS, t), F32)],
        compiler_params=_params(1),
        name="merge_outproj_ln",
    )(x2d, ada_l, o_a, o_b, o_c, w_gate, b_gate, w_pa, w_pb, w_pc, w_o, ln_g, ln_b,
      w_router_t, b_router)


def _route_kernel(logit_ref, tri_ref, eid_ref, gate_ref, pos_ref, cnt_ref, carry_ref):
    i = pl.program_id(0)
    tr = logit_ref.shape[1]

    @pl.when(i == 0)
    def _():
        carry_ref[...] = jnp.zeros_like(carry_ref)

    erow = lax.broadcasted_iota(jnp.int32, (N_EXPERTS, tr), 0)
    cur = logit_ref[...]
    vals, ids = [], []
    for _ in range(TOP_K):
        m = jnp.max(cur, axis=0, keepdims=True)
        idx = jnp.min(jnp.where(cur == m, erow, N_EXPERTS), axis=0, keepdims=True)
        vals.append(m)
        ids.append(idx)
        cur = jnp.where(erow == idx, -jnp.inf, cur)
    exps = [jnp.exp(v - vals[0]) for v in vals]
    denom = exps[0] + exps[1] + exps[2] + exps[3]
    chosen = jnp.zeros((N_EXPERTS, tr), F32)
    for idx in ids:
        chosen = chosen + jnp.where(erow == idx, 1.0, 0.0)
    prefix = (jnp.dot(chosen.astype(MXU_DTYPE), tri_ref[...], preferred_element_type=F32)
              + carry_ref[:, 0:1])
    for k in range(TOP_K):
        eid_ref[k:k + 1, :] = ids[k]
        gate_ref[k:k + 1, :] = exps[k] / denom
        pos_ref[k:k + 1, :] = jnp.sum(jnp.where(erow == ids[k], prefix, 0.0), axis=0,
                                      keepdims=True).astype(jnp.int32)
    carry_ref[...] = carry_ref[...] + jnp.sum(chosen, axis=1, keepdims=True)
    cnt_ref[...] = carry_ref[...].astype(jnp.int32)


def _route(logits_t):
    n_e, t = logits_t.shape
    tr = min(ROUTE_TILE, t)
    tri = (lax.broadcasted_iota(jnp.int32, (tr, tr), 0)
           < lax.broadcasted_iota(jnp.int32, (tr, tr), 1)).astype(MXU_DTYPE)
    tok = lambda i: (0, i)
    return pl.pallas_call(
        _route_kernel,
        grid=(t // tr,),
        in_specs=[pl.BlockSpec((n_e, tr), tok), pl.BlockSpec((tr, tr), lambda i: (0, 0))],
        out_specs=[pl.BlockSpec((TOP_K, tr), tok)] * 3
        + [pl.BlockSpec((n_e, LANES), lambda i: (0, 0))],
        out_shape=[jax.ShapeDtypeStruct((TOP_K, t), jnp.int32),
                   jax.ShapeDtypeStruct((TOP_K, t), F32),
                   jax.ShapeDtypeStruct((TOP_K, t), jnp.int32),
                   jax.ShapeDtypeStruct((n_e, LANES), jnp.int32)],
        scratch_shapes=[pltpu.VMEM((n_e, LANES), F32)],
        compiler_params=_params(1),
        name="route_topk",
    )(logits_t, tri)


def _dest_kernel(start_ref, eid_ref, pos_ref, dest_ref):
    eid = eid_ref[...]
    dest = pos_ref[...]
    for e in range(N_EXPERTS):
        dest = jnp.where(eid == e, dest + start_ref[e], dest)
    dest_ref[...] = dest


def _dest(pad_start, eid, pos):
    k, t = eid.shape
    tr = min(ROUTE_TILE, t)
    spec = pl.BlockSpec((k, tr), lambda i, s: (0, i))
    return pl.pallas_call(
        _dest_kernel,
        grid_spec=pltpu.PrefetchScalarGridSpec(
            num_scalar_prefetch=1, grid=(t // tr,), in_specs=[spec, spec], out_specs=spec),
        out_shape=jax.ShapeDtypeStruct((k, t), jnp.int32),
        compiler_params=_params(1),
        name="route_dest",
    )(pad_start, eid, pos)


def _sc_mesh():
    return plsc.VectorSubcoreMesh(core_axis_name="c", subcore_axis_name="s")


def _sc_worker():
    return lax.axis_index("s") * SC_CORES + lax.axis_index("c")


def _sc_scatter_rows(rows, idx, n_out):
    t, width = rows.shape
    per_worker = t // (SC_CORES * SC_SUBCORES)
    n_chunks = per_worker // SC_CHUNK

    @functools.partial(
        pl.kernel, mesh=_sc_mesh(),
        out_type=jax.ShapeDtypeStruct((n_out, width), rows.dtype),
        scratch_types=[pltpu.VMEM((SC_CHUNK,), jnp.int32),
                       pltpu.VMEM((SC_CHUNK, width), rows.dtype),
                       pltpu.SemaphoreType.DMA],
    )
    def scatter(rows_hbm, idx_hbm, out_hbm, idx_v, rows_v, sem):
        base = _sc_worker() * per_worker

        @pl.loop(0, n_chunks)
        def _(g):
            off = base + g * SC_CHUNK
            pltpu.sync_copy(rows_hbm.at[pl.ds(off, SC_CHUNK)], rows_v)
            for k in range(TOP_K):
                pltpu.sync_copy(idx_hbm.at[pl.ds(k * t + off, SC_CHUNK)], idx_v)
                pltpu.async_copy(rows_v, out_hbm.at[idx_v], sem).wait()

    return scatter(rows, idx)


def _sc_gather_rows(table, idx):
    n_rows = idx.shape[0]
    width = table.shape[1]
    per_worker = n_rows // (SC_CORES * SC_SUBCORES)
    n_chunks = per_worker // SC_CHUNK

    @functools.partial(
        pl.kernel, mesh=_sc_mesh(),
        out_type=jax.ShapeDtypeStruct((n_rows, width), table.dtype),
        scratch_types=[pltpu.VMEM((SC_CHUNK,), jnp.int32),
                       pltpu.VMEM((SC_CHUNK, width), table.dtype),
                       pltpu.SemaphoreType.DMA],
    )
    def gather(table_hbm, idx_hbm, out_hbm, idx_v, rows_v, sem):
        base = _sc_worker() * per_worker

        @pl.loop(0, n_chunks)
        def _(g):
            off = base + g * SC_CHUNK
            pltpu.sync_copy(idx_hbm.at[pl.ds(off, SC_CHUNK)], idx_v)
            pltpu.async_copy(table_hbm.at[idx_v], rows_v, sem).wait()
            pltpu.sync_copy(rows_v, out_hbm.at[pl.ds(off, SC_CHUNK)])

    return gather(table, idx)


def _expert_kernel(be_ref, valid_ref, x_ref, wgu_ref, bgu_ref, wd_ref, bd_ref, y_ref,
                   wgu_lp, wd_lp):
    i = pl.program_id(0)
    d_e = wd_ref.shape[0]
    n_valid = valid_ref[i]

    @pl.when((i == 0) | (be_ref[i] != be_ref[jnp.maximum(i - 1, 0)]))
    def _():
        wgu_lp[...] = wgu_ref[...].astype(MXU_DTYPE)
        wd_lp[...] = wd_ref[...].astype(MXU_DTYPE)

    @pl.when(n_valid > 0)
    def _():
        row = lax.broadcasted_iota(jnp.int32, x_ref.shape, 0)
        lo, hi = _unpack_halves(jnp.where(row < n_valid, x_ref[...], jnp.uint32(0)))
        x = jnp.concatenate([lo, hi], axis=1).astype(MXU_DTYPE)
        gu = jnp.dot(x, wgu_lp[...], preferred_element_type=F32) + bgu_ref[...]
        gate = jnp.minimum(gu[:, :d_e], SWIGLU_LIMIT)
        up = jnp.clip(gu[:, d_e:], -SWIGLU_LIMIT, SWIGLU_LIMIT)
        act = (up + 1.0) * (gate * jax.nn.sigmoid(SWIGLU_ALPHA * gate))
        y = jnp.dot(act.astype(MXU_DTYPE), wd_lp[...], preferred_element_type=F32) + bd_ref[...]
        y_ref[...] = _pack_halves(y)

    @pl.when(n_valid == 0)
    def _():
        y_ref[...] = jnp.zeros_like(y_ref)


def _experts(layer, block_e, valid, x_buf, w_gu, b_gu, w_down, b_down):
    n_slots, half = x_buf.shape
    _, n_e, d, d2 = w_gu.shape
    d_e = w_down.shape[2]
    n_blocks = n_slots // EXPERT_BLOCK
    n_layers = w_gu.shape[0]
    return pl.pallas_call(
        _expert_kernel,
        grid_spec=pltpu.PrefetchScalarGridSpec(
            num_scalar_prefetch=2,
            grid=(n_blocks,),
            in_specs=[
                pl.BlockSpec((EXPERT_BLOCK, half), lambda i, be, nu: (i, 0)),
                pl.BlockSpec((None, None, d, d2), lambda i, be, nu: (layer, be[i], 0, 0)),
                pl.BlockSpec((None, None, 1, d2), lambda i, be, nu: (layer, be[i], 0, 0)),
                pl.BlockSpec((None, None, d_e, d), lambda i, be, nu: (layer, be[i], 0, 0)),
                pl.BlockSpec((None, None, 1, d), lambda i, be, nu: (layer, be[i], 0, 0)),
            ],
            out_specs=pl.BlockSpec((EXPERT_BLOCK, half), lambda i, be, nu: (i, 0)),
            scratch_shapes=[pltpu.VMEM((d, d2), MXU_DTYPE), pltpu.VMEM((d_e, d), MXU_DTYPE)],
        ),
        out_shape=jax.ShapeDtypeStruct((n_slots, half), jnp.uint32),
        compiler_params=_params(1),
        name="moe_experts",
    )(block_e, valid, x_buf, w_gu, b_gu.reshape(n_layers, n_e, 1, d2), w_down,
      b_down.reshape(n_layers, n_e, 1, d))


def _combine_kernel(x1_ref, ada_ref, gate_ref, lng_ref, lnb_ref, *rest):
    y_refs, o_ref = rest[:TOP_K], rest[-1]
    gates = gate_ref[...]
    y_lo = y_hi = None
    for k in range(TOP_K):
        lo, hi = _unpack_halves(y_refs[k][...])
        g = gates[:, k:k + 1]
        y_lo = lo * g if y_lo is None else y_lo + lo * g
        y_hi = hi * g if y_hi is None else y_hi + hi * g
    y = jnp.concatenate([y_lo, y_hi], axis=1)
    gate2 = ada_ref[5:6, :]
    o_ref[...] = _layer_norm(DEEPNORM_ALPHA * x1_ref[...] + (1.0 + gate2) * y,
                             lng_ref[...], lnb_ref[...])


def _combine(x1, ada_l, gates_tk, ln_g, ln_b, y_rows, seq, part, n_parts, earlier):
    t, d = x1.shape
    tm = min(MOVE_TILE, seq)
    tiles_per_seq = seq // tm
    steps = t // tm // n_parts
    first = part * steps
    tok = lambda i: (first + i, 0)
    in_specs = [
        pl.BlockSpec((tm, d), tok),
        pl.BlockSpec((None, 6, d), lambda i: ((first + i) // tiles_per_seq, 0, 0)),
        pl.BlockSpec((tm, TOP_K), tok),
        pl.BlockSpec((1, d), lambda i: (0, 0)),
        pl.BlockSpec((1, d), lambda i: (0, 0)),
    ] + [
        pl.BlockSpec((tm, d // 2), lambda i, k=k: (k * steps + i, 0)) for k in range(TOP_K)
    ]
    args = [x1, ada_l, gates_tk, ln_g, ln_b] + [y_rows] * TOP_K
    aliases = {}
    if earlier is not None:
        in_specs.append(pl.BlockSpec(memory_space=pl.ANY))
        aliases = {len(args): 0}
        args.append(earlier)
    return pl.pallas_call(
        _combine_kernel,
        grid=(steps,),
        in_specs=in_specs,
        out_specs=pl.BlockSpec((tm, d), tok),
        out_shape=jax.ShapeDtypeStruct((t, d), F32),
        input_output_aliases=aliases,
        compiler_params=_params(1),
        name="moe_combine_ln",
    )(*args)


def _split_w_in(w_in):
    d = w_in.shape[0]
    sizes = (ATT_WIDTH, ATT_WIDTH, ATT_WIDTH, N_HEADS, ATT_WIDTH, ATT_WIDTH, ATT_WIDTH,
             LRU_WIDTH, LRU_WIDTH)
    bounds = [sum(sizes[:n]) for n in range(len(sizes) + 1)]
    qa, ka, va, fa, qb, kb, vb, xc, gc = (
        w_in[:, bounds[n]:bounds[n + 1]] for n in range(len(sizes)))
    forget = jnp.concatenate([fa, jnp.zeros((d, LANES - N_HEADS), w_in.dtype)], axis=1)
    w_main = jnp.concatenate([qa, ka, qb, kb, xc, gc, forget], axis=1).astype(MXU_DTYPE)
    w_vt = jnp.concatenate([va, vb], axis=1).T.astype(MXU_DTYPE)
    return w_main, w_vt


def _block_diag(w):
    n, c, dd = w.shape
    eye = jnp.eye(n, dtype=w.dtype)
    return (eye[:, None, :, None] * w[:, :, None, :]).reshape(n * c, n * dd).astype(MXU_DTYPE)


def kernel(x, c, w_ada, b_ada, ln1_g, ln1_b, w_in, b_f, conv_w, conv_b, lru_wa, lru_ba, lru_wx,
           lru_bx, lru_lambda, w_gate, b_gate, w_pa, w_pb, w_pc, w_o, ln2_g, ln2_b, w_router,
           b_router, w_gu, b_gu, w_down, b_down):
    n_batch, seq, d = x.shape
    t = n_batch * seq
    n_layers = w_ada.shape[0]
    n_blocks = (t * TOP_K) // EXPERT_BLOCK + N_EXPERTS
    n_slots = n_blocks * EXPERT_BLOCK
    att_tile = min(FOX_TILE, seq)
    vec = lambda a: a.reshape(1, -1)

    ada = _ada(c, w_ada, b_ada).reshape(n_layers, n_batch, 6, d)
    x2d = x.reshape(t, d)
    for l in range(n_layers):
        ada_l = ada[l]
        bf_pad = jnp.concatenate([b_f[l], jnp.zeros((LANES - N_HEADS,), F32)]).reshape(1, LANES)
        w_main, w_vt = _split_w_in(w_in[l])
        qa, ka, qb, kb, xc, gc, vat, vbt, fcol, frow = _inproj(
            x2d, ada_l, w_main, w_vt, bf_pad, seq)
        frow4 = frow.reshape(n_batch, N_HEADS, seq // att_tile, att_tile)
        o_a = _fox(qa, ka, vat, fcol, frow4, seq)
        o_b = _sb(qb, kb, vbt, seq)
        o_c = _lru(xc, gc, conv_w[l], vec(conv_b[l]), _block_diag(lru_wa[l]), vec(lru_ba[l]),
                   _block_diag(lru_wx[l]), vec(lru_bx[l]), vec(lru_lambda[l]), seq)
        x1, h2, logits_t = _merge(
            x2d, ada_l, o_a, o_b, o_c, w_gate[l].astype(MXU_DTYPE), vec(b_gate[l]),
            w_pa[l].astype(MXU_DTYPE), w_pb[l].astype(MXU_DTYPE), w_pc[l].astype(MXU_DTYPE),
            w_o[l].astype(MXU_DTYPE), vec(ln1_g[l]), vec(ln1_b[l]),
            w_router[l].T.astype(MXU_DTYPE), b_router[l].reshape(N_EXPERTS, 1), seq)

        eid, gates, pos, cnt = _route(logits_t)
        counts = cnt[:, 0]
        padded = (counts + EXPERT_BLOCK - 1) // EXPERT_BLOCK * EXPERT_BLOCK
        pad_end = jnp.cumsum(padded)
        pad_start = (pad_end - padded).astype(jnp.int32)
        block_first = (jnp.arange(n_blocks) * EXPERT_BLOCK)[:, None]
        block_e = jnp.minimum(jnp.sum(pad_end[None, :] <= block_first, axis=1),
                              N_EXPERTS - 1).astype(jnp.int32)
        valid = jnp.clip(counts[block_e] + pad_start[block_e] - block_first[:, 0],
                         0, EXPERT_BLOCK).astype(jnp.int32)
        dest = _dest(pad_start, eid, pos)

        x_buf = _sc_scatter_rows(h2, dest.reshape(-1), n_slots)
        y_buf = _experts(l, block_e, valid, x_buf, w_gu, b_gu, w_down, b_down)
        t_part = t // COMBINE_PARTS
        x2d = None
        for part in range(COMBINE_PARTS):
            idx = dest[:, part * t_part:(part + 1) * t_part].reshape(-1)
            y_rows = _sc_gather_rows(y_buf, idx)
            x2d = _combine(x1, ada_l, gates.T, vec(ln2_g[l]), vec(ln2_b[l]), y_rows, seq,
                           part, COMBINE_PARTS, x2d)
    return x2d.reshape(n_batch, seq, d)
```

```python
import functools

import jax
import jax.numpy as jnp
from jax import lax
from jax.experimental import pallas as pl
from jax.experimental.pallas import tpu as pltpu
from jax.experimental.pallas import tpu_sc as plsc

D_MODEL = 1024
DEPTH = 2
HEAD_DIM = 64
N_HEADS = 8
ATT_WIDTH = N_HEADS * HEAD_DIM
LRU_WIDTH = D_MODEL
LRU_BLOCKS = 16
CONV_WIDTH = 4
LRU_C = 8.0
N_EXPERTS = 32
TOP_K = 4
SWIGLU_LIMIT = 7.0
SWIGLU_ALPHA = 1.702
LN_EPS = 1e-5
DEEPNORM_ALPHA = (2.0 * DEPTH) ** 0.25
ATT_SCALE = HEAD_DIM ** -0.5
LOG2_E = 1.4426950408889634

LANES = 128
MXU_DTYPE = jnp.bfloat16
F32 = jnp.float32

ROW_TILE = 512
FOX_TILE = 512
SB_TILE = 256
ATT_GROUP = 8
LRU_TILE = 256
ROUTE_TILE = 512
EXPERT_BLOCK = 256
MOVE_TILE = 256
COMBINE_PARTS = 4
SC_CORES = 2
SC_SUBCORES = 16
SC_CHUNK = 128
VMEM_LIMIT = 56 * 1024 * 1024


def _params(n_axes, vmem=VMEM_LIMIT):
    return pltpu.CompilerParams(
        dimension_semantics=("arbitrary",) * n_axes, vmem_limit_bytes=vmem)


def _log_sigmoid(x):
    return jnp.minimum(x, 0.0) - jnp.log1p(jnp.exp(-jnp.abs(x)))


def _softplus(x):
    return jnp.maximum(x, 0.0) + jnp.log1p(jnp.exp(-jnp.abs(x)))


def _layer_norm(v, g, b):
    mu = jnp.mean(v, axis=-1, keepdims=True)
    d = v - mu
    var = jnp.mean(d * d, axis=-1, keepdims=True)
    return d * lax.rsqrt(var + LN_EPS) * g + b


def _pack_halves(x):
    n = x.shape[1] // 2
    bits = lambda v: lax.bitcast_convert_type(v.astype(jnp.bfloat16).astype(F32), jnp.uint32)
    return (bits(x[:, :n]) >> 16) | bits(x[:, n:])


def _unpack_halves(p):
    lo = lax.bitcast_convert_type(p << 16, F32)
    hi = lax.bitcast_convert_type(p & jnp.uint32(0xFFFF0000), F32)
    return lo, hi


def _split3(x):
    hi = x.astype(MXU_DTYPE)
    r1 = x - hi.astype(F32)
    mid = r1.astype(MXU_DTYPE)
    lo = (r1 - mid.astype(F32)).astype(MXU_DTYPE)
    return hi, mid, lo


def _ada_kernel(c_ref, w_ref, b_ref, o_ref):
    c = c_ref[...]
    cond = c * jax.nn.sigmoid(c)
    o_ref[...] = jnp.dot(cond.astype(MXU_DTYPE), w_ref[...].astype(MXU_DTYPE),
                         preferred_element_type=F32) + b_ref[...]


def _ada(c, w_ada, b_ada):
    n_layers, d, n6 = w_ada.shape
    b = c.shape[0]
    tn = n6 // 4
    return pl.pallas_call(
        _ada_kernel,
        grid=(n_layers, n6 // tn),
        in_specs=[
            pl.BlockSpec((b, d), lambda l, j: (0, 0)),
            pl.BlockSpec((None, d, tn), lambda l, j: (l, 0, j)),
            pl.BlockSpec((None, 1, tn), lambda l, j: (l, 0, j)),
        ],
        out_specs=pl.BlockSpec((None, b, tn), lambda l, j: (l, 0, j)),
        out_shape=jax.ShapeDtypeStruct((n_layers, b, n6), F32),
        compiler_params=_params(2),
        name="ada",
    )(c, w_ada, b_ada.reshape(n_layers, 1, n6))


def _nt_dot(a, b):
    return lax.dot_general(a, b, (((1,), (1,)), ((), ())), preferred_element_type=F32)


def _inproj_kernel(x_ref, ada_ref, w_ref, wvt_ref, bf_ref, tri_ref,
                   qa_ref, ka_ref, qb_ref, kb_ref, xc_ref, gc_ref, vat_ref, vbt_ref,
                   fcol_ref, frow_ref, carry_ref, *, tiles_per_seq):
    i = pl.program_id(0)
    tm = x_ref.shape[0]
    shift = ada_ref[0:1, :]
    scale = ada_ref[1:2, :]
    h = (x_ref[...] * (1.0 + scale) + shift).astype(MXU_DTYPE)
    off = 0
    for ref in (qa_ref, ka_ref, qb_ref, kb_ref, xc_ref, gc_ref):
        width = ref.shape[1]
        ref[...] = jnp.dot(h, w_ref[:, off:off + width],
                           preferred_element_type=F32).astype(ref.dtype)
        off += width
    vt = _nt_dot(wvt_ref[...], h).astype(MXU_DTYPE)
    for ref, rows in ((vat_ref, slice(0, ATT_WIDTH)), (vbt_ref, slice(ATT_WIDTH, 2 * ATT_WIDTH))):
        chunk = ref.shape[2]
        for n in range(tm // chunk):
            ref[n] = vt[rows, n * chunk:(n + 1) * chunk]
    zf = jnp.dot(h, w_ref[:, off:off + LANES], preferred_element_type=F32) + bf_ref[...]
    logf = _log_sigmoid(zf)

    @pl.when(i % tiles_per_seq == 0)
    def _():
        carry_ref[...] = jnp.zeros_like(carry_ref)

    tri = tri_ref[...]
    f_cum = carry_ref[0:1, :]
    for piece in _split3(logf):
        f_cum = f_cum + jnp.dot(tri, piece, preferred_element_type=F32)
    carry_ref[0:1, :] = f_cum[tm - 1:tm, :]
    fcol_ref[...] = f_cum
    frow_ref[...] = f_cum.T[0:N_HEADS, :]


def _inproj(x2d, ada_l, w_main, w_vt, bf_pad, seq):
    t, d = x2d.shape
    tm = min(ROW_TILE, seq)
    tiles_per_seq = seq // tm
    n_batch = t // seq
    tri = (lax.broadcasted_iota(jnp.int32, (tm, tm), 1)
           <= lax.broadcasted_iota(jnp.int32, (tm, tm), 0)).astype(MXU_DTYPE)
    row = lambda i: (i, 0)
    att = jax.ShapeDtypeStruct((t, ATT_WIDTH), MXU_DTYPE)
    chunks = [min(tile, seq) for tile in (FOX_TILE, SB_TILE)]
    att_t = [jax.ShapeDtypeStruct((t // c, ATT_WIDTH, c), MXU_DTYPE) for c in chunks]
    wide = jax.ShapeDtypeStruct((t, LRU_WIDTH), F32)
    att_spec = pl.BlockSpec((tm, ATT_WIDTH), row)
    att_t_spec = [pl.BlockSpec((tm // c, ATT_WIDTH, c), lambda i: (i, 0, 0)) for c in chunks]
    wide_spec = pl.BlockSpec((tm, LRU_WIDTH), row)
    return pl.pallas_call(
        functools.partial(_inproj_kernel, tiles_per_seq=tiles_per_seq),
        grid=(t // tm,),
        in_specs=[
            pl.BlockSpec((tm, d), row),
            pl.BlockSpec((None, 6, d), lambda i: (i // tiles_per_seq, 0, 0)),
            pl.BlockSpec(w_main.shape, lambda i: (0, 0)),
            pl.BlockSpec(w_vt.shape, lambda i: (0, 0)),
            pl.BlockSpec((1, LANES), lambda i: (0, 0)),
            pl.BlockSpec((tm, tm), lambda i: (0, 0)),
        ],
        out_specs=[att_spec] * 4 + [wide_spec] * 2 + att_t_spec + [
            pl.BlockSpec((tm, LANES), row),
            pl.BlockSpec((None, N_HEADS, tm),
                         lambda i: (i // tiles_per_seq, 0, i % tiles_per_seq)),
        ],
        out_shape=[att] * 4 + [wide] * 2 + att_t + [
            jax.ShapeDtypeStruct((t, LANES), F32),
            jax.ShapeDtypeStruct((n_batch, N_HEADS, seq), F32),
        ],
        scratch_shapes=[pltpu.VMEM((8, LANES), F32)],
        compiler_params=_params(1),
        name="inproj",
    )(x2d, ada_l, w_main, w_vt, bf_pad, tri)


def _head_query(q2, c):
    lane = lax.broadcasted_iota(jnp.int32, (1, LANES), 1)
    in_head = (lane >= c * HEAD_DIM) & (lane < (c + 1) * HEAD_DIM)
    return jnp.where(in_head, q2, jnp.zeros_like(q2)) * ATT_SCALE


def _pair_cols(head):
    return slice((head // 2) * LANES, (head // 2 + 1) * LANES)


def _store_heads(o_ref, heads, outs_t):
    sub = lax.broadcasted_iota(jnp.int32, (LANES, 1), 0)
    for n in range(0, len(heads), 2):
        pair_t = jnp.where(sub < HEAD_DIM, outs_t[n], outs_t[n + 1])
        o_ref[:, _pair_cols(heads[n])] = pair_t.T.astype(o_ref.dtype)


def _fox_kernel(q_ref, k_ref, vt_ref, fcol_ref, frow_ref, o_ref):
    i = pl.program_id(1)
    tq = q_ref.shape[0]
    key = lax.broadcasted_iota(jnp.int32, (tq, tq), 0)
    qry = lax.broadcasted_iota(jnp.int32, (tq, tq), 1)
    causal = key <= qry
    for g in range(N_HEADS // ATT_GROUP):
        heads = list(range(g * ATT_GROUP, (g + 1) * ATT_GROUP))
        qms = [_head_query(q_ref[:, _pair_cols(hd)], hd % 2) for hd in heads]
        fqs = [frow_ref[hd, pl.ds(i, 1), :] for hd in heads]

        def qk(j, heads=heads, qms=qms):
            start = pl.multiple_of(j * tq, tq)
            return tuple(_nt_dot(k_ref[pl.ds(start, tq), _pair_cols(hd)], qms[n])
                         for n, hd in enumerate(heads))

        def block(j, scores, carry, masked, heads=heads, fqs=fqs):
            start = pl.multiple_of(j * tq, tq)
            stats, probs = [], []
            for n, hd in enumerate(heads):
                m, l, _ = carry[n]
                fk = fcol_ref[pl.ds(start, tq), hd:hd + 1]
                s = scores[n] - fk
                if masked:
                    s = jnp.where(causal, s, -jnp.inf)
                m_new = jnp.maximum(m, jnp.max(s, axis=0, keepdims=True) + fqs[n])
                alpha = jnp.exp(m - m_new)
                pm = jnp.exp(s + (fqs[n] - m_new))
                l = alpha * l + jnp.sum(pm, axis=0, keepdims=True)
                stats.append((m_new, l, alpha))
                probs.append(pm.astype(MXU_DTYPE))
            pvs = [jnp.dot(vt_ref[j, _pair_cols(hd), :], probs[n],
                           preferred_element_type=F32) for n, hd in enumerate(heads)]
            return tuple((stats[n][0], stats[n][1], stats[n][2] * carry[n][2] + pvs[n])
                         for n in range(len(heads)))

        init = tuple((jnp.full((1, tq), -jnp.inf, F32), jnp.zeros((1, tq), F32),
                      jnp.zeros((LANES, tq), F32)) for _ in heads)
        carry = lax.fori_loop(
            0, i, lambda j, cr, qk=qk, block=block: block(j, qk(j), cr, False), init)
        carry = block(i, qk(i), carry, True)
        _store_heads(o_ref, heads, [acc / l for _, l, acc in carry])


def _att_specs(t, seq, tile):
    n_batch = t // seq
    tq = min(tile, seq)
    nq = seq // tq
    q_spec = pl.BlockSpec((tq, ATT_WIDTH), lambda b, i: (b * nq + i, 0))
    k_spec = pl.BlockSpec((seq, ATT_WIDTH), lambda b, i: (b, 0))
    vt_spec = pl.BlockSpec((nq, ATT_WIDTH, tq), lambda b, i: (b, 0, 0))
    return n_batch, tq, nq, q_spec, k_spec, vt_spec


def _fox(q, k, vt, fcol, frow4, seq):
    t = q.shape[0]
    n_batch, tq, nq, q_spec, k_spec, vt_spec = _att_specs(t, seq, FOX_TILE)
    return pl.pallas_call(
        _fox_kernel,
        grid=(n_batch, nq),
        in_specs=[
            q_spec, k_spec, vt_spec,
            pl.BlockSpec((seq, LANES), lambda b, i: (b, 0)),
            pl.BlockSpec((None, N_HEADS, nq, tq), lambda b, i: (b, 0, 0, 0)),
        ],
        out_specs=q_spec,
        out_shape=jax.ShapeDtypeStruct((t, ATT_WIDTH), MXU_DTYPE),
        compiler_params=_params(2),
        name="fox_attention",
    )(q, k, vt, fcol, frow4)


def _sb_kernel(q_ref, k_ref, vt_ref, suf_ref, o_ref):
    i = pl.program_id(1)
    tq = q_ref.shape[0]
    key = lax.broadcasted_iota(jnp.int32, (tq, tq), 0)
    qry = lax.broadcasted_iota(jnp.int32, (tq, tq), 1)
    strict = key < qry
    suf = suf_ref[...]
    for g in range(N_HEADS // ATT_GROUP):
        heads = list(range(g * ATT_GROUP, (g + 1) * ATT_GROUP))
        qms = [_head_query(q_ref[:, _pair_cols(hd)], hd % 2) for hd in heads]

        def qk(j, heads=heads, qms=qms):
            start = pl.multiple_of(j * tq, tq)
            return tuple(_nt_dot(k_ref[pl.ds(start, tq), _pair_cols(hd)], qms[n])
                         for n, hd in enumerate(heads))

        def block(j, zs, carry, masked, heads=heads):
            log_betas, splits, laters = [], [], []
            for n in range(len(heads)):
                z = zs[n]
                sp = jnp.maximum(z, 0.0) + jnp.log(1.0 + jnp.exp2(jnp.abs(z) * -LOG2_E))
                log_betas.append(z - sp)
                if masked:
                    sp = jnp.where(strict, sp, 0.0)
                splits.append(sp.astype(MXU_DTYPE))
                laters.append(carry[n][0] + jnp.sum(sp, axis=0, keepdims=True))
            afters = [jnp.dot(suf, splits[n], preferred_element_type=F32)
                      for n in range(len(heads))]
            ws = []
            for n in range(len(heads)):
                w = jnp.exp(log_betas[n] - jnp.maximum(afters[n] + carry[n][0], 0.0))
                if masked:
                    w = jnp.where(strict, w, 0.0)
                ws.append(w.astype(MXU_DTYPE))
            return tuple(
                (laters[n], carry[n][1] + jnp.dot(vt_ref[j, _pair_cols(hd), :], ws[n],
                                                  preferred_element_type=F32))
                for n, hd in enumerate(heads))

        init = tuple((jnp.zeros((1, tq), F32), jnp.zeros((LANES, tq), F32)) for _ in heads)

        carry = block(i, qk(i), init, True)
        carry = lax.fori_loop(
            0, i, lambda s, cr, qk=qk, block=block: block(i - 1 - s, qk(i - 1 - s), cr, False),
            carry)
        _store_heads(o_ref, heads, [acc for _, acc in carry])


def _sb(q, k, vt, seq):
    t = q.shape[0]
    n_batch, tq, nq, q_spec, k_spec, vt_spec = _att_specs(t, seq, SB_TILE)
    suf = (lax.broadcasted_iota(jnp.int32, (tq, tq), 1)
           > lax.broadcasted_iota(jnp.int32, (tq, tq), 0)).astype(MXU_DTYPE)
    return pl.pallas_call(
        _sb_kernel,
        grid=(n_batch, nq),
        in_specs=[q_spec, k_spec, vt_spec, pl.BlockSpec((tq, tq), lambda b, i: (0, 0))],
        out_specs=q_spec,
        out_shape=jax.ShapeDtypeStruct((t, ATT_WIDTH), MXU_DTYPE),
        compiler_params=_params(2),
        name="sb_attention",
    )(q, k, vt, suf)


def _lru_kernel(xc_ref, gc_ref, cw_ref, cb_ref, wa_ref, ba_ref, wx_ref, bx_ref, lam_ref,
                o_ref, ext_ref, a_ref, u_ref, h_ref, state_ref):
    s_idx = pl.program_id(1)
    ts = xc_ref.shape[0]
    halo = 8

    @pl.when(s_idx == 0)
    def _():
        ext_ref[0:halo, :] = jnp.zeros((halo, LRU_WIDTH), F32)
        state_ref[...] = jnp.zeros_like(state_ref)

    ext_ref[halo:halo + ts, :] = xc_ref[...]
    xconv = cb_ref[...]
    for j in range(CONV_WIDTH):
        lo = halo - (CONV_WIDTH - 1) + j
        xconv = xconv + cw_ref[j:j + 1, :] * ext_ref[lo:lo + ts, :]
    ext_ref[0:halo, :] = ext_ref[ts:ts + halo, :]

    xb = xconv.astype(MXU_DTYPE)
    r = jax.nn.sigmoid(jnp.dot(xb, wa_ref[...], preferred_element_type=F32) + ba_ref[...])
    ig = jax.nn.sigmoid(jnp.dot(xb, wx_ref[...], preferred_element_type=F32) + bx_ref[...])
    log_a = (-LRU_C * _softplus(-lam_ref[...])) * r
    a = jnp.exp(log_a)
    a_ref[...] = a
    u_ref[...] = jnp.sqrt(1.0 - a * a) * (ig * xconv)

    def step(t, h):
        h = a_ref[pl.ds(t, 1), :] * h + u_ref[pl.ds(t, 1), :]
        h_ref[pl.ds(t, 1), :] = h
        return h

    state_ref[0:1, :] = lax.fori_loop(0, ts, step, state_ref[0:1, :], unroll=8)
    o_ref[...] = (h_ref[...] * jax.nn.gelu(gc_ref[...])).astype(o_ref.dtype)


def _lru(xc, gc, conv_w, conv_b, wa_bd, ba, wx_bd, bx, lam, seq):
    t, w = xc.shape
    n_batch = t // seq
    ts = min(LRU_TILE, seq)
    ns = seq // ts
    row_spec = pl.BlockSpec((ts, w), lambda b, s: (b * ns + s, 0))
    vec_spec = pl.BlockSpec((1, w), lambda b, s: (0, 0))
    mat_spec = pl.BlockSpec((w, w), lambda b, s: (0, 0))
    return pl.pallas_call(
        _lru_kernel,
        grid=(n_batch, ns),
        in_specs=[row_spec, row_spec,
                  pl.BlockSpec((CONV_WIDTH, w), lambda b, s: (0, 0)), vec_spec,
                  mat_spec, vec_spec, mat_spec, vec_spec, vec_spec],
        out_specs=row_spec,
        out_shape=jax.ShapeDtypeStruct((t, w), MXU_DTYPE),
        scratch_shapes=[pltpu.VMEM((ts + 8, w), F32), pltpu.VMEM((ts, w), F32),
                        pltpu.VMEM((ts, w), F32), pltpu.VMEM((ts, w), F32),
                        pltpu.VMEM((8, w), F32)],
        compiler_params=_params(2),
        name="rg_lru",
    )(xc, gc, conv_w, conv_b, wa_bd, ba, wx_bd, bx, lam)


def _merge_kernel(x_ref, ada_ref, oa_ref, ob_ref, oc_ref, wg_ref, bg_ref, wpa_ref, wpb_ref,
                  wpc_ref, wo_ref, lng_ref, lnb_ref, wr_ref, br_ref,
                  x1_ref, h2_ref, logit_ref):
    d = x_ref.shape[1]
    x = x_ref[...]
    shift1, scale1, gate1 = ada_ref[0:1, :], ada_ref[1:2, :], ada_ref[2:3, :]
    shift2, scale2 = ada_ref[3:4, :], ada_ref[4:5, :]
    h = (x * (1.0 + scale1) + shift1).astype(MXU_DTYPE)
    merged = None
    for n, (o_ref, w_ref) in enumerate(((oa_ref, wpa_ref), (ob_ref, wpb_ref), (oc_ref, wpc_ref))):
        cs = slice(n * d, (n + 1) * d)
        g = jax.nn.sigmoid(jnp.dot(h, wg_ref[:, cs], preferred_element_type=F32) + bg_ref[:, cs])
        term = g * jnp.dot(o_ref[...], w_ref[...], preferred_element_type=F32)
        merged = term if merged is None else merged + term
    y = jnp.dot(merged.astype(MXU_DTYPE), wo_ref[...], preferred_element_type=F32)
    x1 = _layer_norm(DEEPNORM_ALPHA * x + (1.0 + gate1) * y, lng_ref[...], lnb_ref[...])
    x1_ref[...] = x1
    h2 = x1 * (1.0 + scale2) + shift2
    h2_ref[...] = _pack_halves(h2)
    logit_ref[...] = _nt_dot(wr_ref[...], h2.astype(MXU_DTYPE)) + br_ref[...]


def _merge(x2d, ada_l, o_a, o_b, o_c, w_gate, b_gate, w_pa, w_pb, w_pc, w_o, ln_g, ln_b,
           w_router_t, b_router, seq):
    t, d = x2d.shape
    tm = min(ROW_TILE, seq)
    tiles_per_seq = seq // tm
    row = lambda i: (i, 0)
    whole = lambda a: pl.BlockSpec(a.shape, lambda i: (0,) * a.ndim)
    return pl.pallas_call(
        _merge_kernel,
        grid=(t // tm,),
        in_specs=[
            pl.BlockSpec((tm, d), row),
            pl.BlockSpec((None, 6, d), lambda i: (i // tiles_per_seq, 0, 0)),
            pl.BlockSpec((tm, ATT_WIDTH), row), pl.BlockSpec((tm, ATT_WIDTH), row),
            pl.BlockSpec((tm, LRU_WIDTH), row),
            whole(w_gate), whole(b_gate), whole(w_pa), whole(w_pb), whole(w_pc), whole(w_o),
            whole(ln_g), whole(ln_b), whole(w_router_t), whole(b_router),
        ],
        out_specs=[pl.BlockSpec((tm, d), row), pl.BlockSpec((tm, d // 2), row),
                   pl.BlockSpec((N_EXPERTS, tm), lambda i: (0, i))],
        out_shape=[jax.ShapeDtypeStruct((t, d), F32),
                   jax.ShapeDtypeStruct((t, d // 2), jnp.uint32),
                   jax.ShapeDtypeStruct((N_EXPERTS, t), F32)],
        compiler_params=_params(1),
        name="merge_outproj_ln",
    )(x2d, ada_l, o_a, o_b, o_c, w_gate, b_gate, w_pa, w_pb, w_pc, w_o, ln_g, ln_b,
      w_router_t, b_router)


def _route_kernel(logit_ref, tri_ref, eid_ref, gate_ref, pos_ref, cnt_ref, carry_ref):
    i = pl.program_id(0)
    tr = logit_ref.shape[1]

    @pl.when(i == 0)
    def _():
        carry_ref[...] = jnp.zeros_like(carry_ref)

    erow = lax.broadcasted_iota(jnp.int32, (N_EXPERTS, tr), 0)
    cur = logit_ref[...]
    vals, ids = [], []
    for _ in range(TOP_K):
        m = jnp.max(cur, axis=0, keepdims=True)
        idx = jnp.min(jnp.where(cur == m, erow, N_EXPERTS), axis=0, keepdims=True)
        vals.append(m)
        ids.append(idx)
        cur = jnp.where(erow == idx, -jnp.inf, cur)
    exps = [jnp.exp(v - vals[0]) for v in vals]
    denom = exps[0] + exps[1] + exps[2] + exps[3]
    chosen = jnp.zeros((N_EXPERTS, tr), F32)
    for idx in ids:
        chosen = chosen + jnp.where(erow == idx, 1.0, 0.0)
    prefix = (jnp.dot(chosen.astype(MXU_DTYPE), tri_ref[...], preferred_element_type=F32)
              + carry_ref[:, 0:1])
    for k in range(TOP_K):
        eid_ref[k:k + 1, :] = ids[k]
        gate_ref[k:k + 1, :] = exps[k] / denom
        pos_ref[k:k + 1, :] = jnp.sum(jnp.where(erow == ids[k], prefix, 0.0), axis=0,
                                      keepdims=True).astype(jnp.int32)
    carry_ref[...] = carry_ref[...] + jnp.sum(chosen, axis=1, keepdims=True)
    cnt_ref[...] = carry_ref[...].astype(jnp.int32)


def _route(logits_t):
    n_e, t = logits_t.shape
    tr = min(ROUTE_TILE, t)
    tri = (lax.broadcasted_iota(jnp.int32, (tr, tr), 0)
           < lax.broadcasted_iota(jnp.int32, (tr, tr), 1)).astype(MXU_DTYPE)
    tok = lambda i: (0, i)
    return pl.pallas_call(
        _route_kernel,
        grid=(t // tr,),
        in_specs=[pl.BlockSpec((n_e, tr), tok), pl.BlockSpec((tr, tr), lambda i: (0, 0))],
        out_specs=[pl.BlockSpec((TOP_K, tr), tok)] * 3
        + [pl.BlockSpec((n_e, LANES), lambda i: (0, 0))],
        out_shape=[jax.ShapeDtypeStruct((TOP_K, t), jnp.int32),
                   jax.ShapeDtypeStruct((TOP_K, t), F32),
                   jax.ShapeDtypeStruct((TOP_K, t), jnp.int32),
                   jax.ShapeDtypeStruct((n_e, LANES), jnp.int32)],
        scratch_shapes=[pltpu.VMEM((n_e, LANES), F32)],
        compiler_params=_params(1),
        name="route_topk",
    )(logits_t, tri)


def _dest_kernel(start_ref, eid_ref, pos_ref, dest_ref):
    eid = eid_ref[...]
    dest = pos_ref[...]
    for e in range(N_EXPERTS):
        dest = jnp.where(eid == e, dest + start_ref[e], dest)
    dest_ref[...] = dest


def _dest(pad_start, eid, pos):
    k, t = eid.shape
    tr = min(ROUTE_TILE, t)
    spec = pl.BlockSpec((k, tr), lambda i, s: (0, i))
    return pl.pallas_call(
        _dest_kernel,
        grid_spec=pltpu.PrefetchScalarGridSpec(
            num_scalar_prefetch=1, grid=(t // tr,), in_specs=[spec, spec], out_specs=spec),
        out_shape=jax.ShapeDtypeStruct((k, t), jnp.int32),
        compiler_params=_params(1),
        name="route_dest",
    )(pad_start, eid, pos)


def _sc_mesh():
    return plsc.VectorSubcoreMesh(core_axis_name="c", subcore_axis_name="s")


def _sc_worker():
    return lax.axis_index("s") * SC_CORES + lax.axis_index("c")


def _sc_scatter_rows(rows, idx, n_out):
    t, width = rows.shape
    per_worker = t // (SC_CORES * SC_SUBCORES)
    n_chunks = per_worker // SC_CHUNK

    @functools.partial(
        pl.kernel, mesh=_sc_mesh(),
        out_type=jax.ShapeDtypeStruct((n_out, width), rows.dtype),
        scratch_types=[pltpu.VMEM((SC_CHUNK,), jnp.int32),
                       pltpu.VMEM((SC_CHUNK, width), rows.dtype),
                       pltpu.SemaphoreType.DMA],
    )
    def scatter(rows_hbm, idx_hbm, out_hbm, idx_v, rows_v, sem):
        base = _sc_worker() * per_worker

        @pl.loop(0, n_chunks)
        def _(g):
            off = base + g * SC_CHUNK
            pltpu.sync_copy(rows_hbm.at[pl.ds(off, SC_CHUNK)], rows_v)
            for k in range(TOP_K):
                pltpu.sync_copy(idx_hbm.at[pl.ds(k * t + off, SC_CHUNK)], idx_v)
                pltpu.async_copy(rows_v, out_hbm.at[idx_v], sem).wait()

    return scatter(rows, idx)


def _sc_gather_rows(table, idx):
    n_rows = idx.shape[0]
    width = table.shape[1]
    per_worker = n_rows // (SC_CORES * SC_SUBCORES)
    n_chunks = per_worker // SC_CHUNK

    @functools.partial(
        pl.kernel, mesh=_sc_mesh(),
        out_type=jax.ShapeDtypeStruct((n_rows, width), table.dtype),
        scratch_types=[pltpu.VMEM((SC_CHUNK,), jnp.int32),
                       pltpu.VMEM((SC_CHUNK, width), table.dtype),
                       pltpu.SemaphoreType.DMA],
    )
    def gather(table_hbm, idx_hbm, out_hbm, idx_v, rows_v, sem):
        base = _sc_worker() * per_worker

        @pl.loop(0, n_chunks)
        def _(g):
            off = base + g * SC_CHUNK
            pltpu.sync_copy(idx_hbm.at[pl.ds(off, SC_CHUNK)], idx_v)
            pltpu.async_copy(table_hbm.at[idx_v], rows_v, sem).wait()
            pltpu.sync_copy(rows_v, out_hbm.at[pl.ds(off, SC_CHUNK)])

    return gather(table, idx)


def _expert_kernel(be_ref, valid_ref, x_ref, wgu_ref, bgu_ref, wd_ref, bd_ref, y_ref,
                   wgu_lp, wd_lp):
    i = pl.program_id(0)
    d_e = wd_ref.shape[0]
    n_valid = valid_ref[i]

    @pl.when((i == 0) | (be_ref[i] != be_ref[jnp.maximum(i - 1, 0)]))
    def _():
        wgu_lp[...] = wgu_ref[...].astype(MXU_DTYPE)
        wd_lp[...] = wd_ref[...].astype(MXU_DTYPE)

    @pl.when(n_valid > 0)
    def _():
        row = lax.broadcasted_iota(jnp.int32, x_ref.shape, 0)
        lo, hi = _unpack_halves(jnp.where(row < n_valid, x_ref[...], jnp.uint32(0)))
        x = jnp.concatenate([lo, hi], axis=1).astype(MXU_DTYPE)
        gu = jnp.dot(x, wgu_lp[...], preferred_element_type=F32) + bgu_ref[...]
        gate = jnp.minimum(gu[:, :d_e], SWIGLU_LIMIT)
        up = jnp.clip(gu[:, d_e:], -SWIGLU_LIMIT, SWIGLU_LIMIT)
        act = (up + 1.0) * (gate * jax.nn.sigmoid(SWIGLU_ALPHA * gate))
        y = jnp.dot(act.astype(MXU_DTYPE), wd_lp[...], preferred_element_type=F32) + bd_ref[...]
        y_ref[...] = _pack_halves(y)

    @pl.when(n_valid == 0)
    def _():
        y_ref[...] = jnp.zeros_like(y_ref)


def _experts(layer, block_e, valid, x_buf, w_gu, b_gu, w_down, b_down):
    n_slots, half = x_buf.shape
    _, n_e, d, d2 = w_gu.shape
    d_e = w_down.shape[2]
    n_blocks = n_slots // EXPERT_BLOCK
    n_layers = w_gu.shape[0]
    return pl.pallas_call(
        _expert_kernel,
        grid_spec=pltpu.PrefetchScalarGridSpec(
            num_scalar_prefetch=2,
            grid=(n_blocks,),
            in_specs=[
                pl.BlockSpec((EXPERT_BLOCK, half), lambda i, be, nu: (i, 0)),
                pl.BlockSpec((None, None, d, d2), lambda i, be, nu: (layer, be[i], 0, 0)),
                pl.BlockSpec((None, None, 1, d2), lambda i, be, nu: (layer, be[i], 0, 0)),
                pl.BlockSpec((None, None, d_e, d), lambda i, be, nu: (layer, be[i], 0, 0)),
                pl.BlockSpec((None, None, 1, d), lambda i, be, nu: (layer, be[i], 0, 0)),
            ],
            out_specs=pl.BlockSpec((EXPERT_BLOCK, half), lambda i, be, nu: (i, 0)),
            scratch_shapes=[pltpu.VMEM((d, d2), MXU_DTYPE), pltpu.VMEM((d_e, d), MXU_DTYPE)],
        ),
        out_shape=jax.ShapeDtypeStruct((n_slots, half), jnp.uint32),
        compiler_params=_params(1),
        name="moe_experts",
    )(block_e, valid, x_buf, w_gu, b_gu.reshape(n_layers, n_e, 1, d2), w_down,
      b_down.reshape(n_layers, n_e, 1, d))


def _combine_kernel(x1_ref, ada_ref, gate_ref, lng_ref, lnb_ref, *rest):
    y_refs, o_ref = rest[:TOP_K], rest[-1]
    gates = gate_ref[...]
    y_lo = y_hi = None
    for k in range(TOP_K):
        lo, hi = _unpack_halves(y_refs[k][...])
        g = gates[:, k:k + 1]
        y_lo = lo * g if y_lo is None else y_lo + lo * g
        y_hi = hi * g if y_hi is None else y_hi + hi * g
    y = jnp.concatenate([y_lo, y_hi], axis=1)
    gate2 = ada_ref[5:6, :]
    o_ref[...] = _layer_norm(DEEPNORM_ALPHA * x1_ref[...] + (1.0 + gate2) * y,
                             lng_ref[...], lnb_ref[...])


def _combine(x1, ada_l, gates_tk, ln_g, ln_b, y_rows, seq, part, n_parts, earlier):
    t, d = x1.shape
    tm = min(MOVE_TILE, seq)
    tiles_per_seq = seq // tm
    steps = t // tm // n_parts
    first = part * steps
    tok = lambda i: (first + i, 0)
    in_specs = [
        pl.BlockSpec((tm, d), tok),
        pl.BlockSpec((None, 6, d), lambda i: ((first + i) // tiles_per_seq, 0, 0)),
        pl.BlockSpec((tm, TOP_K), tok),
        pl.BlockSpec((1, d), lambda i: (0, 0)),
        pl.BlockSpec((1, d), lambda i: (0, 0)),
    ] + [
        pl.BlockSpec((tm, d // 2), lambda i, k=k: (k * steps + i, 0)) for k in range(TOP_K)
    ]
    args = [x1, ada_l, gates_tk, ln_g, ln_b] + [y_rows] * TOP_K
    aliases = {}
    if earlier is not None:
        in_specs.append(pl.BlockSpec(memory_space=pl.ANY))
        aliases = {len(args): 0}
        args.append(earlier)
    return pl.pallas_call(
        _combine_kernel,
        grid=(steps,),
        in_specs=in_specs,
        out_specs=pl.BlockSpec((tm, d), tok),
        out_shape=jax.ShapeDtypeStruct((t, d), F32),
        input_output_aliases=aliases,
        compiler_params=_params(1),
        name="moe_combine_ln",
    )(*args)


def _split_w_in(w_in):
    d = w_in.shape[0]
    sizes = (ATT_WIDTH, ATT_WIDTH, ATT_WIDTH, N_HEADS, ATT_WIDTH, ATT_WIDTH, ATT_WIDTH,
             LRU_WIDTH, LRU_WIDTH)
    bounds = [sum(sizes[:n]) for n in range(len(sizes) + 1)]
    qa, ka, va, fa, qb, kb, vb, xc, gc = (
        w_in[:, bounds[n]:bounds[n + 1]] for n in range(len(sizes)))
    forget = jnp.concatenate([fa, jnp.zeros((d, LANES - N_HEADS), w_in.dtype)], axis=1)
    w_main = jnp.concatenate([qa, ka, qb, kb, xc, gc, forget], axis=1).astype(MXU_DTYPE)
    w_vt = jnp.concatenate([va, vb], axis=1).T.astype(MXU_DTYPE)
    return w_main, w_vt


def _block_diag(w):
    n, c, dd = w.shape
    eye = jnp.eye(n, dtype=w.dtype)
    return (eye[:, None, :, None] * w[:, :, None, :]).reshape(n * c, n * dd).astype(MXU_DTYPE)


def kernel(x, c, w_ada, b_ada, ln1_g, ln1_b, w_in, b_f, conv_w, conv_b, lru_wa, lru_ba, lru_wx,
           lru_bx, lru_lambda, w_gate, b_gate, w_pa, w_pb, w_pc, w_o, ln2_g, ln2_b, w_router,
           b_router, w_gu, b_gu, w_down, b_down):
    n_batch, seq, d = x.shape
    t = n_batch * seq
    n_layers = w_ada.shape[0]
    n_blocks = (t * TOP_K) // EXPERT_BLOCK + N_EXPERTS
    n_slots = n_blocks * EXPERT_BLOCK
    att_tile = min(FOX_TILE, seq)
    vec = lambda a: a.reshape(1, -1)

    ada = _ada(c, w_ada, b_ada).reshape(n_layers, n_batch, 6, d)
    x2d = x.reshape(t, d)
    for l in range(n_layers):
        ada_l = ada[l]
        bf_pad = jnp.concatenate([b_f[l], jnp.zeros((LANES - N_HEADS,), F32)]).reshape(1, LANES)
        w_main, w_vt = _split_w_in(w_in[l])
        qa, ka, qb, kb, xc, gc, vat, vbt, fcol, frow = _inproj(
            x2d, ada_l, w_main, w_vt, bf_pad, seq)
        frow4 = frow.reshape(n_batch, N_HEADS, seq // att_tile, att_tile)
        o_a = _fox(qa, ka, vat, fcol, frow4, seq)
        o_b = _sb(qb, kb, vbt, seq)
        o_c = _lru(xc, gc, conv_w[l], vec(conv_b[l]), _block_diag(lru_wa[l]), vec(lru_ba[l]),
                   _block_diag(lru_wx[l]), vec(lru_bx[l]), vec(lru_lambda[l]), seq)
        x1, h2, logits_t = _merge(
            x2d, ada_l, o_a, o_b, o_c, w_gate[l].astype(MXU_DTYPE), vec(b_gate[l]),
            w_pa[l].astype(MXU_DTYPE), w_pb[l].astype(MXU_DTYPE), w_pc[l].astype(MXU_DTYPE),
            w_o[l].astype(MXU_DTYPE), vec(ln1_g[l]), vec(ln1_b[l]),
            w_router[l].T.astype(MXU_DTYPE), b_router[l].reshape(N_EXPERTS, 1), seq)

        eid, gates, pos, cnt = _route(logits_t)
        counts = cnt[:, 0]
        padded = (counts + EXPERT_BLOCK - 1) // EXPERT_BLOCK * EXPERT_BLOCK
        pad_end = jnp.cumsum(padded)
        pad_start = (pad_end - padded).astype(jnp.int32)
        block_first = (jnp.arange(n_blocks) * EXPERT_BLOCK)[:, None]
        block_e = jnp.minimum(jnp.sum(pad_end[None, :] <= block_first, axis=1),
                              N_EXPERTS - 1).astype(jnp.int32)
        valid = jnp.clip(counts[block_e] + pad_start[block_e] - block_first[:, 0],
                         0, EXPERT_BLOCK).astype(jnp.int32)
        dest = _dest(pad_start, eid, pos)

        x_buf = _sc_scatter_rows(h2, dest.reshape(-1), n_slots)
        y_buf = _experts(l, block_e, valid, x_buf, w_gu, b_gu, w_down, b_down)
        t_part = t // COMBINE_PARTS
        x2d = None
        for part in range(COMBINE_PARTS):
            idx = dest[:, part * t_part:(part + 1) * t_part].reshape(-1)
            y_rows = _sc_gather_rows(y_buf, idx)
            x2d = _combine(x1, ada_l, gates.T, vec(ln2_g[l]), vec(ln2_b[l]), y_rows, seq,
                           part, COMBINE_PARTS, x2d)
    return x2d.reshape(n_batch, seq, d)
```
